```python
import jax
import jax.numpy as jnp
from jax import lax
import numpy as np

D_MODEL = 1024
BATCH = 8
SEQ = 4096
DEPTH = 1
DEC_BATCH = 128
DEC_SEQ = 4
PAST_LEN = 8192
PAGE_SIZE = 128

HEAD_DIM = 64
NSA_WIDTH = 3 * D_MODEL // 4
NSA_HEADS = NSA_WIDTH // HEAD_DIM
NSA_GROUPS = 2
NSA_HPG = NSA_HEADS // NSA_GROUPS
CMP_LEN = 32
CMP_STRIDE = 16
CMP_HID = 64
SLC_BLOCK = 64
N_SELECT = 16
WINDOW = 512
Q_BLOCK = 64
FORCE_SCORE = 1e4
RWKV_WIDTH = 3 * D_MODEL // 4
RWKV_HEAD_DIM = 64
RWKV_HEADS = RWKV_WIDTH // RWKV_HEAD_DIM
DECAY_LORA = 64
ICL_LORA = 64
GATE_LORA = 128
RWKV_SPLITS = (RWKV_WIDTH, RWKV_WIDTH, RWKV_WIDTH, DECAY_LORA, ICL_LORA, GATE_LORA)
RWKV_PROJ = 3 * RWKV_WIDTH + DECAY_LORA + ICL_LORA + GATE_LORA
GN_EPS = 64e-5
MEM_TOKENS = 256
MEM_HEADS = 4
MEM_WIDTH = D_MODEL // 2
MEM_HEAD_DIM = MEM_WIDTH // MEM_HEADS
N_BRANCHES = 3
D_FF = ((-(-8 * D_MODEL // 3) + 255) // 256) * 256
NSA_KV_COLS = 3 * 2 * NSA_GROUPS * HEAD_DIM
IN_SPLITS = (NSA_WIDTH, NSA_KV_COLS, 3 * NSA_HEADS, RWKV_PROJ, MEM_WIDTH, N_BRANCHES * D_MODEL)
IN_COLS = NSA_WIDTH + NSA_KV_COLS + 3 * NSA_HEADS + RWKV_PROJ + MEM_WIDTH + N_BRANCHES * D_MODEL
RMS_EPS = 1e-6

kernel_name = 'nsa_rwkv7_memory_hybrid_step'


def rms_norm(x, g):
    xf = x.astype(jnp.float32)
    y = xf * lax.rsqrt(jnp.mean(xf * xf, axis=-1, keepdims=True) + RMS_EPS)
    return (y * g.astype(jnp.float32)).astype(x.dtype)


def split_cols(z, sizes):
    parts, start = [], 0
    for s in sizes:
        parts.append(z[..., start:start + s])
        start += s
    return parts


def masked_softmax(s, mask):
    s = jnp.where(mask, s.astype(jnp.float32), -1e30)
    return jnp.where(mask, jax.nn.softmax(s, axis=-1), 0.0)


def nsa_kv_rows(kv):
    b, t = kv.shape[:2]
    kv = kv.reshape(b, t, 3, 2, NSA_GROUPS, HEAD_DIM)
    return [kv[:, :, br, j] for br in range(3) for j in range(2)]


def compress_rows(rows, pe, w1, b1, w2):
    b, L, g, hd = rows.shape
    n_chunks = L // CMP_STRIDE
    r = CMP_LEN // CMP_STRIDE
    nc = n_chunks - r + 1
    ch = rows[:, :n_chunks * CMP_STRIDE].reshape(b, n_chunks, CMP_STRIDE, g, hd)
    w1r = w1.reshape(r, CMP_STRIDE, hd, CMP_HID)
    u = jnp.einsum('bncgd,icdh->ibngh', ch, w1r)
    pre = u[0][:, 0:nc]
    for i in range(1, r):
        pre = pre + u[i][:, i:i + nc]
    pre = pre + (pe.reshape(-1) @ w1 + b1)
    return jax.nn.gelu(pre) @ w2


def block_overlap(nc, ns):
    i = jnp.arange(nc, dtype=jnp.int32)[:, None] * CMP_STRIDE
    j = jnp.arange(ns, dtype=jnp.int32)[None, :] * SLC_BLOCK
    return ((i < j + SLC_BLOCK) & (i + CMP_LEN > j)).astype(jnp.float32)


def to_blocks(rows):
    b, L, g, hd = rows.shape
    ns = -(-L // SLC_BLOCK)
    rows = jnp.pad(rows, ((0, 0), (0, ns * SLC_BLOCK - L), (0, 0), (0, 0)))
    return rows.reshape(b, ns, SLC_BLOCK, g, hd).transpose(0, 3, 1, 2, 4)


def nsa_attend(q, t_q, kc, vc, ksb, vsb, kw, vw, t_w, gate):
    b, nq, g, r, hd = q.shape
    nc, ns = kc.shape[1], ksb.shape[2]
    scale = HEAD_DIM ** -0.5
    s_c = jnp.einsum('bqgrd,bngd->bqgrn', q, kc).astype(jnp.float32) * scale
    c_end = jnp.arange(nc, dtype=jnp.int32) * CMP_STRIDE + (CMP_LEN - 1)
    p_c = masked_softmax(s_c, (c_end[None, :] <= t_q[:, None])[None, :, None, None, :])
    o_c = jnp.einsum('bqgrn,bngd->bqgrd', p_c.astype(vc.dtype), vc)
    imp = jnp.einsum('bqgn,ns->bqgs', p_c.sum(axis=3), block_overlap(nc, ns))
    j = jnp.arange(ns, dtype=jnp.int32)[None, :]
    cur = (t_q // SLC_BLOCK)[:, None]
    forced = (j == 0) | (j == cur) | (j == cur - 1)
    score = jnp.where(forced[None, :, None, :], FORCE_SCORE, imp)
    score = jnp.where((j <= cur)[None, :, None, :], score, -jnp.inf)
    top_s, idx = lax.top_k(score, min(N_SELECT, ns))
    b_ix = jnp.arange(b)[:, None, None, None]
    g_ix = jnp.arange(g)[None, None, :, None]
    kb = ksb[b_ix, g_ix, idx]
    vb = vsb[b_ix, g_ix, idx]
    s_s = jnp.einsum('bqgrd,bqgkcd->bqgrkc', q, kb).astype(jnp.float32) * scale
    pos = idx[..., None] * SLC_BLOCK + jnp.arange(SLC_BLOCK, dtype=jnp.int32)
    m_s = (pos <= t_q[None, :, None, None, None]) & jnp.isfinite(top_s)[..., None]
    nk = idx.shape[-1] * SLC_BLOCK
    p_s = masked_softmax(s_s.reshape(b, nq, g, r, nk), m_s.reshape(b, nq, g, 1, nk)).reshape(s_s.shape)
    o_s = jnp.einsum('bqgrkc,bqgkcd->bqgrd', p_s.astype(vb.dtype), vb)
    s_w = jnp.einsum('bqgrd,bkgd->bqgrk', q, kw).astype(jnp.float32) * scale
    diff = t_q[:, None] - t_w[None, :]
    m_w = ((diff >= 0) & (diff <= WINDOW) & (t_w >= 0)[None, :])[None, :, None, None, :]
    p_w = masked_softmax(s_w, m_w)
    o_w = jnp.einsum('bqgrk,bkgd->bqgrd', p_w.astype(vw.dtype), vw)
    gate = gate.astype(q.dtype)
    return gate[..., 0:1] * o_c + gate[..., 1:2] * o_s + gate[..., 2:3] * o_w


def nsa_gates(gn):
    b, t = gn.shape[:2]
    return jax.nn.sigmoid(gn.astype(jnp.float32)).astype(gn.dtype).reshape(b, t, NSA_GROUPS, NSA_HPG, 3)


def nsa_prompt(q, gate, kc, vc, ks_r, vs_r, kw_r, vw_r):
    b, t = q.shape[:2]
    nb = t // Q_BLOCK
    qb = jnp.moveaxis(q.reshape(b, nb, Q_BLOCK, NSA_GROUPS, NSA_HPG, HEAD_DIM), 1, 0)
    gb = jnp.moveaxis(gate.reshape(b, nb, Q_BLOCK, NSA_GROUPS, NSA_HPG, 3), 1, 0)
    starts = jnp.arange(nb, dtype=jnp.int32) * Q_BLOCK
    ksb, vsb = to_blocks(ks_r), to_blocks(vs_r)
    pad = ((0, 0), (WINDOW, 0), (0, 0), (0, 0))
    kw_pad, vw_pad = jnp.pad(kw_r, pad), jnp.pad(vw_r, pad)
    span = Q_BLOCK + WINDOW

    def one_block(args):
        q_blk, g_blk, start = args
        t_q = start + jnp.arange(Q_BLOCK, dtype=jnp.int32)
        t_w = start - WINDOW + jnp.arange(span, dtype=jnp.int32)
        kw = lax.dynamic_slice_in_dim(kw_pad, start, span, axis=1)
        vw = lax.dynamic_slice_in_dim(vw_pad, start, span, axis=1)
        return nsa_attend(q_blk, t_q, kc, vc, ksb, vsb, kw, vw, t_w, g_blk)

    out = lax.map(one_block, (qb, gb, starts))
    return jnp.moveaxis(out, 0, 1).reshape(b, t, NSA_WIDTH)


def rwkv_prep(p, p_prev, mu, w0, w2, a0, a2, g2, k_k, k_a):
    b, t = p.shape[:2]
    prev = jnp.concatenate([p_prev[:, None].astype(p.dtype), p[:, :-1]], axis=1)
    xm = p + (prev - p) * mu
    r, k, v, wd, ad, gd = split_cols(xm, RWKV_SPLITS)
    w = (w0 + jnp.tanh(wd) @ w2).astype(jnp.float32)
    decay = jnp.exp(-jnp.exp(-jax.nn.softplus(-w) - 0.5))
    a = jax.nn.sigmoid(a0 + ad @ a2)
    g = jax.nn.sigmoid(gd) @ g2
    heads = lambda z: z.reshape(b, t, RWKV_HEADS, RWKV_HEAD_DIM)
    kk = heads((k * k_k).astype(jnp.float32))
    kk = kk / jnp.maximum(jnp.sqrt(jnp.sum(kk * kk, axis=-1, keepdims=True)), 1e-12)
    k = k * (1.0 + (a - 1.0) * k_a)
    return heads(r), heads(decay), heads(k), heads(v), kk, heads(a), g


def rwkv_scan(s0, r, decay, k, v, kk, a):
    def step(s, inp):
        r_t, w_t, k_t, v_t, kk_t, a_t = inp
        sa = jnp.einsum('bhij,bhj->bhi', s, -kk_t)
        s = s * w_t[:, :, None, :] + sa[..., None] * (kk_t * a_t)[:, :, None, :] + v_t[..., None] * k_t[:, :, None, :]
        return s, jnp.einsum('bhij,bhj->bhi', s, r_t)
    seq = tuple(jnp.moveaxis(z.astype(jnp.float32), 1, 0) for z in (r, decay, k, v, kk, a))
    s, y = lax.scan(step, s0.astype(jnp.float32), seq)
    return s, jnp.moveaxis(y, 0, 1)


def rwkv_out(y, r, k, v, g, ln_g, ln_b, r_k):
    b, t = y.shape[:2]
    mean = jnp.mean(y, axis=-1, keepdims=True)
    var = jnp.mean(jnp.square(y - mean), axis=-1, keepdims=True)
    yn = (y - mean) * lax.rsqrt(var + GN_EPS)
    bonus = jnp.sum((r * k * r_k).astype(jnp.float32), axis=-1, keepdims=True) * v.astype(jnp.float32)
    out = (yn.reshape(b, t, RWKV_WIDTH) * ln_g + ln_b + bonus.reshape(b, t, RWKV_WIDTH)) * g
    return out.astype(g.dtype)


def mem_kv(mem, g, w):
    b, m = mem.shape[:2]
    k, v = split_cols(rms_norm(mem, g) @ w, (MEM_WIDTH, MEM_WIDTH))
    return k.reshape(b, m, MEM_HEADS, MEM_HEAD_DIM), v.reshape(b, m, MEM_HEADS, MEM_HEAD_DIM)


def mem_attend(q, mk, mv):
    b, t = q.shape[:2]
    qh = q.reshape(b, t, MEM_HEADS, MEM_HEAD_DIM)
    s = jnp.einsum('bthd,bmhd->bthm', qh, mk).astype(jnp.float32) * MEM_HEAD_DIM ** -0.5
    p = jax.nn.softmax(s, axis=-1).astype(mv.dtype)
    return jnp.einsum('bthm,bmhd->bthd', p, mv).reshape(b, t, MEM_WIDTH)


def merge_ffn(x, o_nsa, o_rwkv, o_mem, mg, w_o_nsa, w_o_rwkv, w_o_mem, w_out, ffn_norm, w_gate, w_up, w_down):
    b, t = x.shape[:2]
    g = jax.nn.sigmoid(mg.astype(jnp.float32)).astype(x.dtype).reshape(b, t, N_BRANCHES, D_MODEL)
    m = g[:, :, 0] * (o_nsa @ w_o_nsa) + g[:, :, 1] * (o_rwkv @ w_o_rwkv) + g[:, :, 2] * (o_mem @ w_o_mem)
    x = x + m @ w_out
    hf = rms_norm(x, ffn_norm)
    return x + (jax.nn.silu(hf @ w_gate) * (hf @ w_up)) @ w_down


def gather_pages(pool, page_table):
    rows = pool[page_table]
    b, n, ps = rows.shape[:3]
    return rows.reshape(b, n * ps, rows.shape[3], rows.shape[4])


def setup_inputs(seed: int = 0) -> dict:
    key = jax.random.key(seed)
    ks = list(jax.random.split(key, 80))

    def nrm(shape, scale):
        return jax.random.normal(ks.pop(), shape, jnp.float32) * scale

    n_pages = PAST_LEN // PAGE_SIZE
    n_used = DEC_BATCH * n_pages
    n_phys = n_used + n_used // 4
    lw = min(WINDOW, PAST_LEN)
    paged = (DEPTH, n_phys, PAGE_SIZE, NSA_GROUPS, HEAD_DIM)
    win = (DEPTH, DEC_BATCH, lw, NSA_GROUPS, HEAD_DIM)
    memc = (DEPTH, DEC_BATCH, MEM_TOKENS, MEM_HEADS, MEM_HEAD_DIM)
    perm = jax.random.permutation(ks.pop(), n_phys)
    page_table = perm[:n_used].reshape(DEC_BATCH, n_pages).astype(jnp.int32)
    cl = CMP_LEN * HEAD_DIM
    inp = {
        'x_prompt': nrm((BATCH, SEQ, D_MODEL), 1.0),
        'x_sample': nrm((DEC_BATCH, DEC_SEQ, D_MODEL), 1.0),
        'cache_cmp_k': nrm(paged, 1.0),
        'cache_cmp_v': nrm(paged, 1.0),
        'cache_slc_k': nrm(paged, 1.0),
        'cache_slc_v': nrm(paged, 1.0),
        'cache_win_k': nrm(win, 1.0),
        'cache_win_v': nrm(win, 1.0),
        'state_rwkv_shift': nrm((DEPTH, DEC_BATCH, RWKV_PROJ), 1.0),
        'state_rwkv_wkv': nrm((DEPTH, DEC_BATCH, RWKV_HEADS, RWKV_HEAD_DIM, RWKV_HEAD_DIM), 0.3),
        'cache_mem_k': nrm(memc, 1.0),
        'cache_mem_v': nrm(memc, 1.0),
        'page_table': page_table,
        'mem_prompt': nrm((BATCH, MEM_TOKENS, D_MODEL), 1.0),
        'attn_norm': 1.0 + nrm((DEPTH, D_MODEL), 0.02),
        'w_in': nrm((DEPTH, D_MODEL, IN_COLS), D_MODEL ** -0.5),
        'cmp_pe_k': nrm((DEPTH, CMP_LEN, HEAD_DIM), 0.1),
        'cmp_w1_k': nrm((DEPTH, cl, CMP_HID), cl ** -0.5),
        'cmp_b1_k': nrm((DEPTH, CMP_HID), 0.02),
        'cmp_w2_k': nrm((DEPTH, CMP_HID, HEAD_DIM), 1.5 * CMP_HID ** -0.5),
        'cmp_pe_v': nrm((DEPTH, CMP_LEN, HEAD_DIM), 0.1),
        'cmp_w1_v': nrm((DEPTH, cl, CMP_HID), cl ** -0.5),
        'cmp_b1_v': nrm((DEPTH, CMP_HID), 0.02),
        'cmp_w2_v': nrm((DEPTH, CMP_HID, HEAD_DIM), 1.5 * CMP_HID ** -0.5),
        'rwkv_mu': jax.random.uniform(ks.pop(), (DEPTH, RWKV_PROJ), jnp.float32),
        'rwkv_w0': nrm((DEPTH, RWKV_WIDTH), 0.5),
        'rwkv_w2': nrm((DEPTH, DECAY_LORA, RWKV_WIDTH), 0.1),
        'rwkv_a0': nrm((DEPTH, RWKV_WIDTH), 0.1),
        'rwkv_a2': nrm((DEPTH, ICL_LORA, RWKV_WIDTH), 0.1),
        'rwkv_g2': nrm((DEPTH, GATE_LORA, RWKV_WIDTH), GATE_LORA ** -0.5),
        'rwkv_kk': 0.85 + nrm((DEPTH, RWKV_WIDTH), 0.02),
        'rwkv_ka': 1.0 + nrm((DEPTH, RWKV_WIDTH), 0.02),
        'rwkv_rk': nrm((DEPTH, RWKV_HEADS, RWKV_HEAD_DIM), 0.1),
        'rwkv_ln_g': 1.0 + nrm((DEPTH, RWKV_WIDTH), 0.02),
        'rwkv_ln_b': nrm((DEPTH, RWKV_WIDTH), 0.02),
        'mem_norm': 1.0 + nrm((DEPTH, D_MODEL), 0.02),
        'w_mem_kv': nrm((DEPTH, D_MODEL, 2 * MEM_WIDTH), D_MODEL ** -0.5),
        'w_o_nsa': nrm((DEPTH, NSA_WIDTH, D_MODEL), NSA_WIDTH ** -0.5),
        'w_o_rwkv': nrm((DEPTH, RWKV_WIDTH, D_MODEL), RWKV_WIDTH ** -0.5),
        'w_o_mem': nrm((DEPTH, MEM_WIDTH, D_MODEL), MEM_WIDTH ** -0.5),
        'w_out': nrm((DEPTH, D_MODEL, D_MODEL), D_MODEL ** -0.5),
        'ffn_norm': 1.0 + nrm((DEPTH, D_MODEL), 0.02),
        'w_gate': nrm((DEPTH, D_MODEL, D_FF), D_MODEL ** -0.5),
        'w_up': nrm((DEPTH, D_MODEL, D_FF), D_MODEL ** -0.5),
        'w_down': nrm((DEPTH, D_FF, D_MODEL), D_FF ** -0.5),
        'final_norm': 1.0 + nrm((D_MODEL,), 0.02),
    }
    return inp


def reference(x_prompt, x_sample, cache_cmp_k, cache_cmp_v, cache_slc_k, cache_slc_v, cache_win_k, cache_win_v,
              state_rwkv_shift, state_rwkv_wkv, cache_mem_k, cache_mem_v, page_table, mem_prompt,
              attn_norm, w_in, cmp_pe_k, cmp_w1_k, cmp_b1_k, cmp_w2_k, cmp_pe_v, cmp_w1_v, cmp_b1_v, cmp_w2_v,
              rwkv_mu, rwkv_w0, rwkv_w2, rwkv_a0, rwkv_a2, rwkv_g2, rwkv_kk, rwkv_ka, rwkv_rk, rwkv_ln_g, rwkv_ln_b,
              mem_norm, w_mem_kv, w_o_nsa, w_o_rwkv, w_o_mem, w_out, ffn_norm, w_gate, w_up, w_down, final_norm):
    xp, xs = x_prompt, x_sample
    bp, t = xp.shape[:2]
    bs, tn = xs.shape[:2]
    lw = cache_win_k.shape[2]
    t_new = PAST_LEN + jnp.arange(tn, dtype=jnp.int32)
    t_w_s = PAST_LEN - lw + jnp.arange(lw + tn, dtype=jnp.int32)
    wp0 = max(t - WINDOW, 0)
    P = {n: [] for n in ('ck', 'cv', 'sk', 'sv', 'wk', 'wv', 'sh', 'wkv', 'mk', 'mv')}
    S = {n: [] for n in ('ck', 'cv', 'sk', 'sv', 'wk', 'wv', 'sh', 'wkv')}
    for l in range(DEPTH):
        cmp_k = (cmp_pe_k[l], cmp_w1_k[l], cmp_b1_k[l], cmp_w2_k[l])
        cmp_v = (cmp_pe_v[l], cmp_w1_v[l], cmp_b1_v[l], cmp_w2_v[l])
        rw_p = (rwkv_mu[l], rwkv_w0[l], rwkv_w2[l], rwkv_a0[l], rwkv_a2[l], rwkv_g2[l], rwkv_kk[l], rwkv_ka[l])
        out_p = (w_o_nsa[l], w_o_rwkv[l], w_o_mem[l], w_out[l], ffn_norm[l], w_gate[l], w_up[l], w_down[l])

        hp = rms_norm(xp, attn_norm[l])
        q, kv, gn, pr, mq, mg = split_cols(hp @ w_in[l], IN_SPLITS)
        kc_r, vc_r, ks_r, vs_r, kw_r, vw_r = nsa_kv_rows(kv)
        kc = compress_rows(kc_r, *cmp_k)
        vc = compress_rows(vc_r, *cmp_v)
        o_nsa = nsa_prompt(q, nsa_gates(gn), kc, vc, ks_r, vs_r, kw_r, vw_r)
        r_, w_, k_, v_, kk_, a_, g_ = rwkv_prep(pr, jnp.zeros((bp, RWKV_PROJ), pr.dtype), *rw_p)
        s_p, y_ = rwkv_scan(jnp.zeros((bp, RWKV_HEADS, RWKV_HEAD_DIM, RWKV_HEAD_DIM), jnp.float32),
                            r_, w_, k_, v_, kk_, a_)
        o_rwkv = rwkv_out(y_, r_, k_, v_, g_, rwkv_ln_g[l], rwkv_ln_b[l], rwkv_rk[l])
        mk, mv = mem_kv(mem_prompt, mem_norm[l], w_mem_kv[l])
        o_mem = mem_attend(mq, mk, mv)
        xp = merge_ffn(xp, o_nsa, o_rwkv, o_mem, mg, *out_p)
        for n, a in (('ck', kc_r), ('cv', vc_r), ('sk', ks_r), ('sv', vs_r), ('wk', kw_r[:, wp0:]),
                     ('wv', vw_r[:, wp0:]), ('sh', pr[:, -1]), ('wkv', s_p.astype(xp.dtype)), ('mk', mk), ('mv', mv)):
            P[n].append(a)

        hs = rms_norm(xs, attn_norm[l])
        q, kv, gn, pr, mq, mg = split_cols(hs @ w_in[l], IN_SPLITS)
        kc_n, vc_n, ks_n, vs_n, kw_n, vw_n = nsa_kv_rows(kv)
        kc = compress_rows(jnp.concatenate([gather_pages(cache_cmp_k[l], page_table), kc_n], axis=1), *cmp_k)
        vc = compress_rows(jnp.concatenate([gather_pages(cache_cmp_v[l], page_table), vc_n], axis=1), *cmp_v)
        ksb = to_blocks(jnp.concatenate([gather_pages(cache_slc_k[l], page_table), ks_n], axis=1))
        vsb = to_blocks(jnp.concatenate([gather_pages(cache_slc_v[l], page_table), vs_n], axis=1))
        kw = jnp.concatenate([cache_win_k[l], kw_n], axis=1)
        vw = jnp.concatenate([cache_win_v[l], vw_n], axis=1)
        qh = q.reshape(bs, tn, NSA_GROUPS, NSA_HPG, HEAD_DIM)
        o_nsa = nsa_attend(qh, t_new, kc, vc, ksb, vsb, kw, vw, t_w_s, nsa_gates(gn)).reshape(bs, tn, NSA_WIDTH)
        r_, w_, k_, v_, kk_, a_, g_ = rwkv_prep(pr, state_rwkv_shift[l], *rw_p)
        s_s, y_ = rwkv_scan(state_rwkv_wkv[l], r_, w_, k_, v_, kk_, a_)
        o_rwkv = rwkv_out(y_, r_, k_, v_, g_, rwkv_ln_g[l], rwkv_ln_b[l], rwkv_rk[l])
        o_mem = mem_attend(mq, cache_mem_k[l], cache_mem_v[l])
        xs = merge_ffn(xs, o_nsa, o_rwkv, o_mem, mg, *out_p)
        for n, a in (('ck', kc_n), ('cv', vc_n), ('sk', ks_n), ('sv', vs_n), ('wk', kw[:, tn:]),
                     ('wv', vw[:, tn:]), ('sh', pr[:, -1]), ('wkv', s_s.astype(xs.dtype))):
            S[n].append(a)

    y_prompt = rms_norm(xp, final_norm)
    y_sample = rms_norm(xs, final_norm)
    p_ck, p_cv, p_sk, p_sv = jnp.stack(P['ck']), jnp.stack(P['cv']), jnp.stack(P['sk']), jnp.stack(P['sv'])
    p_wk, p_wv, p_sh, p_wkv = jnp.stack(P['wk']), jnp.stack(P['wv']), jnp.stack(P['sh']), jnp.stack(P['wkv'])
    p_mk, p_mv = jnp.stack(P['mk']), jnp.stack(P['mv'])
    s_ck, s_cv, s_sk, s_sv = jnp.stack(S['ck']), jnp.stack(S['cv']), jnp.stack(S['sk']), jnp.stack(S['sv'])
    s_wk, s_wv, s_sh, s_wkv = jnp.stack(S['wk']), jnp.stack(S['wv']), jnp.stack(S['sh']), jnp.stack(S['wkv'])
    return (y_prompt, y_sample,
            p_ck, p_cv, p_sk, p_sv, p_wk, p_wv, p_sh, p_wkv, p_mk, p_mv,
            s_ck, s_cv, s_sk, s_sv, s_wk, s_wv, s_sh, s_wkv)
```

```python
import functools

import numpy as np
import jax
import jax.numpy as jnp
from jax import lax
from jax.experimental import pallas as pl
from jax.experimental.pallas import tpu as pltpu

F32 = jnp.float32
BF16 = jnp.bfloat16

D_MODEL = 1024
HEAD_DIM = 64
NSA_WIDTH = 768
NSA_HEADS = 12
NSA_GROUPS = 2
NSA_HPG = 6
CMP_LEN = 32
CMP_STRIDE = 16
CMP_HID = 64
SLC_BLOCK = 64
N_SELECT = 16
WINDOW = 512
Q_BLOCK = 64
FORCE_SCORE = 1e4
RWKV_WIDTH = 768
RWKV_HEAD_DIM = 64
RWKV_HEADS = 12
DECAY_LORA = 64
ICL_LORA = 64
GATE_LORA = 128
RWKV_PROJ = 3 * RWKV_WIDTH + DECAY_LORA + ICL_LORA + GATE_LORA
GN_EPS = 64e-5
MEM_HEADS = 4
MEM_WIDTH = 512
MEM_HEAD_DIM = 128
N_BRANCHES = 3
NSA_KV_COLS = 3 * 2 * NSA_GROUPS * HEAD_DIM
RMS_EPS = 1e-6
NEG = -1e30

COL_Q = 0
COL_KV = 768
COL_PR = 1536
COL_MG = 4096
COL_MQ = 7168
COL_GN = 7680
SLAB_COLS = 8192
PAGES_PER_STEP = 8

VMEM_LIMIT = 56 * 1024 * 1024


def _dot(a, b):
    return jnp.dot(a, b, preferred_element_type=F32)


def _dot_nt(a, b):
    return lax.dot_general(a, b, (((1,), (1,)), ((), ())), preferred_element_type=F32)


def _iota(shape, dim):
    return lax.broadcasted_iota(jnp.int32, shape, dim)


def _eye(n, dtype):
    return (_iota((n, n), 0) == _iota((n, n), 1)).astype(dtype)


def _sigmoid(x):
    return 1.0 / (1.0 + jnp.exp(-x))


def _softplus(z):
    return jnp.maximum(z, 0.0) + jnp.log(1.0 + jnp.exp(-jnp.abs(z)))


def _gelu_tanh(x):
    return 0.5 * x * (1.0 + jnp.tanh(np.sqrt(2.0 / np.pi).astype(np.float32) * (x + 0.044715 * (x * x * x))))


def _rms(x, g):
    ms = jnp.mean(x * x, axis=-1, keepdims=True)
    return (x * lax.rsqrt(ms + RMS_EPS)) * g


def _pad_rows(x, n):
    if x.shape[0] == n:
        return x
    return jnp.concatenate([x, jnp.zeros((n - x.shape[0],) + x.shape[1:], x.dtype)], axis=0)


def _norm_matmul_body(x_ref, g_ref, w_ref, o_ref, h_ref):
    @pl.when(pl.program_id(1) == 0)
    def _():
        h_ref[...] = _rms(x_ref[...], g_ref[...]).astype(BF16)

    o_ref[...] = _dot(h_ref[...], w_ref[...])


def norm_matmul(x, g, w, tm, tn):
    n, d = x.shape
    c = w.shape[1]
    return pl.pallas_call(
        _norm_matmul_body,
        grid=(n // tm, c // tn),
        in_specs=[pl.BlockSpec((tm, d), lambda i, j: (i, 0)),
                  pl.BlockSpec((1, d), lambda i, j: (0, 0)),
                  pl.BlockSpec((d, tn), lambda i, j: (0, j))],
        out_specs=pl.BlockSpec((tm, tn), lambda i, j: (i, j)),
        out_shape=jax.ShapeDtypeStruct((n, c), F32),
        scratch_shapes=[pltpu.VMEM((tm, d), BF16)],
        compiler_params=pltpu.CompilerParams(dimension_semantics=("parallel", "arbitrary"),
                                             vmem_limit_bytes=VMEM_LIMIT),
    )(x, g, w)


def _compress_consts(pe_ref, b1_ref, w0_ref, w1_ref):
    pe0 = jnp.broadcast_to(pe_ref[0], (8, pe_ref.shape[2])).astype(BF16)
    pe1 = jnp.broadcast_to(pe_ref[1], (8, pe_ref.shape[2])).astype(BF16)
    c = _dot(pe0, w0_ref[...]) + _dot(pe1, w1_ref[...])
    return c[0:1] + b1_ref[...]


def _compress_finish(u0, u1, cst, w2):
    n = u0.shape[0]
    pre = u0 + pltpu.roll(u1, n - 1, 0) + cst
    out = _dot(_gelu_tanh(pre).astype(BF16), w2)
    return jnp.where(_iota(out.shape, 0) < n - 1, out, 0.0)


def _compress_prompt_body(xk_ref, xv_ref, pek_ref, b1k_ref, w0k_ref, w1k_ref, w2k_ref,
                          pev_ref, b1v_ref, w0v_ref, w1v_ref, w2v_ref, ok_ref, ov_ref):
    for x_ref, pe_ref, b1_ref, w0_ref, w1_ref, w2_ref, o_ref in (
            (xk_ref, pek_ref, b1k_ref, w0k_ref, w1k_ref, w2k_ref, ok_ref),
            (xv_ref, pev_ref, b1v_ref, w0v_ref, w1v_ref, w2v_ref, ov_ref)):
        x = x_ref[0].astype(BF16)
        cst = _compress_consts(pe_ref, b1_ref, w0_ref, w1_ref)
        res = _compress_finish(_dot(x, w0_ref[...]), _dot(x, w1_ref[...]), cst, w2_ref[...])
        for g in range(NSA_GROUPS):
            o_ref[0, g] = res[:, g * HEAD_DIM:(g + 1) * HEAD_DIM].astype(o_ref.dtype)


def _compress_weights(pe, w1, b1, w2):
    r = CMP_LEN // CMP_STRIDE
    eye = jnp.eye(NSA_GROUPS, dtype=F32)
    w1r = w1.reshape(r, CMP_STRIDE, HEAD_DIM, CMP_HID)
    w1e = jnp.einsum('icdh,gk->icgdkh', w1r, eye).reshape(r, CMP_STRIDE * NSA_GROUPS * HEAD_DIM,
                                                         NSA_GROUPS * CMP_HID)
    pee = jnp.broadcast_to(pe.reshape(r, CMP_STRIDE, 1, HEAD_DIM), (r, CMP_STRIDE, NSA_GROUPS, HEAD_DIM))
    pee = pee.reshape(r, 1, CMP_STRIDE * NSA_GROUPS * HEAD_DIM)
    b1e = jnp.tile(b1, NSA_GROUPS).reshape(1, NSA_GROUPS * CMP_HID)
    w2e = jnp.einsum('hd,gk->ghkd', w2, eye).reshape(NSA_GROUPS * CMP_HID, NSA_GROUPS * HEAD_DIM)
    return pee, b1e, w1e[0].astype(BF16), w1e[1].astype(BF16), w2e.astype(BF16)


def compress_prompt(xk, xv, wk, wv):
    b, nch, width = xk.shape
    full = lambda a: pl.BlockSpec(a.shape, lambda i: (0,) * a.ndim)
    xspec = pl.BlockSpec((1, nch, width), lambda i: (i, 0, 0))
    ospec = pl.BlockSpec((1, NSA_GROUPS, nch, HEAD_DIM), lambda i: (i, 0, 0, 0))
    oshape = jax.ShapeDtypeStruct((b, NSA_GROUPS, nch, HEAD_DIM), BF16)
    return pl.pallas_call(
        _compress_prompt_body,
        grid=(b,),
        in_specs=[xspec, xspec] + [full(a) for a in wk] + [full(a) for a in wv],
        out_specs=[ospec, ospec],
        out_shape=[oshape, oshape],
        compiler_params=pltpu.CompilerParams(dimension_semantics=("parallel",), vmem_limit_bytes=VMEM_LIMIT),
    )(xk, xv, *wk, *wv)


def _select_blocks(score_ref, n_rows):
    score = score_ref[...]
    j_idx = _iota(score.shape, 0)

    def body(jp, cnt):
        row = score_ref[pl.ds(jp, 1), :]
        ahead = (row > score) | ((row == score) & (jp < j_idx))
        return cnt + jnp.where(ahead, 1.0, 0.0)

    cnt = lax.fori_loop(0, n_rows, body, jnp.zeros(score.shape, F32))
    return jnp.where((cnt < N_SELECT) & (score > -jnp.inf), 1.0, 0.0)


def _masked_softmax_rows(s, valid):
    s = jnp.where(valid, s, NEG)
    m = jnp.max(s, axis=-1, keepdims=True)
    p = jnp.where(valid, jnp.exp(s - m), 0.0)
    l = jnp.sum(p, axis=-1, keepdims=True)
    return p / jnp.where(l > 0.0, l, 1.0)


def _online_update(carry, s, valid, v):
    m, l, acc = carry
    s = jnp.where(valid, s, NEG)
    m_new = jnp.maximum(m, jnp.max(s, axis=-1, keepdims=True))
    alpha = jnp.exp(m - m_new)
    p = jnp.where(valid, jnp.exp(s - m_new), 0.0)
    l = alpha * l + jnp.sum(p, axis=-1, keepdims=True)
    acc = alpha * acc + _dot(p.astype(BF16), v)
    return m_new, l, acc


KV_TILE = 512
WIN_TILE = 640


def _nsa_prompt_body(q_ref, gn_ref, kc_ref, vc_ref, ks_ref, vs_ref, kw_ref, vw_ref, ovt_ref, o_ref, score_ref):
    g = pl.program_id(1)
    qb = pl.program_id(2)
    nq = Q_BLOCK
    rows = NSA_HPG * nq
    qf = q_ref[...]
    q2 = jnp.concatenate([qf[:, h * HEAD_DIM:(h + 1) * HEAD_DIM] for h in range(NSA_HPG)], axis=0).astype(BF16)
    t_q1 = qb * nq + _iota((nq, 1), 0)
    t_q = jnp.concatenate([t_q1] * NSA_HPG, axis=0)

    kc = kc_ref[0, 0]
    ncp = kc.shape[0]
    s_c = _dot_nt(q2, kc)
    c_end = _iota((rows, ncp), 1) * CMP_STRIDE + (CMP_LEN - 1)
    p_c = _masked_softmax_rows(s_c, c_end <= t_q)
    o_c = _dot(p_c.astype(BF16), vc_ref[0, 0])
    psum = p_c[0:nq]
    for h in range(1, NSA_HPG):
        psum = psum + p_c[h * nq:(h + 1) * nq]

    imp_t = lax.dot_general(ovt_ref[...], psum, (((1,), (1,)), ((), ())),
                            precision=lax.Precision.HIGHEST, preferred_element_type=F32)
    j_idx = _iota(imp_t.shape, 0)
    forced = (j_idx == 0) | (j_idx == qb) | (j_idx == qb - 1)
    score = jnp.where(forced, FORCE_SCORE, imp_t)
    score_ref[...] = jnp.where(j_idx <= qb, score, -jnp.inf)
    sel_t = _select_blocks(score_ref, qb + 1)
    sel = _dot_nt(_eye(nq, BF16), sel_t.astype(BF16)).astype(BF16)

    blocks_per_tile = KV_TILE // SLC_BLOCK
    col_blk = _iota((sel.shape[1], KV_TILE), 1) // SLC_BLOCK
    row_blk = _iota((sel.shape[1], KV_TILE), 0)
    key_off = _iota((nq, KV_TILE), 1)

    def kv_step(kt, carry):
        off = pl.multiple_of(kt * KV_TILE, KV_TILE)
        k = ks_ref[0, 0, pl.ds(off, KV_TILE), :]
        v = vs_ref[0, 0, pl.ds(off, KV_TILE), :]
        expand = jnp.where(row_blk == col_blk + kt * blocks_per_tile, 1.0, 0.0).astype(BF16)
        m1 = (_dot(sel, expand) > 0.5) & (key_off + off <= t_q1)
        valid = jnp.concatenate([m1] * NSA_HPG, axis=0)
        return _online_update(carry, _dot_nt(q2, k), valid, v)

    init = (jnp.full((rows, 1), NEG, F32), jnp.zeros((rows, 1), F32), jnp.zeros((rows, HEAD_DIM), F32))
    _, l_s, acc_s = lax.fori_loop(0, qb // blocks_per_tile + 1, kv_step, init)
    o_s = acc_s / l_s

    w0 = pl.multiple_of(jnp.maximum(qb * nq + nq - WIN_TILE, 0), SLC_BLOCK)
    kw = kw_ref[0, 0, pl.ds(w0, WIN_TILE), :]
    vw = vw_ref[0, 0, pl.ds(w0, WIN_TILE), :]
    diff = t_q - (w0 + _iota((rows, WIN_TILE), 1))
    p_w = _masked_softmax_rows(_dot_nt(q2, kw), (diff >= 0) & (diff <= WINDOW))
    o_w = _dot(p_w.astype(BF16), vw)

    gates = _sigmoid(gn_ref[...])
    per_group = NSA_HPG * 3
    gates = jnp.where(g == 0, gates[:, 0:per_group], gates[:, per_group:2 * per_group])
    outs = []
    for h in range(NSA_HPG):
        sl = slice(h * nq, (h + 1) * nq)
        outs.append(gates[:, 3 * h:3 * h + 1] * o_c[sl] + gates[:, 3 * h + 1:3 * h + 2] * o_s[sl]
                    + gates[:, 3 * h + 2:3 * h + 3] * o_w[sl])
    o_ref[...] = jnp.concatenate(outs, axis=1).astype(o_ref.dtype)


def _overlap_t(n_blocks, n_cmp_padded, n_cmp):
    i = np.arange(n_cmp_padded)[None, :] * CMP_STRIDE
    j = np.arange(n_blocks)[:, None] * SLC_BLOCK
    ov = (i < j + SLC_BLOCK) & (i + CMP_LEN > j) & (np.arange(n_cmp_padded)[None, :] < n_cmp)
    return jnp.asarray(ov.astype(np.float32))


def nsa_prompt(slab, kc, vc, ks, vs, kw, vw, b, t):
    nb = t // Q_BLOCK
    gw = NSA_HPG * HEAD_DIM
    ncp = kc.shape[2]
    ovt = _overlap_t(nb, ncp, ncp - 1)
    seq = pl.BlockSpec((1, 1, t, HEAD_DIM), lambda bi, g, qb: (bi, g, 0, 0))
    cmp_spec = pl.BlockSpec((1, 1, ncp, HEAD_DIM), lambda bi, g, qb: (bi, g, 0, 0))
    return pl.pallas_call(
        _nsa_prompt_body,
        grid=(b, NSA_GROUPS, nb),
        in_specs=[pl.BlockSpec((Q_BLOCK, gw), lambda bi, g, qb: (bi * nb + qb, g)),
                  pl.BlockSpec((Q_BLOCK, 128), lambda bi, g, qb: (bi * nb + qb, COL_GN // 128)),
                  cmp_spec, cmp_spec, seq, seq, seq, seq,
                  pl.BlockSpec(ovt.shape, lambda bi, g, qb: (0, 0))],
        out_specs=pl.BlockSpec((Q_BLOCK, gw), lambda bi, g, qb: (bi * nb + qb, g)),
        out_shape=jax.ShapeDtypeStruct((b * t, NSA_WIDTH), BF16),
        scratch_shapes=[pltpu.VMEM((nb, Q_BLOCK), F32)],
        compiler_params=pltpu.CompilerParams(dimension_semantics=("parallel", "parallel", "arbitrary"),
                                             vmem_limit_bytes=VMEM_LIMIT),
    )(slab, slab, kc, vc, ks, vs, kw, vw, ovt)


def _compress_sample_body(pt_ref, *refs):
    n = PAGES_PER_STEP
    k_pages, v_pages = refs[0:n], refs[n:2 * n]
    (pek_ref, b1k_ref, w0k_ref, w1k_ref, w2k_ref, wck_ref,
     pev_ref, b1v_ref, w0v_ref, w1v_ref, w2v_ref, wcv_ref, ok_ref, ov_ref, uk_ref, uv_ref) = refs[2 * n:]
    j = pl.program_id(1)
    rows = n * k_pages[0].shape[1]
    off = pl.multiple_of(j * rows, rows)
    for pages, wc_ref, u_ref in ((k_pages, wck_ref, uk_ref), (v_pages, wcv_ref, uv_ref)):
        x = jnp.concatenate([r[0] for r in pages], axis=0).astype(BF16)
        u_ref[pl.ds(off, rows), :] = _dot(x, wc_ref[...])

    @pl.when(j == pl.num_programs(1) - 1)
    def _():
        half = NSA_GROUPS * CMP_HID
        for pe_ref, b1_ref, w0_ref, w1_ref, w2_ref, u_ref, o_ref in (
                (pek_ref, b1k_ref, w0k_ref, w1k_ref, w2k_ref, uk_ref, ok_ref),
                (pev_ref, b1v_ref, w0v_ref, w1v_ref, w2v_ref, uv_ref, ov_ref)):
            cst = _compress_consts(pe_ref, b1_ref, w0_ref, w1_ref)
            u = u_ref[...]
            res = _compress_finish(u[:, 0:half], u[:, half:2 * half], cst, w2_ref[...])
            for g in range(NSA_GROUPS):
                o_ref[0, g] = res[:, g * HEAD_DIM:(g + 1) * HEAD_DIM].astype(o_ref.dtype)


def compress_sample(pool_k, pool_v, page_table, wk, wv):
    bs, n_pages = page_table.shape
    _, cpp, width = pool_k.shape
    n = PAGES_PER_STEP
    nch = n_pages * cpp
    wck = jnp.concatenate([wk[2], wk[3]], axis=1)
    wcv = jnp.concatenate([wv[2], wv[3]], axis=1)
    page = lambda k: pl.BlockSpec((1, cpp, width), lambda b, j, pt: (pt[b, n * j + k], 0, 0))
    full = lambda a: pl.BlockSpec(a.shape, lambda b, j, pt: (0,) * a.ndim)
    consts = list(wk) + [wck] + list(wv) + [wcv]
    ospec = pl.BlockSpec((1, NSA_GROUPS, nch, HEAD_DIM), lambda b, j, pt: (b, 0, 0, 0))
    oshape = jax.ShapeDtypeStruct((bs, NSA_GROUPS, nch, HEAD_DIM), BF16)
    return pl.pallas_call(
        _compress_sample_body,
        grid_spec=pltpu.PrefetchScalarGridSpec(
            num_scalar_prefetch=1,
            grid=(bs, n_pages // n),
            in_specs=[page(k) for k in range(n)] * 2 + [full(a) for a in consts],
            out_specs=[ospec, ospec],
            scratch_shapes=[pltpu.VMEM((nch, 2 * NSA_GROUPS * CMP_HID), F32)] * 2),
        out_shape=[oshape, oshape],
        compiler_params=pltpu.CompilerParams(dimension_semantics=("parallel", "arbitrary"),
                                             vmem_limit_bytes=VMEM_LIMIT),
    )(page_table, *([pool_k] * n), *([pool_v] * n), *consts)


TOK_PAD = 8


def _nsa_sample_body(past_len, tn, pt_ref, *refs):
    n = PAGES_PER_STEP
    q_ref, gn_ref, skn_ref, svn_ref, wkn_ref, wvn_ref, kc_ref, vc_ref, wkc_ref, wvc_ref = refs[0:10]
    k_pages, v_pages = refs[10:10 + n], refs[10 + n:10 + 2 * n]
    ovt_ref, o_ref, score_ref, sel_ref, m_ref, l_ref, acc_ref, oc_ref, ow_ref = refs[10 + 2 * n:]
    j = pl.program_id(1)
    tp = TOK_PAD
    rows = NSA_HPG * tp
    gw = NSA_HPG * HEAD_DIM
    lanes = NSA_GROUPS * HEAD_DIM
    nsp = score_ref.shape[0]

    q8 = _pad_rows(q_ref[0], tp)
    zero = jnp.zeros((rows, HEAD_DIM), F32)
    q2, q2w = [], []
    for g in range(NSA_GROUPS):
        qg = jnp.concatenate([q8[:, g * gw + h * HEAD_DIM:g * gw + (h + 1) * HEAD_DIM] for h in range(NSA_HPG)], axis=0)
        q2.append(qg.astype(BF16))
        q2w.append(jnp.concatenate([qg, zero] if g == 0 else [zero, qg], axis=1).astype(BF16))
    tok = _iota((rows, 1), 0) % tp
    t_q = past_len + tok

    def new_keys_valid(width):
        tk = _iota((rows, width), 1)
        return (tk <= tok) & (tk < tn)

    @pl.when(j == 0)
    def _():
        psums = []
        for g in range(NSA_GROUPS):
            kc = kc_ref[0, g]
            s_c = _dot_nt(q2[g], kc)
            c_end = _iota(s_c.shape, 1) * CMP_STRIDE + (CMP_LEN - 1)
            p_c = _masked_softmax_rows(s_c, c_end <= t_q)
            oc_ref[g] = _dot(p_c.astype(BF16), vc_ref[0, g])
            ps = p_c[0:tp]
            for h in range(1, NSA_HPG):
                ps = ps + p_c[h * tp:(h + 1) * tp]
            psums.append(ps)
        psum = jnp.concatenate(psums, axis=0)
        imp_t = lax.dot_general(ovt_ref[...], psum, (((1,), (1,)), ((), ())),
                                precision=lax.Precision.HIGHEST, preferred_element_type=F32)
        j_idx = _iota(imp_t.shape, 0)
        cur = (past_len + _iota(imp_t.shape, 1) % tp) // SLC_BLOCK
        forced = (j_idx == 0) | (j_idx == cur) | (j_idx == cur - 1)
        score = jnp.where(forced, FORCE_SCORE, imp_t)
        score_ref[...] = jnp.where(j_idx <= cur, score, -jnp.inf)
        sel_t = _select_blocks(score_ref, (past_len + tn - 1) // SLC_BLOCK + 1)
        sel_ref[...] = _dot_nt(_eye(NSA_GROUPS * tp, BF16), sel_t.astype(BF16))

        lw = wkc_ref.shape[1]
        kw = wkc_ref[0].astype(BF16)
        vw = wvc_ref[0].astype(BF16)
        kwn = _pad_rows(wkn_ref[0], 16).astype(BF16)
        vwn = _pad_rows(wvn_ref[0], 16).astype(BF16)
        diff = t_q - (past_len - lw + _iota((rows, lw), 1))
        valid_c = (diff >= 0) & (diff <= WINDOW)
        valid_n = new_keys_valid(16)
        for g in range(NSA_GROUPS):
            carry = (jnp.full((rows, 1), NEG, F32), jnp.zeros((rows, 1), F32), jnp.zeros((rows, lanes), F32))
            carry = _online_update(carry, _dot_nt(q2w[g], kw), valid_c, vw)
            _, l_w, acc_w = _online_update(carry, _dot_nt(q2w[g], kwn), valid_n, vwn)
            ow_ref[g] = acc_w / l_w
            m_ref[g] = jnp.full((rows, 1), NEG, F32)
            l_ref[g] = jnp.zeros((rows, 1), F32)
            acc_ref[g] = jnp.zeros((rows, lanes), F32)

    page_rows = k_pages[0].shape[1]
    nk = n * page_rows
    k = jnp.concatenate([r[0] for r in k_pages], axis=0).astype(BF16)
    v = jnp.concatenate([r[0] for r in v_pages], axis=0).astype(BF16)
    expand = jnp.where(_iota((nsp, nk), 0) == _iota((nsp, nk), 1) // SLC_BLOCK + j * (nk // SLC_BLOCK),
                       1.0, 0.0).astype(BF16)
    mask_all = _dot(sel_ref[...].astype(BF16), expand) > 0.5
    for g in range(NSA_GROUPS):
        valid = jnp.concatenate([mask_all[g * tp:(g + 1) * tp]] * NSA_HPG, axis=0)
        m, l, acc = _online_update((m_ref[g], l_ref[g], acc_ref[g]), _dot_nt(q2w[g], k), valid, v)
        m_ref[g], l_ref[g], acc_ref[g] = m, l, acc

    @pl.when(j == pl.num_programs(1) - 1)
    def _():
        kn = _pad_rows(skn_ref[0], 16).astype(BF16)
        vn = _pad_rows(svn_ref[0], 16).astype(BF16)
        expand_n = jnp.where(_iota((nsp, 16), 0) == (past_len + _iota((nsp, 16), 1)) // SLC_BLOCK,
                             1.0, 0.0).astype(BF16)
        sel_n = _dot(sel_ref[...].astype(BF16), expand_n) > 0.5
        gates = _sigmoid(_pad_rows(gn_ref[0], tp))
        outs = []
        for g in range(NSA_GROUPS):
            valid = jnp.concatenate([sel_n[g * tp:(g + 1) * tp]] * NSA_HPG, axis=0) & new_keys_valid(16)
            _, l_s, acc_s = _online_update((m_ref[g], l_ref[g], acc_ref[g]), _dot_nt(q2w[g], kn), valid, vn)
            o_s = (acc_s / l_s)[:, g * HEAD_DIM:(g + 1) * HEAD_DIM]
            o_w = ow_ref[g][:, g * HEAD_DIM:(g + 1) * HEAD_DIM]
            o_c = oc_ref[g]
            for h in range(NSA_HPG):
                sl = slice(h * tp, (h + 1) * tp)
                c0 = (g * NSA_HPG + h) * 3
                outs.append(gates[:, c0:c0 + 1] * o_c[sl] + gates[:, c0 + 1:c0 + 2] * o_s[sl]
                            + gates[:, c0 + 2:c0 + 3] * o_w[sl])
        o_ref[0] = jnp.concatenate(outs, axis=1)[0:tn].astype(o_ref.dtype)


def nsa_sample(slab3, kc, vc, win_k, win_v, pool_k, pool_v, page_table, past_len):
    bs, tn, _ = slab3.shape
    n_pages = page_table.shape[1]
    n = PAGES_PER_STEP
    page_rows = pool_k.shape[1]
    lanes = NSA_GROUPS * HEAD_DIM
    ncp = kc.shape[2]
    ns = -(-(past_len + tn) // SLC_BLOCK)
    nsp = -(-ns // 8) * 8
    ovt = _overlap_t(nsp, ncp, (past_len + tn) // CMP_STRIDE - CMP_LEN // CMP_STRIDE + 1)
    rows = NSA_HPG * TOK_PAD
    tokblk = lambda width, col: pl.BlockSpec((1, tn, width), lambda b, j, pt: (b, 0, col // width))
    cmp_spec = pl.BlockSpec((1, NSA_GROUPS, ncp, HEAD_DIM), lambda b, j, pt: (b, 0, 0, 0))
    win_spec = pl.BlockSpec((1, win_k.shape[1], lanes), lambda b, j, pt: (b, 0, 0))
    page = lambda k: pl.BlockSpec((1, page_rows, lanes), lambda b, j, pt: (pt[b, n * j + k], 0, 0))
    return pl.pallas_call(
        functools.partial(_nsa_sample_body, past_len, tn),
        grid_spec=pltpu.PrefetchScalarGridSpec(
            num_scalar_prefetch=1,
            grid=(bs, n_pages // n),
            in_specs=[tokblk(NSA_WIDTH, COL_Q), tokblk(128, COL_GN),
                      tokblk(lanes, COL_KV + 2 * lanes), tokblk(lanes, COL_KV + 3 * lanes),
                      tokblk(lanes, COL_KV + 4 * lanes), tokblk(lanes, COL_KV + 5 * lanes),
                      cmp_spec, cmp_spec, win_spec, win_spec]
            + [page(k) for k in range(n)] * 2
            + [pl.BlockSpec(ovt.shape, lambda b, j, pt: (0, 0))],
            out_specs=pl.BlockSpec((1, tn, NSA_WIDTH), lambda b, j, pt: (b, 0, 0)),
            scratch_shapes=[pltpu.VMEM((nsp, NSA_GROUPS * TOK_PAD), F32),
                            pltpu.VMEM((NSA_GROUPS * TOK_PAD, nsp), F32),
                            pltpu.VMEM((NSA_GROUPS, rows, 1), F32),
                            pltpu.VMEM((NSA_GROUPS, rows, 1), F32),
                            pltpu.VMEM((NSA_GROUPS, rows, lanes), F32),
                            pltpu.VMEM((NSA_GROUPS, rows, HEAD_DIM), F32),
                            pltpu.VMEM((NSA_GROUPS, rows, lanes), F32)]),
        out_shape=jax.ShapeDtypeStruct((bs, tn, NSA_WIDTH), F32),
        compiler_params=pltpu.CompilerParams(dimension_semantics=("parallel", "arbitrary"),
                                             vmem_limit_bytes=VMEM_LIMIT),
    )(page_table, slab3, slab3, slab3, slab3, slab3, slab3, kc, vc, win_k, win_v,
      *([pool_k] * n), *([pool_v] * n), ovt)


def _cumsum_rows(x):
    n = x.shape[0]
    row = _iota((n, 1), 0)
    k = 1
    while k < n:
        x = x + jnp.where(row >= k, pltpu.roll(x, k, 0), 0.0)
        k *= 2
    return x


def _rwkv_body(n_valid, chunk, p0_ref, p1_ref, p2_ref, p3_ref, p4_ref, prev_ref, s0_ref,
               mu_ref, w0_ref, w2_ref, a0_ref, a2_ref, g2_ref, kk_ref, ka_ref, rk_ref, lng_ref, lnb_ref,
               o_ref, sout_ref, carry_ref, s_ref):
    c = pl.program_id(1)
    hd = RWKV_HEAD_DIM

    @pl.when(c == 0)
    def _():
        carry_ref[...] = jnp.broadcast_to(prev_ref[0], carry_ref.shape)
        s_ref[...] = s0_ref[0]

    p = jnp.concatenate([r[0] for r in (p0_ref, p1_ref, p2_ref, p3_ref, p4_ref)], axis=1)
    p = _pad_rows(p, chunk)
    row = _iota((chunk, 1), 0)
    valid = row < n_valid
    prev = jnp.where(row == 0, carry_ref[0:1, :], pltpu.roll(p, 1, 0))
    xm = p + (prev - p) * mu_ref[...]
    carry_ref[...] = jnp.broadcast_to(p[n_valid - 1:n_valid, :], carry_ref.shape)

    wdt = RWKV_WIDTH
    r_all, k_all, v_all = xm[:, 0:wdt], xm[:, wdt:2 * wdt], xm[:, 2 * wdt:3 * wdt]
    o = 3 * wdt
    wd, ad, gd = xm[:, o:o + DECAY_LORA], xm[:, o + DECAY_LORA:o + DECAY_LORA + ICL_LORA], \
        xm[:, o + DECAY_LORA + ICL_LORA:o + DECAY_LORA + ICL_LORA + GATE_LORA]
    w = w0_ref[...] + _dot(jnp.tanh(wd).astype(BF16), w2_ref[...])
    logw = -jnp.exp(-_softplus(-w) - 0.5)
    a_all = _sigmoid(a0_ref[...] + _dot(ad.astype(BF16), a2_ref[...]))
    g_all = _dot(_sigmoid(gd).astype(BF16), g2_ref[...])
    logw = jnp.where(valid, logw, 0.0)
    cum = _cumsum_rows(logw)
    total = cum[chunk - 1:chunk, :]
    w_in = jnp.exp(cum)
    w_ex = jnp.exp(cum - logw)
    w_inv = jnp.exp(-cum)
    w_rem = jnp.exp(total - cum)
    w_tot = jnp.exp(total)
    kk_all = k_all * kk_ref[...]
    k2_all = k_all * (1.0 + (a_all - 1.0) * ka_ref[...])

    t_i = _iota((chunk, chunk), 0)
    s_i = _iota((chunk, chunk), 1)
    strict = s_i < t_i
    incl = s_i <= t_i
    eye = _eye(hd, BF16)
    n_rounds = int(np.log2(chunk))
    outs = []
    for h in range(RWKV_HEADS):
        sl = slice(h * hd, (h + 1) * hd)
        kkh = kk_all[:, sl]
        nrm = jnp.sqrt(jnp.sum(kkh * kkh, axis=-1, keepdims=True))
        kkh = jnp.where(valid, kkh / jnp.maximum(nrm, 1e-12), 0.0)
        k2u = k2_all[:, sl]
        k2h = jnp.where(valid, k2u, 0.0)
        vh = jnp.where(valid, v_all[:, sl], 0.0)
        rh = r_all[:, sl]
        bh = kkh * a_all[:, sl]
        at = (-kkh * w_ex[:, sl]).astype(BF16)
        rt = (rh * w_in[:, sl]).astype(BF16)
        bt = (bh * w_inv[:, sl]).astype(BF16)
        kt = (k2h * w_inv[:, sl]).astype(BF16)
        bp = (bh * w_rem[:, sl]).astype(BF16)
        kp = (k2h * w_rem[:, sl]).astype(BF16)
        vb = vh.astype(BF16)
        a_ab = jnp.where(strict, _dot_nt(at, bt), 0.0)
        a_ak = jnp.where(strict, _dot_nt(at, kt), 0.0)
        a_rb = jnp.where(incl, _dot_nt(rt, bt), 0.0)
        a_rk = jnp.where(incl, _dot_nt(rt, kt), 0.0)
        s0 = s_ref[h]
        s0t = _dot_nt(eye, s0.astype(BF16)).astype(BF16)
        u = _dot(at, s0t) + _dot(a_ak.astype(BF16), vb)
        pw = a_ab
        for it in range(n_rounds):
            pwb = pw.astype(BF16)
            u = u + _dot(pwb, u.astype(BF16))
            if it + 1 < n_rounds:
                pw = _dot(pwb, pwb)
        ub = u.astype(BF16)
        y = _dot(rt, s0t) + _dot(a_rb.astype(BF16), ub) + _dot(a_rk.astype(BF16), vb)
        ut = _dot_nt(eye, ub).astype(BF16)
        vt = _dot_nt(eye, vb).astype(BF16)
        s_ref[h] = s0 * w_tot[:, sl] + _dot(ut, bp) + _dot(vt, kp)
        mean = jnp.mean(y, axis=-1, keepdims=True)
        yc = y - mean
        var = jnp.mean(yc * yc, axis=-1, keepdims=True)
        yn = yc * lax.rsqrt(var + GN_EPS)
        bonus = jnp.sum(rh * k2u * rk_ref[:, sl], axis=-1, keepdims=True) * v_all[:, sl]
        outs.append((yn * lng_ref[:, sl] + lnb_ref[:, sl] + bonus) * g_all[:, sl])
    out = jnp.concatenate(outs, axis=1)
    o_ref[0] = out[0:o_ref.shape[1]].astype(o_ref.dtype)

    @pl.when(c == pl.num_programs(1) - 1)
    def _():
        sout_ref[0] = s_ref[...]


def rwkv(slab3, p_prev, s0, params, chunk, out_dtype):
    b, t, _ = slab3.shape
    tc = min(t, chunk)
    nchunks = t // tc
    blk = 512
    pspec = lambda k: pl.BlockSpec((1, tc, blk), lambda bi, c: (bi, c, COL_PR // blk + k))
    full = lambda a: pl.BlockSpec(a.shape, lambda bi, c: (0,) * a.ndim)
    sspec = pl.BlockSpec((1, RWKV_HEADS, RWKV_HEAD_DIM, RWKV_HEAD_DIM), lambda bi, c: (bi, 0, 0, 0))
    return pl.pallas_call(
        functools.partial(_rwkv_body, tc, chunk),
        grid=(b, nchunks),
        in_specs=[pspec(k) for k in range(5)]
        + [pl.BlockSpec((1, 1, RWKV_PROJ), lambda bi, c: (bi, 0, 0)), sspec]
        + [full(a) for a in params],
        out_specs=[pl.BlockSpec((1, tc, RWKV_WIDTH), lambda bi, c: (bi, c, 0)), sspec],
        out_shape=[jax.ShapeDtypeStruct((b, t, RWKV_WIDTH), out_dtype),
                   jax.ShapeDtypeStruct(s0.shape, F32)],
        scratch_shapes=[pltpu.VMEM((8, RWKV_PROJ), F32),
                        pltpu.VMEM((RWKV_HEADS, RWKV_HEAD_DIM, RWKV_HEAD_DIM), F32)],
        compiler_params=pltpu.CompilerParams(dimension_semantics=("parallel", "arbitrary"),
                                             vmem_limit_bytes=VMEM_LIMIT),
    )(slab3, slab3, slab3, slab3, slab3, p_prev, s0, *params)


def _mem_attend_body(q_ref, k_ref, v_ref, o_ref):
    tm = q_ref.shape[1]
    q = _pad_rows(q_ref[0], max(tm, 16)).astype(BF16)
    k = k_ref[0].astype(BF16)
    v = v_ref[0].astype(BF16)
    outs = []
    for h in range(MEM_HEADS):
        sl = slice(h * MEM_HEAD_DIM, (h + 1) * MEM_HEAD_DIM)
        s = _dot_nt(q[:, sl], k[:, sl]) * (MEM_HEAD_DIM ** -0.5)
        m = jnp.max(s, axis=-1, keepdims=True)
        p = jnp.exp(s - m)
        p = p / jnp.sum(p, axis=-1, keepdims=True)
        outs.append(_dot(p.astype(BF16), v[:, sl]))
    o_ref[0] = jnp.concatenate(outs, axis=1)[0:tm].astype(o_ref.dtype)


def mem_attend(slab3, mk, k_blk, mv, v_blk, tm, out_dtype):
    b, t, _ = slab3.shape
    m = mk.shape[1]
    return pl.pallas_call(
        _mem_attend_body,
        grid=(b, t // tm),
        in_specs=[pl.BlockSpec((1, tm, MEM_WIDTH), lambda bi, i: (bi, i, COL_MQ // MEM_WIDTH)),
                  pl.BlockSpec((1, m, MEM_WIDTH), lambda bi, i: (bi, 0, k_blk)),
                  pl.BlockSpec((1, m, MEM_WIDTH), lambda bi, i: (bi, 0, v_blk))],
        out_specs=pl.BlockSpec((1, tm, MEM_WIDTH), lambda bi, i: (bi, i, 0)),
        out_shape=jax.ShapeDtypeStruct((b, t, MEM_WIDTH), out_dtype),
        compiler_params=pltpu.CompilerParams(dimension_semantics=("parallel", "parallel"),
                                             vmem_limit_bytes=VMEM_LIMIT),
    )(slab3, mk, mv)


def _merge_body(x_ref, on_ref, or_ref, om_ref, g0_ref, g1_ref, g2_ref, wn_ref, wr_ref, wm_ref, wo_ref, o_ref):
    m = _sigmoid(g0_ref[...]) * _dot(on_ref[...].astype(BF16), wn_ref[...])
    m = m + _sigmoid(g1_ref[...]) * _dot(or_ref[...].astype(BF16), wr_ref[...])
    m = m + _sigmoid(g2_ref[...]) * _dot(om_ref[...].astype(BF16), wm_ref[...])
    o_ref[...] = x_ref[...] + _dot(m.astype(BF16), wo_ref[...])


def merge(x, o_nsa, o_rwkv, o_mem, slab, wn, wr, wm, wo, tm):
    n, d = x.shape
    row = lambda w: pl.BlockSpec((tm, w), lambda i: (i, 0))
    full = lambda a: pl.BlockSpec(a.shape, lambda i: (0, 0))
    gate = lambda k: pl.BlockSpec((tm, d), lambda i: (i, COL_MG // d + k))
    return pl.pallas_call(
        _merge_body,
        grid=(n // tm,),
        in_specs=[row(d), row(NSA_WIDTH), row(RWKV_WIDTH), row(MEM_WIDTH), gate(0), gate(1), gate(2),
                  full(wn), full(wr), full(wm), full(wo)],
        out_specs=row(d),
        out_shape=jax.ShapeDtypeStruct((n, d), F32),
        compiler_params=pltpu.CompilerParams(dimension_semantics=("parallel",), vmem_limit_bytes=VMEM_LIMIT),
    )(x, o_nsa, o_rwkv, o_mem, slab, slab, slab, wn, wr, wm, wo)


def _ffn_body(x_ref, gf_ref, wg_ref, wu_ref, wd_ref, gl_ref, o_ref):
    x = x_ref[...]
    hf = _rms(x, gf_ref[...]).astype(BF16)
    gate = _dot(hf, wg_ref[...])
    up = _dot(hf, wu_ref[...])
    act = (gate * _sigmoid(gate) * up).astype(BF16)
    x2 = x + _dot(act, wd_ref[...])
    o_ref[...] = _rms(x2, gl_ref[...])


def ffn(x, gf, wg, wu, wd, gl, tm):
    n, d = x.shape
    row = pl.BlockSpec((tm, d), lambda i: (i, 0))
    full = lambda a: pl.BlockSpec(a.shape, lambda i: (0, 0), pipeline_mode=pl.Buffered(1))
    return pl.pallas_call(
        _ffn_body,
        grid=(n // tm,),
        in_specs=[row, full(gf), full(wg), full(wu), full(wd), full(gl)],
        out_specs=row,
        out_shape=jax.ShapeDtypeStruct((n, d), F32),
        compiler_params=pltpu.CompilerParams(dimension_semantics=("parallel",), vmem_limit_bytes=VMEM_LIMIT),
    )(x, gf, wg, wu, wd, gl)


def _slab_weight(w_in):
    wq, wkv, wgn, wpr, wmq, wmg = jnp.split(w_in, np.cumsum(
        [NSA_WIDTH, NSA_KV_COLS, 3 * NSA_HEADS, RWKV_PROJ, MEM_WIDTH])[:5].tolist(), axis=1)
    d = w_in.shape[0]
    pad = jnp.zeros((d, SLAB_COLS - COL_GN - 3 * NSA_HEADS), w_in.dtype)
    w = jnp.concatenate([wq * (HEAD_DIM ** -0.5), wkv, wpr, wmg, wmq, wgn, pad], axis=1)
    return w.astype(BF16)


def _group_rows(slab3, col, dtype):
    b, t, _ = slab3.shape
    x = slab3[:, :, col:col + NSA_GROUPS * HEAD_DIM].reshape(b, t, NSA_GROUPS, HEAD_DIM)
    return jnp.transpose(x, (0, 2, 1, 3)).astype(dtype)


def kernel(x_prompt, x_sample, cache_cmp_k, cache_cmp_v, cache_slc_k, cache_slc_v, cache_win_k, cache_win_v, state_rwkv_shift, state_rwkv_wkv, cache_mem_k, cache_mem_v, page_table, mem_prompt, attn_norm, w_in, cmp_pe_k, cmp_w1_k, cmp_b1_k, cmp_w2_k, cmp_pe_v, cmp_w1_v, cmp_b1_v, cmp_w2_v, rwkv_mu, rwkv_w0, rwkv_w2, rwkv_a0, rwkv_a2, rwkv_g2, rwkv_kk, rwkv_ka, rwkv_rk, rwkv_ln_g, rwkv_ln_b, mem_norm, w_mem_kv, w_o_nsa, w_o_rwkv, w_o_mem, w_out, ffn_norm, w_gate, w_up, w_down, final_norm):
    assert w_in.shape[0] == 1, "one layer"
    bp, t, d = x_prompt.shape
    bs, tn, _ = x_sample.shape
    row2 = lambda a: a.reshape(1, -1)

    w_slab = _slab_weight(w_in[0])
    cmp_wk = _compress_weights(cmp_pe_k[0], cmp_w1_k[0], cmp_b1_k[0], cmp_w2_k[0])
    cmp_wv = _compress_weights(cmp_pe_v[0], cmp_w1_v[0], cmp_b1_v[0], cmp_w2_v[0])
    rw_params = (row2(rwkv_mu[0]), row2(rwkv_w0[0]), rwkv_w2[0].astype(BF16), row2(rwkv_a0[0]),
                 rwkv_a2[0].astype(BF16), rwkv_g2[0].astype(BF16), row2(rwkv_kk[0]), row2(rwkv_ka[0]),
                 row2(rwkv_rk[0]), row2(rwkv_ln_g[0]), row2(rwkv_ln_b[0]))
    wn, wr, wm, wo = (a[0].astype(BF16) for a in (w_o_nsa, w_o_rwkv, w_o_mem, w_out))
    wg, wu, wd = (a[0].astype(BF16) for a in (w_gate, w_up, w_down))
    gf, gl = row2(ffn_norm[0]), row2(final_norm)

    xp2 = x_prompt.reshape(bp * t, d)
    slab = norm_matmul(xp2, row2(attn_norm[0]), w_slab, 1024, 512)
    slab3 = slab.reshape(bp, t, SLAB_COLS)
    gsz = NSA_GROUPS * HEAD_DIM
    kv_rows = [slab3[:, :, COL_KV + i * gsz:COL_KV + (i + 1) * gsz] for i in range(6)]
    nch = t // CMP_STRIDE
    kc, vc = compress_prompt(kv_rows[0].reshape(bp, nch, CMP_STRIDE * gsz),
                             kv_rows[1].reshape(bp, nch, CMP_STRIDE * gsz), cmp_wk, cmp_wv)
    ks, vs, kw, vw = (_group_rows(slab3, COL_KV + i * gsz, BF16) for i in range(2, 6))
    o_nsa = nsa_prompt(slab, kc, vc, ks, vs, kw, vw, bp, t)
    o_rwkv, s_p = rwkv(slab3, jnp.zeros((bp, 1, RWKV_PROJ), F32),
                       jnp.zeros((bp, RWKV_HEADS, RWKV_HEAD_DIM, RWKV_HEAD_DIM), F32), rw_params, 64, BF16)
    mem_n = mem_prompt.shape[1]
    mkv = norm_matmul(mem_prompt.reshape(bp * mem_n, d), row2(mem_norm[0]), w_mem_kv[0].astype(BF16),
                      min(1024, bp * mem_n), 512).reshape(bp, mem_n, 2 * MEM_WIDTH)
    o_mem = mem_attend(slab3, mkv, 0, mkv, 1, 512, BF16)
    x1 = merge(xp2, o_nsa, o_rwkv.reshape(bp * t, RWKV_WIDTH), o_mem.reshape(bp * t, MEM_WIDTH), slab,
               wn, wr, wm, wo, 512)
    y_prompt = ffn(x1, gf, wg, wu, wd, gl, 256).reshape(bp, t, d)

    heads = lambda a: a.reshape(1, a.shape[0], a.shape[1], NSA_GROUPS, HEAD_DIM)
    wp0 = max(t - WINDOW, 0)
    p_state = (heads(kv_rows[0]), heads(kv_rows[1]), heads(kv_rows[2]), heads(kv_rows[3]),
               heads(kv_rows[4][:, wp0:]), heads(kv_rows[5][:, wp0:]),
               slab3[:, t - 1, COL_PR:COL_PR + RWKV_PROJ][None],
               s_p[None],
               mkv[:, :, :MEM_WIDTH].reshape(1, bp, mem_n, MEM_HEADS, MEM_HEAD_DIM),
               mkv[:, :, MEM_WIDTH:].reshape(1, bp, mem_n, MEM_HEADS, MEM_HEAD_DIM))

    past_len = page_table.shape[1] * cache_cmp_k.shape[2]
    assert page_table.shape[1] % PAGES_PER_STEP == 0 and past_len % SLC_BLOCK == 0
    assert (past_len + tn) // CMP_STRIDE == past_len // CMP_STRIDE and tn <= TOK_PAD
    xs2 = x_sample.reshape(bs * tn, d)
    slab_s = norm_matmul(xs2, row2(attn_norm[0]), w_slab, bs * tn, 512)
    slab_s3 = slab_s.reshape(bs, tn, SLAB_COLS)
    n_phys, page_rows = cache_cmp_k.shape[1], cache_cmp_k.shape[2]
    cpp = page_rows // CMP_STRIDE
    chunked = lambda pool: pool[0].reshape(n_phys, cpp, CMP_STRIDE * gsz)
    paged = lambda pool: pool[0].reshape(n_phys, page_rows, gsz)
    kc_s, vc_s = compress_sample(chunked(cache_cmp_k), chunked(cache_cmp_v), page_table, cmp_wk, cmp_wv)
    lw = cache_win_k.shape[2]
    win_k, win_v = cache_win_k[0].reshape(bs, lw, gsz), cache_win_v[0].reshape(bs, lw, gsz)
    o_nsa_s = nsa_sample(slab_s3, kc_s, vc_s, win_k, win_v, paged(cache_slc_k), paged(cache_slc_v),
                         page_table, past_len)
    o_rwkv_s, s_s = rwkv(slab_s3, state_rwkv_shift[0][:, None, :], state_rwkv_wkv[0], rw_params, 16, F32)
    mem_s = cache_mem_k.shape[2]
    o_mem_s = mem_attend(slab_s3, cache_mem_k[0].reshape(bs, mem_s, MEM_WIDTH), 0,
                         cache_mem_v[0].reshape(bs, mem_s, MEM_WIDTH), 0, tn, F32)
    x1s = merge(xs2, o_nsa_s.reshape(bs * tn, NSA_WIDTH), o_rwkv_s.reshape(bs * tn, RWKV_WIDTH),
                o_mem_s.reshape(bs * tn, MEM_WIDTH), slab_s, wn, wr, wm, wo, min(512, bs * tn))
    y_sample = ffn(x1s, gf, wg, wu, wd, gl, min(256, bs * tn)).reshape(bs, tn, d)
    kv_new = [slab_s3[:, :, COL_KV + i * gsz:COL_KV + (i + 1) * gsz] for i in range(6)]
    s_state = (heads(kv_new[0]), heads(kv_new[1]), heads(kv_new[2]), heads(kv_new[3]),
               heads(jnp.concatenate([win_k, kv_new[4]], axis=1)[:, tn:]),
               heads(jnp.concatenate([win_v, kv_new[5]], axis=1)[:, tn:]),
               slab_s3[:, tn - 1, COL_PR:COL_PR + RWKV_PROJ][None],
               s_s[None])
    return (y_prompt, y_sample) + p_state + s_state
```

```python
import functools

import numpy as np
import jax
import jax.numpy as jnp
from jax import lax
from jax.experimental import pallas as pl
from jax.experimental.pallas import tpu as pltpu

F32 = jnp.float32
BF16 = jnp.bfloat16

D_MODEL = 1024
HEAD_DIM = 64
NSA_WIDTH = 768
NSA_HEADS = 12
NSA_GROUPS = 2
NSA_HPG = 6
CMP_LEN = 32
CMP_STRIDE = 16
CMP_HID = 64
SLC_BLOCK = 64
N_SELECT = 16
WINDOW = 512
Q_BLOCK = 64
FORCE_SCORE = 1e4
RWKV_WIDTH = 768
RWKV_HEAD_DIM = 64
RWKV_HEADS = 12
DECAY_LORA = 64
ICL_LORA = 64
GATE_LORA = 128
RWKV_PROJ = 3 * RWKV_WIDTH + DECAY_LORA + ICL_LORA + GATE_LORA
GN_EPS = 64e-5
MEM_HEADS = 4
MEM_WIDTH = 512
MEM_HEAD_DIM = 128
N_BRANCHES = 3
NSA_KV_COLS = 3 * 2 * NSA_GROUPS * HEAD_DIM
GROUP_LANES = NSA_GROUPS * HEAD_DIM
RMS_EPS = 1e-6
NEG = -1e30
LOG2E = 1.4426950408889634

COL_Q = 0
COL_KV = 768
COL_PR = 1536
COL_MG = 4096
COL_MQ = 7168
COL_GN = 7680
SLAB_COLS = 8192
PAGES_PER_STEP = 8

VMEM_LIMIT = 56 * 1024 * 1024


def _dot(a, b):
    return jnp.dot(a, b, preferred_element_type=F32)


def _dot_nt(a, b):
    return lax.dot_general(a, b, (((1,), (1,)), ((), ())), preferred_element_type=F32)


def _dot_tn(a, b):
    return lax.dot_general(a, b, (((0,), (0,)), ((), ())), preferred_element_type=F32)


def _iota(shape, dim):
    return lax.broadcasted_iota(jnp.int32, shape, dim)


def _eye(n, dtype):
    return (_iota((n, n), 0) == _iota((n, n), 1)).astype(dtype)


def _sigmoid(x):
    return 1.0 / (1.0 + jnp.exp(-x))


def _softplus(z):
    return jnp.maximum(z, 0.0) + jnp.log(1.0 + jnp.exp(-jnp.abs(z)))


def _gelu_tanh(x):
    return 0.5 * x * (1.0 + jnp.tanh(np.sqrt(2.0 / np.pi).astype(np.float32) * (x + 0.044715 * (x * x * x))))


def _rms(x, g):
    ms = jnp.mean(x * x, axis=-1, keepdims=True)
    return (x * lax.rsqrt(ms + RMS_EPS)) * g


def _pad_rows(x, n):
    if x.shape[0] == n:
        return x
    return jnp.concatenate([x, jnp.zeros((n - x.shape[0],) + x.shape[1:], x.dtype)], axis=0)


def _norm_matmul_body(x_ref, g_ref, w_ref, o_ref, h_ref):
    @pl.when(pl.program_id(1) == 0)
    def _():
        h_ref[...] = _rms(x_ref[...], g_ref[...]).astype(BF16)

    o_ref[...] = _dot(h_ref[...], w_ref[...])


def norm_matmul(x, g, w, tm, tn):
    n, d = x.shape
    c = w.shape[1]
    return pl.pallas_call(
        _norm_matmul_body,
        grid=(n // tm, c // tn),
        in_specs=[pl.BlockSpec((tm, d), lambda i, j: (i, 0)),
                  pl.BlockSpec((1, d), lambda i, j: (0, 0)),
                  pl.BlockSpec((d, tn), lambda i, j: (0, j))],
        out_specs=pl.BlockSpec((tm, tn), lambda i, j: (i, j)),
        out_shape=jax.ShapeDtypeStruct((n, c), F32),
        scratch_shapes=[pltpu.VMEM((tm, d), BF16)],
        compiler_params=pltpu.CompilerParams(dimension_semantics=("parallel", "arbitrary"),
                                             vmem_limit_bytes=VMEM_LIMIT),
    )(x, g, w)


def _proj_prompt_body(x_ref, g_ref, w_ref, wkv_ref, wkvt_ref, o_ref, kvt_ref, ktb_ref, vb_ref, ck_ref, cv_ref, h_ref):
    @pl.when(pl.program_id(1) == 0)
    def _():
        h = _rms(x_ref[...], g_ref[...]).astype(BF16)
        h_ref[...] = h
        gl = GROUP_LANES
        kvt = _dot_nt(wkvt_ref[...], h)
        kvt_ref[0] = kvt
        ktb_ref[0] = jnp.concatenate([kvt[2 * gl:3 * gl], kvt[4 * gl:5 * gl]], axis=0).astype(BF16)
        kv = _dot(h, wkv_ref[...])
        ck_ref[...] = kv[:, 0:gl].astype(BF16)
        cv_ref[...] = kv[:, gl:2 * gl].astype(BF16)
        vb_ref[...] = jnp.concatenate([kv[:, 3 * gl:4 * gl], kv[:, 5 * gl:6 * gl]], axis=1).astype(BF16)

    o_ref[...] = _dot(h_ref[...], w_ref[...])


def proj_prompt(x, g, w, wkv, wkvt, b, t, tm, tn):
    n, d = x.shape
    c = w.shape[1]
    tpb = t // tm
    gl = GROUP_LANES
    full = lambda a: pl.BlockSpec(a.shape, lambda i, j: (0, 0))
    rows = lambda width: pl.BlockSpec((tm, width), lambda i, j: (i, 0))
    return pl.pallas_call(
        _proj_prompt_body,
        grid=(n // tm, c // tn),
        in_specs=[pl.BlockSpec((tm, d), lambda i, j: (i, 0)), full(g),
                  pl.BlockSpec((d, tn), lambda i, j: (0, j)), full(wkv), full(wkvt)],
        out_specs=[pl.BlockSpec((tm, tn), lambda i, j: (i, j)),
                   pl.BlockSpec((1, NSA_KV_COLS, tm), lambda i, j: (i // tpb, 0, i % tpb)),
                   pl.BlockSpec((1, 2 * gl, tm), lambda i, j: (i // tpb, 0, i % tpb)),
                   rows(2 * gl), rows(gl), rows(gl)],
        out_shape=[jax.ShapeDtypeStruct((n, c), F32),
                   jax.ShapeDtypeStruct((b, NSA_KV_COLS, t), F32),
                   jax.ShapeDtypeStruct((b, 2 * gl, t), BF16),
                   jax.ShapeDtypeStruct((n, 2 * gl), BF16),
                   jax.ShapeDtypeStruct((n, gl), BF16),
                   jax.ShapeDtypeStruct((n, gl), BF16)],
        scratch_shapes=[pltpu.VMEM((tm, d), BF16)],
        compiler_params=pltpu.CompilerParams(dimension_semantics=("parallel", "arbitrary"),
                                             vmem_limit_bytes=VMEM_LIMIT),
    )(x, g, w, wkv, wkvt)


def _compress_consts(pe_ref, b1_ref, w0_ref, w1_ref):
    pe0 = jnp.broadcast_to(pe_ref[0], (8, pe_ref.shape[2])).astype(BF16)
    pe1 = jnp.broadcast_to(pe_ref[1], (8, pe_ref.shape[2])).astype(BF16)
    c = _dot(pe0, w0_ref[...]) + _dot(pe1, w1_ref[...])
    return c[0:1] + b1_ref[...]


def _compress_finish(u0, u1, cst, w2):
    n = u0.shape[0]
    pre = u0 + pltpu.roll(u1, n - 1, 0) + cst
    out = _dot(_gelu_tanh(pre).astype(BF16), w2)
    return jnp.where(_iota(out.shape, 0) < n - 1, out, 0.0)


def _compress_prompt_body(xk_ref, xv_ref, pek_ref, b1k_ref, w0k_ref, w1k_ref, w2k_ref,
                          pev_ref, b1v_ref, w0v_ref, w1v_ref, w2v_ref, ok_ref, ov_ref):
    for x_ref, pe_ref, b1_ref, w0_ref, w1_ref, w2_ref, o_ref in (
            (xk_ref, pek_ref, b1k_ref, w0k_ref, w1k_ref, w2k_ref, ok_ref),
            (xv_ref, pev_ref, b1v_ref, w0v_ref, w1v_ref, w2v_ref, ov_ref)):
        x = x_ref[0].astype(BF16)
        cst = _compress_consts(pe_ref, b1_ref, w0_ref, w1_ref)
        res = _compress_finish(_dot(x, w0_ref[...]), _dot(x, w1_ref[...]), cst, w2_ref[...])
        for g in range(NSA_GROUPS):
            o_ref[0, g] = res[:, g * HEAD_DIM:(g + 1) * HEAD_DIM].astype(o_ref.dtype)


def _compress_weights(pe, w1, b1, w2):
    r = CMP_LEN // CMP_STRIDE
    eye = jnp.eye(NSA_GROUPS, dtype=F32)
    w1r = w1.reshape(r, CMP_STRIDE, HEAD_DIM, CMP_HID)
    w1e = jnp.einsum('icdh,gk->icgdkh', w1r, eye).reshape(r, CMP_STRIDE * NSA_GROUPS * HEAD_DIM,
                                                         NSA_GROUPS * CMP_HID)
    pee = jnp.broadcast_to(pe.reshape(r, CMP_STRIDE, 1, HEAD_DIM), (r, CMP_STRIDE, NSA_GROUPS, HEAD_DIM))
    pee = pee.reshape(r, 1, CMP_STRIDE * NSA_GROUPS * HEAD_DIM)
    b1e = jnp.tile(b1, NSA_GROUPS).reshape(1, NSA_GROUPS * CMP_HID)
    w2e = jnp.einsum('hd,gk->ghkd', w2, eye).reshape(NSA_GROUPS * CMP_HID, NSA_GROUPS * HEAD_DIM)
    return pee, b1e, w1e[0].astype(BF16), w1e[1].astype(BF16), w2e.astype(BF16)


def compress_prompt(xk, xv, wk, wv):
    b, nch, width = xk.shape
    full = lambda a: pl.BlockSpec(a.shape, lambda i: (0,) * a.ndim)
    xspec = pl.BlockSpec((1, nch, width), lambda i: (i, 0, 0))
    ospec = pl.BlockSpec((1, NSA_GROUPS, nch, HEAD_DIM), lambda i: (i, 0, 0, 0))
    oshape = jax.ShapeDtypeStruct((b, NSA_GROUPS, nch, HEAD_DIM), BF16)
    return pl.pallas_call(
        _compress_prompt_body,
        grid=(b,),
        in_specs=[xspec, xspec] + [full(a) for a in wk] + [full(a) for a in wv],
        out_specs=[ospec, ospec],
        out_shape=[oshape, oshape],
        compiler_params=pltpu.CompilerParams(dimension_semantics=("parallel",), vmem_limit_bytes=VMEM_LIMIT),
    )(xk, xv, *wk, *wv)


def _select_blocks(score_ref, n_rows):
    score = score_ref[...]
    j_idx = _iota(score.shape, 0)

    def body(jp, cnt):
        row = score_ref[pl.ds(jp, 1), :]
        ahead = (row > score) | ((row == score) & (jp < j_idx))
        return cnt + jnp.where(ahead, 1.0, 0.0)

    cnt = lax.fori_loop(0, n_rows, body, jnp.zeros(score.shape, F32))
    return jnp.where((cnt < N_SELECT) & (score > -jnp.inf), 1.0, 0.0)


def _select_blocks_unrolled(score):
    nb = score.shape[0]
    rows = [score[jp:jp + 1, :] for jp in range(nb)]
    cnts = []
    for r in range(nb // 8):
        blk = score[8 * r:8 * r + 8]
        jj = 8 * r + _iota(blk.shape, 0)
        cnt = jnp.zeros(blk.shape, F32)
        for jp in range(nb):
            ge = jnp.where(rows[jp] >= blk, 1.0, 0.0)
            gt = jnp.where(rows[jp] > blk, 1.0, 0.0)
            if jp < 8 * r:
                cnt = cnt + ge
            elif jp >= 8 * r + 8:
                cnt = cnt + gt
            else:
                cnt = cnt + jnp.where(jj > jp, ge, gt)
        cnts.append(cnt)
    cnt = jnp.concatenate(cnts, axis=0)
    return jnp.where((cnt < N_SELECT) & (score > -jnp.inf), 1.0, 0.0)


def _masked_softmax_rows(s, valid):
    s = jnp.where(valid, s, NEG)
    m = jnp.max(s, axis=-1, keepdims=True)
    p = jnp.where(valid, jnp.exp2(s - m), 0.0)
    l = jnp.sum(p, axis=-1, keepdims=True)
    return p / jnp.where(l > 0.0, l, 1.0)


def _online_update(carry, s, valid, v, pv=_dot):
    m, l, acc = carry
    s = jnp.where(valid, s, NEG)
    m_new = jnp.maximum(m, jnp.max(s, axis=-1, keepdims=True))
    alpha = jnp.exp2(m - m_new)
    p = jnp.where(valid, jnp.exp2(s - m_new), 0.0)
    l = alpha * l + jnp.sum(p, axis=-1, keepdims=True)
    acc = alpha * acc + pv(p.astype(BF16), v)
    return m_new, l, acc


def _online_update_biased(carry, s, v):
    m, l, acc = carry
    m_new = jnp.maximum(m, jnp.max(s, axis=-1, keepdims=True))
    alpha = jnp.exp2(m - m_new)
    p = jnp.exp2(s - m_new)
    l = alpha * l + jnp.sum(p, axis=-1, keepdims=True)
    acc = alpha * acc + _dot(p.astype(BF16), v)
    return m_new, l, acc


KV_TILE = 512
WIN_TILE = 640


def _nsa_prompt_body(q_ref, gn_ref, kc_ref, vc_ref, kts_ref, vs_ref, ktw_ref, vw_ref, ovt_ref, o_ref):
    g = pl.program_id(1)
    qb = pl.program_id(2)
    nq = Q_BLOCK
    hpg = NSA_HPG
    rows = hpg * nq
    qf = q_ref[...]
    q2f = jnp.concatenate([qf[:, h * HEAD_DIM:(h + 1) * HEAD_DIM] for h in range(hpg)], axis=0)
    q2 = q2f.astype(BF16)
    zero = jnp.zeros_like(q2f)
    q2w = jnp.where(g == 0, jnp.concatenate([q2f, zero], axis=1), jnp.concatenate([zero, q2f], axis=1)).astype(BF16)
    t_q1 = qb * nq + _iota((nq, 1), 0)
    tile6 = lambda x: jnp.concatenate([x] * hpg, axis=0)
    pick = lambda x: jnp.where(g == 0, x[:, 0:HEAD_DIM], x[:, HEAD_DIM:2 * HEAD_DIM])

    kc = kc_ref[0, 0]
    ncp = kc.shape[0]
    c_end = _iota((nq, ncp), 1) * CMP_STRIDE + (CMP_LEN - 1)
    s_c = _dot_nt(q2, kc) + tile6(jnp.where(c_end <= t_q1, 0.0, NEG))
    e_c = jnp.exp2(s_c - jnp.max(s_c, axis=-1, keepdims=True))
    l_c = jnp.sum(e_c, axis=-1, keepdims=True)
    any_c = tile6(t_q1 >= CMP_LEN - 1)
    p_c = e_c * jnp.where(any_c, 1.0 / l_c, 0.0)
    o_c = _dot(p_c.astype(BF16), vc_ref[0, 0])
    psum = p_c[0:nq]
    for h in range(1, hpg):
        psum = psum + p_c[h * nq:(h + 1) * nq]

    imp_t = lax.dot_general(ovt_ref[...], psum, (((1,), (1,)), ((), ())),
                            precision=lax.Precision.HIGHEST, preferred_element_type=F32)
    j_idx = _iota(imp_t.shape, 0)
    forced = (j_idx == 0) | (j_idx == qb) | (j_idx == qb - 1)
    score = jnp.where(forced, FORCE_SCORE, imp_t)
    sel_t = _select_blocks_unrolled(jnp.where(j_idx <= qb, score, -jnp.inf))
    sel = _dot_nt(_eye(nq, BF16), sel_t.astype(BF16)).astype(BF16)

    w0 = pl.multiple_of(jnp.maximum(qb * nq - WINDOW, 0) // 128 * 128, 128)
    diff = t_q1 - (w0 + _iota((nq, WIN_TILE), 1))
    ok_w = (diff >= 0) & (diff <= WINDOW)
    s_w = _dot(q2w, ktw_ref[0, :, pl.ds(w0, WIN_TILE)]) + tile6(jnp.where(ok_w, 0.0, NEG))
    e_w = jnp.exp2(s_w - jnp.max(s_w, axis=-1, keepdims=True))
    l_w = jnp.sum(e_w, axis=-1, keepdims=True)
    o_w = pick(_dot(e_w.astype(BF16), vw_ref[0, pl.ds(w0, WIN_TILE), :])) / l_w

    bpt = KV_TILE // SLC_BLOCK
    col_blk = _iota((sel.shape[1], KV_TILE), 1) // SLC_BLOCK
    row_blk = _iota((sel.shape[1], KV_TILE), 0)

    def block_mask(kt):
        expand = jnp.where(row_blk == col_blk + kt * bpt, 1.0, 0.0).astype(BF16)
        return _dot(sel, expand) > 0.5

    def scores(kt):
        off = pl.multiple_of(kt * KV_TILE, KV_TILE)
        return _dot(q2w, kts_ref[0, :, pl.ds(off, KV_TILE)])

    def values(kt):
        off = pl.multiple_of(kt * KV_TILE, KV_TILE)
        return vs_ref[0, pl.ds(off, KV_TILE), :]

    nt = qb // bpt
    ok_d = block_mask(nt) & (_iota((nq, KV_TILE), 1) + nt * KV_TILE <= t_q1)
    init = (jnp.full((rows, 1), NEG, F32), jnp.zeros((rows, 1), F32), jnp.zeros((rows, GROUP_LANES), F32))
    carry = _online_update_biased(init, scores(nt) + tile6(jnp.where(ok_d, 0.0, NEG)), values(nt))

    def pair_step(i, carry):
        k0 = 2 * i
        k1 = 2 * i + 1
        s0 = scores(k0)
        s1 = scores(k1)
        b0 = jnp.where(block_mask(k0), 0.0, NEG)
        b1 = jnp.where(block_mask(k1) & (k1 < nt), 0.0, NEG)
        carry = _online_update_biased(carry, s0 + tile6(b0), values(k0))
        return _online_update_biased(carry, s1 + tile6(b1), values(k1))

    _, l_s, acc_s = lax.fori_loop(0, (nt + 1) // 2, pair_step, carry)
    o_s = pick(acc_s) / l_s

    gates = _sigmoid(gn_ref[...])
    per_group = hpg * 3
    gates = jnp.where(g == 0, gates[:, 0:per_group], gates[:, per_group:2 * per_group])
    outs = []
    for h in range(hpg):
        sl = slice(h * nq, (h + 1) * nq)
        outs.append(gates[:, 3 * h:3 * h + 1] * o_c[sl] + gates[:, 3 * h + 1:3 * h + 2] * o_s[sl]
                    + gates[:, 3 * h + 2:3 * h + 3] * o_w[sl])
    o_ref[...] = jnp.concatenate(outs, axis=1).astype(o_ref.dtype)


def _overlap_t(n_blocks, n_cmp_padded, n_cmp):
    i = np.arange(n_cmp_padded)[None, :] * CMP_STRIDE
    j = np.arange(n_blocks)[:, None] * SLC_BLOCK
    ov = (i < j + SLC_BLOCK) & (i + CMP_LEN > j) & (np.arange(n_cmp_padded)[None, :] < n_cmp)
    return jnp.asarray(ov.astype(np.float32))


def nsa_prompt(slab, kc, vc, ktb, vb, b, t):
    nb = t // Q_BLOCK
    gw = NSA_HPG * HEAD_DIM
    gl = GROUP_LANES
    ncp = kc.shape[2]
    ovt = _overlap_t(nb, ncp, ncp - 1)
    kt_spec = lambda k: pl.BlockSpec((1, gl, t), lambda bi, g, qb: (bi, k, 0))
    v_spec = lambda k: pl.BlockSpec((1, t, gl), lambda bi, g, qb: (bi, 0, k))
    cmp_spec = pl.BlockSpec((1, 1, ncp, HEAD_DIM), lambda bi, g, qb: (bi, g, 0, 0))
    return pl.pallas_call(
        _nsa_prompt_body,
        grid=(b, NSA_GROUPS, nb),
        in_specs=[pl.BlockSpec((Q_BLOCK, gw), lambda bi, g, qb: (bi * nb + qb, g)),
                  pl.BlockSpec((Q_BLOCK, 128), lambda bi, g, qb: (bi * nb + qb, COL_GN // 128)),
                  cmp_spec, cmp_spec, kt_spec(0), v_spec(0), kt_spec(1), v_spec(1),
                  pl.BlockSpec(ovt.shape, lambda bi, g, qb: (0, 0))],
        out_specs=pl.BlockSpec((Q_BLOCK, gw), lambda bi, g, qb: (bi * nb + qb, g)),
        out_shape=jax.ShapeDtypeStruct((b * t, NSA_WIDTH), BF16),
        compiler_params=pltpu.CompilerParams(dimension_semantics=("parallel", "parallel", "arbitrary"),
                                             vmem_limit_bytes=VMEM_LIMIT),
    )(slab, slab, kc, vc, ktb, vb, ktb, vb, ovt)


def _compress_sample_body(pt_ref, *refs):
    n = PAGES_PER_STEP
    k_pages, v_pages = refs[0:n], refs[n:2 * n]
    (pek_ref, b1k_ref, w0k_ref, w1k_ref, w2k_ref, wck_ref,
     pev_ref, b1v_ref, w0v_ref, w1v_ref, w2v_ref, wcv_ref, ok_ref, ov_ref, uk_ref, uv_ref) = refs[2 * n:]
    j = pl.program_id(1)
    rows = n * k_pages[0].shape[1]
    off = pl.multiple_of(j * rows, rows)
    for pages, wc_ref, u_ref in ((k_pages, wck_ref, uk_ref), (v_pages, wcv_ref, uv_ref)):
        x = jnp.concatenate([r[0] for r in pages], axis=0).astype(BF16)
        u_ref[pl.ds(off, rows), :] = _dot(x, wc_ref[...])

    @pl.when(j == pl.num_programs(1) - 1)
    def _():
        half = NSA_GROUPS * CMP_HID
        for pe_ref, b1_ref, w0_ref, w1_ref, w2_ref, u_ref, o_ref in (
                (pek_ref, b1k_ref, w0k_ref, w1k_ref, w2k_ref, uk_ref, ok_ref),
                (pev_ref, b1v_ref, w0v_ref, w1v_ref, w2v_ref, uv_ref, ov_ref)):
            cst = _compress_consts(pe_ref, b1_ref, w0_ref, w1_ref)
            u = u_ref[...]
            res = _compress_finish(u[:, 0:half], u[:, half:2 * half], cst, w2_ref[...])
            for g in range(NSA_GROUPS):
                o_ref[0, g] = res[:, g * HEAD_DIM:(g + 1) * HEAD_DIM].astype(o_ref.dtype)


def compress_sample(pool_k, pool_v, page_table, wk, wv):
    bs, n_pages = page_table.shape
    _, cpp, width = pool_k.shape
    n = PAGES_PER_STEP
    nch = n_pages * cpp
    wck = jnp.concatenate([wk[2], wk[3]], axis=1)
    wcv = jnp.concatenate([wv[2], wv[3]], axis=1)
    page = lambda k: pl.BlockSpec((1, cpp, width), lambda b, j, pt: (pt[b, n * j + k], 0, 0))
    full = lambda a: pl.BlockSpec(a.shape, lambda b, j, pt: (0,) * a.ndim)
    consts = list(wk) + [wck] + list(wv) + [wcv]
    ospec = pl.BlockSpec((1, NSA_GROUPS, nch, HEAD_DIM), lambda b, j, pt: (b, 0, 0, 0))
    oshape = jax.ShapeDtypeStruct((bs, NSA_GROUPS, nch, HEAD_DIM), BF16)
    return pl.pallas_call(
        _compress_sample_body,
        grid_spec=pltpu.PrefetchScalarGridSpec(
            num_scalar_prefetch=1,
            grid=(bs, n_pages // n),
            in_specs=[page(k) for k in range(n)] * 2 + [full(a) for a in consts],
            out_specs=[ospec, ospec],
            scratch_shapes=[pltpu.VMEM((nch, 2 * NSA_GROUPS * CMP_HID), F32)] * 2),
        out_shape=[oshape, oshape],
        compiler_params=pltpu.CompilerParams(dimension_semantics=("parallel", "arbitrary"),
                                             vmem_limit_bytes=VMEM_LIMIT),
    )(page_table, *([pool_k] * n), *([pool_v] * n), *consts)


TOK_PAD = 8


def _nsa_sample_body(past_len, tn, pt_ref, *refs):
    n = PAGES_PER_STEP
    q_ref, gn_ref, skn_ref, svn_ref, wkn_ref, wvn_ref, kc_ref, vc_ref, wkc_ref, wvc_ref = refs[0:10]
    k_pages, v_pages = refs[10:10 + n], refs[10 + n:10 + 2 * n]
    ovt_ref, o_ref, score_ref, sel_ref, m_ref, l_ref, acc_ref, oc_ref, ow_ref = refs[10 + 2 * n:]
    j = pl.program_id(1)
    tp = TOK_PAD
    rows = NSA_HPG * tp
    gw = NSA_HPG * HEAD_DIM
    lanes = GROUP_LANES
    nsp = score_ref.shape[0]

    q8 = _pad_rows(q_ref[0], tp)
    zero = jnp.zeros((rows, HEAD_DIM), F32)
    q2, q2w = [], []
    for g in range(NSA_GROUPS):
        qg = jnp.concatenate([q8[:, g * gw + h * HEAD_DIM:g * gw + (h + 1) * HEAD_DIM] for h in range(NSA_HPG)], axis=0)
        q2.append(qg.astype(BF16))
        q2w.append(jnp.concatenate([qg, zero] if g == 0 else [zero, qg], axis=1).astype(BF16))
    tok = _iota((rows, 1), 0) % tp
    t_q = past_len + tok

    def new_keys_valid(width):
        tk = _iota((rows, width), 1)
        return (tk <= tok) & (tk < tn)

    @pl.when(j == 0)
    def _():
        psums = []
        for g in range(NSA_GROUPS):
            kc = kc_ref[0, g]
            s_c = _dot_nt(q2[g], kc)
            c_end = _iota(s_c.shape, 1) * CMP_STRIDE + (CMP_LEN - 1)
            p_c = _masked_softmax_rows(s_c, c_end <= t_q)
            oc_ref[g] = _dot(p_c.astype(BF16), vc_ref[0, g])
            ps = p_c[0:tp]
            for h in range(1, NSA_HPG):
                ps = ps + p_c[h * tp:(h + 1) * tp]
            psums.append(ps)
        psum = jnp.concatenate(psums, axis=0)
        imp_t = lax.dot_general(ovt_ref[...], psum, (((1,), (1,)), ((), ())),
                                precision=lax.Precision.HIGHEST, preferred_element_type=F32)
        j_idx = _iota(imp_t.shape, 0)
        cur = (past_len + _iota(imp_t.shape, 1) % tp) // SLC_BLOCK
        forced = (j_idx == 0) | (j_idx == cur) | (j_idx == cur - 1)
        score = jnp.where(forced, FORCE_SCORE, imp_t)
        score_ref[...] = jnp.where(j_idx <= cur, score, -jnp.inf)
        sel_t = _select_blocks(score_ref, (past_len + tn - 1) // SLC_BLOCK + 1)
        sel_ref[...] = _dot_nt(_eye(NSA_GROUPS * tp, BF16), sel_t.astype(BF16))

        lw = wkc_ref.shape[2]
        kwt = wkc_ref[0].astype(BF16)
        vwt = wvc_ref[0].astype(BF16)
        kwn = _pad_rows(wkn_ref[0], 16).astype(BF16)
        vwn = _pad_rows(wvn_ref[0], 16).astype(BF16)
        diff = t_q - (past_len - lw + _iota((rows, lw), 1))
        valid_c = (diff >= 0) & (diff <= WINDOW)
        valid_n = new_keys_valid(16)
        for g in range(NSA_GROUPS):
            carry = (jnp.full((rows, 1), NEG, F32), jnp.zeros((rows, 1), F32), jnp.zeros((rows, lanes), F32))
            carry = _online_update(carry, _dot(q2w[g], kwt), valid_c, vwt, pv=_dot_nt)
            _, l_w, acc_w = _online_update(carry, _dot_nt(q2w[g], kwn), valid_n, vwn)
            ow_ref[g] = acc_w / l_w
            m_ref[g] = jnp.full((rows, 1), NEG, F32)
            l_ref[g] = jnp.zeros((rows, 1), F32)
            acc_ref[g] = jnp.zeros((rows, lanes), F32)

    page_rows = k_pages[0].shape[2]
    nk = n * page_rows
    kt = jnp.concatenate([r[0] for r in k_pages], axis=1).astype(BF16)
    vt = jnp.concatenate([r[0] for r in v_pages], axis=1).astype(BF16)
    expand = jnp.where(_iota((nsp, nk), 0) == _iota((nsp, nk), 1) // SLC_BLOCK + j * (nk // SLC_BLOCK),
                       1.0, 0.0).astype(BF16)
    mask_all = _dot(sel_ref[...].astype(BF16), expand) > 0.5
    for g in range(NSA_GROUPS):
        valid = jnp.concatenate([mask_all[g * tp:(g + 1) * tp]] * NSA_HPG, axis=0)
        m, l, acc = _online_update((m_ref[g], l_ref[g], acc_ref[g]), _dot(q2w[g], kt), valid, vt, pv=_dot_nt)
        m_ref[g], l_ref[g], acc_ref[g] = m, l, acc

    @pl.when(j == pl.num_programs(1) - 1)
    def _():
        kn = _pad_rows(skn_ref[0], 16).astype(BF16)
        vn = _pad_rows(svn_ref[0], 16).astype(BF16)
        expand_n = jnp.where(_iota((nsp, 16), 0) == (past_len + _iota((nsp, 16), 1)) // SLC_BLOCK,
                             1.0, 0.0).astype(BF16)
        sel_n = _dot(sel_ref[...].astype(BF16), expand_n) > 0.5
        gates = _sigmoid(_pad_rows(gn_ref[0], tp))
        outs = []
        for g in range(NSA_GROUPS):
            valid = jnp.concatenate([sel_n[g * tp:(g + 1) * tp]] * NSA_HPG, axis=0) & new_keys_valid(16)
            _, l_s, acc_s = _online_update((m_ref[g], l_ref[g], acc_ref[g]), _dot_nt(q2w[g], kn), valid, vn)
            o_s = (acc_s / l_s)[:, g * HEAD_DIM:(g + 1) * HEAD_DIM]
            o_w = ow_ref[g][:, g * HEAD_DIM:(g + 1) * HEAD_DIM]
            o_c = oc_ref[g]
            for h in range(NSA_HPG):
                sl = slice(h * tp, (h + 1) * tp)
                c0 = (g * NSA_HPG + h) * 3
                outs.append(gates[:, c0:c0 + 1] * o_c[sl] + gates[:, c0 + 1:c0 + 2] * o_s[sl]
                            + gates[:, c0 + 2:c0 + 3] * o_w[sl])
        o_ref[0] = jnp.concatenate(outs, axis=1)[0:tn].astype(o_ref.dtype)


def nsa_sample(slab3, kc, vc, win_kt, win_vt, pool_kt, pool_vt, page_table, past_len):
    bs, tn, _ = slab3.shape
    n_pages = page_table.shape[1]
    n = PAGES_PER_STEP
    page_rows = pool_kt.shape[2]
    lanes = GROUP_LANES
    ncp = kc.shape[2]
    ns = -(-(past_len + tn) // SLC_BLOCK)
    nsp = -(-ns // 8) * 8
    ovt = _overlap_t(nsp, ncp, (past_len + tn) // CMP_STRIDE - CMP_LEN // CMP_STRIDE + 1)
    rows = NSA_HPG * TOK_PAD
    tokblk = lambda width, col: pl.BlockSpec((1, tn, width), lambda b, j, pt: (b, 0, col // width))
    cmp_spec = pl.BlockSpec((1, NSA_GROUPS, ncp, HEAD_DIM), lambda b, j, pt: (b, 0, 0, 0))
    win_spec = pl.BlockSpec((1, lanes, win_kt.shape[2]), lambda b, j, pt: (b, 0, 0))
    page = lambda k: pl.BlockSpec((1, lanes, page_rows), lambda b, j, pt: (pt[b, n * j + k], 0, 0))
    return pl.pallas_call(
        functools.partial(_nsa_sample_body, past_len, tn),
        grid_spec=pltpu.PrefetchScalarGridSpec(
            num_scalar_prefetch=1,
            grid=(bs, n_pages // n),
            in_specs=[tokblk(NSA_WIDTH, COL_Q), tokblk(128, COL_GN),
                      tokblk(lanes, COL_KV + 2 * lanes), tokblk(lanes, COL_KV + 3 * lanes),
                      tokblk(lanes, COL_KV + 4 * lanes), tokblk(lanes, COL_KV + 5 * lanes),
                      cmp_spec, cmp_spec, win_spec, win_spec]
            + [page(k) for k in range(n)] * 2
            + [pl.BlockSpec(ovt.shape, lambda b, j, pt: (0, 0))],
            out_specs=pl.BlockSpec((1, tn, NSA_WIDTH), lambda b, j, pt: (b, 0, 0)),
            scratch_shapes=[pltpu.VMEM((nsp, NSA_GROUPS * TOK_PAD), F32),
                            pltpu.VMEM((NSA_GROUPS * TOK_PAD, nsp), F32),
                            pltpu.VMEM((NSA_GROUPS, rows, 1), F32),
                            pltpu.VMEM((NSA_GROUPS, rows, 1), F32),
                            pltpu.VMEM((NSA_GROUPS, rows, lanes), F32),
                            pltpu.VMEM((NSA_GROUPS, rows, HEAD_DIM), F32),
                            pltpu.VMEM((NSA_GROUPS, rows, lanes), F32)]),
        out_shape=jax.ShapeDtypeStruct((bs, tn, NSA_WIDTH), F32),
        compiler_params=pltpu.CompilerParams(dimension_semantics=("parallel", "arbitrary"),
                                             vmem_limit_bytes=VMEM_LIMIT),
    )(page_table, slab3, slab3, slab3, slab3, slab3, slab3, kc, vc, win_kt, win_vt,
      *([pool_kt] * n), *([pool_vt] * n), ovt)


def _cumsum_rows(x):
    n = x.shape[0]
    row = _iota((n, 1), 0)
    k = 1
    while k < n:
        x = x + jnp.where(row >= k, pltpu.roll(x, k, 0), 0.0)
        k *= 2
    return x


def _rwkv_body(n_valid, chunk, p0_ref, p1_ref, p2_ref, p3_ref, p4_ref, prev_ref, s0_ref,
               mu_ref, w0_ref, w2_ref, a0_ref, a2_ref, g2_ref, kk_ref, ka_ref, rk_ref, lng_ref, lnb_ref,
               o_ref, sout_ref, carry_ref, s_ref):
    c = pl.program_id(1)
    hd = RWKV_HEAD_DIM

    @pl.when(c == 0)
    def _():
        carry_ref[...] = jnp.broadcast_to(prev_ref[0], carry_ref.shape)
        s_ref[...] = s0_ref[0]

    p = jnp.concatenate([r[0] for r in (p0_ref, p1_ref, p2_ref, p3_ref, p4_ref)], axis=1)
    p = _pad_rows(p, chunk)
    row = _iota((chunk, 1), 0)
    valid = row < n_valid
    prev = jnp.where(row == 0, carry_ref[0:1, :], pltpu.roll(p, 1, 0))
    xm = p + (prev - p) * mu_ref[...]
    carry_ref[...] = jnp.broadcast_to(p[n_valid - 1:n_valid, :], carry_ref.shape)

    wdt = RWKV_WIDTH
    r_all, k_all, v_all = xm[:, 0:wdt], xm[:, wdt:2 * wdt], xm[:, 2 * wdt:3 * wdt]
    o = 3 * wdt
    wd, ad, gd = xm[:, o:o + DECAY_LORA], xm[:, o + DECAY_LORA:o + DECAY_LORA + ICL_LORA], \
        xm[:, o + DECAY_LORA + ICL_LORA:o + DECAY_LORA + ICL_LORA + GATE_LORA]
    w = w0_ref[...] + _dot(jnp.tanh(wd).astype(BF16), w2_ref[...])
    logw = -jnp.exp(-_softplus(-w) - 0.5)
    a_all = _sigmoid(a0_ref[...] + _dot(ad.astype(BF16), a2_ref[...]))
    g_all = _dot(_sigmoid(gd).astype(BF16), g2_ref[...])
    logw = jnp.where(valid, logw, 0.0)
    cum = _cumsum_rows(logw)
    total = cum[chunk - 1:chunk, :]
    w_in = jnp.exp(cum)
    w_ex = jnp.exp(cum - logw)
    w_inv = jnp.exp(-cum)
    w_rem = jnp.exp(total - cum)
    w_tot = jnp.exp(total)
    kk_all = k_all * kk_ref[...]
    k2_all = k_all * (1.0 + (a_all - 1.0) * ka_ref[...])

    t_i = _iota((chunk, chunk), 0)
    s_i = _iota((chunk, chunk), 1)
    strict = s_i < t_i
    incl = s_i <= t_i
    n_rounds = int(np.log2(chunk))
    heads = range(RWKV_HEADS)
    sls = [slice(h * hd, (h + 1) * hd) for h in heads]

    lr, bt, kt, bk, vb, at, rt = [], [], [], [], [], [], []
    for sl in sls:
        kkh = kk_all[:, sl]
        nrm = jnp.sqrt(jnp.sum(kkh * kkh, axis=-1, keepdims=True))
        kkh = jnp.where(valid, kkh / jnp.maximum(nrm, 1e-12), 0.0)
        k2h = jnp.where(valid, k2_all[:, sl], 0.0)
        vh = jnp.where(valid, v_all[:, sl], 0.0)
        bh = kkh * a_all[:, sl]
        a_t = -kkh * w_ex[:, sl]
        r_t = r_all[:, sl] * w_in[:, sl]
        at.append(a_t.astype(BF16))
        rt.append(r_t.astype(BF16))
        lr.append(jnp.concatenate([a_t, r_t], axis=0).astype(BF16))
        bt.append((bh * w_inv[:, sl]).astype(BF16))
        kt.append((k2h * w_inv[:, sl]).astype(BF16))
        bk.append(jnp.concatenate([bh * w_rem[:, sl], k2h * w_rem[:, sl]], axis=0).astype(BF16))
        vb.append(vh)
    m_b = [_dot_nt(lr[h], bt[h]) for h in heads]
    m_k = [_dot_nt(lr[h], kt[h]) for h in heads]
    a_ab = [jnp.where(strict, m[0:chunk], 0.0) for m in m_b]
    a_rb = [jnp.where(incl, m[chunk:2 * chunk], 0.0).astype(BF16) for m in m_b]
    a_ak = [jnp.where(strict, m[0:chunk], 0.0).astype(BF16) for m in m_k]
    a_rk = [jnp.where(incl, m[chunk:2 * chunk], 0.0).astype(BF16) for m in m_k]
    s0 = [s_ref[h] for h in heads]
    s0b = [x.astype(BF16) for x in s0]
    vbb = [x.astype(BF16) for x in vb]
    u = [_dot_nt(at[h], s0b[h]) + _dot(a_ak[h], vbb[h]) for h in heads]
    pw = a_ab
    for it in range(n_rounds):
        pwb = [x.astype(BF16) for x in pw]
        u = [u[h] + _dot(pwb[h], u[h].astype(BF16)) for h in heads]
        if it + 1 < n_rounds:
            pw = [_dot(x, x) for x in pwb]
    ub = [x.astype(BF16) for x in u]
    y = [_dot_nt(rt[h], s0b[h]) + _dot(a_rb[h], ub[h]) + _dot(a_rk[h], vbb[h]) for h in heads]
    for h in heads:
        uv = jnp.concatenate([u[h], vb[h]], axis=0).astype(BF16)
        s_ref[h] = s0[h] * w_tot[:, sls[h]] + _dot_tn(uv, bk[h])
    outs = []
    for h in heads:
        sl = sls[h]
        mean = jnp.mean(y[h], axis=-1, keepdims=True)
        yc = y[h] - mean
        var = jnp.mean(yc * yc, axis=-1, keepdims=True)
        yn = yc * lax.rsqrt(var + GN_EPS)
        bonus = jnp.sum(r_all[:, sl] * k2_all[:, sl] * rk_ref[:, sl], axis=-1, keepdims=True) * v_all[:, sl]
        outs.append((yn * lng_ref[:, sl] + lnb_ref[:, sl] + bonus) * g_all[:, sl])
    out = jnp.concatenate(outs, axis=1)
    o_ref[0] = out[0:o_ref.shape[1]].astype(o_ref.dtype)

    @pl.when(c == pl.num_programs(1) - 1)
    def _():
        sout_ref[0] = s_ref[...]


def rwkv(slab3, p_prev, s0, params, chunk, out_dtype):
    b, t, _ = slab3.shape
    tc = min(t, chunk)
    nchunks = t // tc
    blk = 512
    pspec = lambda k: pl.BlockSpec((1, tc, blk), lambda bi, c: (bi, c, COL_PR // blk + k))
    full = lambda a: pl.BlockSpec(a.shape, lambda bi, c: (0,) * a.ndim)
    sspec = pl.BlockSpec((1, RWKV_HEADS, RWKV_HEAD_DIM, RWKV_HEAD_DIM), lambda bi, c: (bi, 0, 0, 0))
    return pl.pallas_call(
        functools.partial(_rwkv_body, tc, chunk),
        grid=(b, nchunks),
        in_specs=[pspec(k) for k in range(5)]
        + [pl.BlockSpec((1, 1, RWKV_PROJ), lambda bi, c: (bi, 0, 0)), sspec]
        + [full(a) for a in params],
        out_specs=[pl.BlockSpec((1, tc, RWKV_WIDTH), lambda bi, c: (bi, c, 0)), sspec],
        out_shape=[jax.ShapeDtypeStruct((b, t, RWKV_WIDTH), out_dtype),
                   jax.ShapeDtypeStruct(s0.shape, F32)],
        scratch_shapes=[pltpu.VMEM((8, RWKV_PROJ), F32),
                        pltpu.VMEM((RWKV_HEADS, RWKV_HEAD_DIM, RWKV_HEAD_DIM), F32)],
        compiler_params=pltpu.CompilerParams(dimension_semantics=("parallel", "arbitrary"),
                                             vmem_limit_bytes=VMEM_LIMIT),
    )(slab3, slab3, slab3, slab3, slab3, p_prev, s0, *params)


def _mem_attend_body(q_ref, k_ref, v_ref, o_ref):
    tm = q_ref.shape[1]
    q = _pad_rows(q_ref[0], max(tm, 16)).astype(BF16)
    k = k_ref[0].astype(BF16)
    v = v_ref[0].astype(BF16)
    outs = []
    for h in range(MEM_HEADS):
        sl = slice(h * MEM_HEAD_DIM, (h + 1) * MEM_HEAD_DIM)
        s = _dot_nt(q[:, sl], k[:, sl]) * (MEM_HEAD_DIM ** -0.5)
        m = jnp.max(s, axis=-1, keepdims=True)
        p = jnp.exp(s - m)
        p = p / jnp.sum(p, axis=-1, keepdims=True)
        outs.append(_dot(p.astype(BF16), v[:, sl]))
    o_ref[0] = jnp.concatenate(outs, axis=1)[0:tm].astype(o_ref.dtype)


def mem_attend(slab3, mk, k_blk, mv, v_blk, tm, out_dtype):
    b, t, _ = slab3.shape
    m = mk.shape[1]
    return pl.pallas_call(
        _mem_attend_body,
        grid=(b, t // tm),
        in_specs=[pl.BlockSpec((1, tm, MEM_WIDTH), lambda bi, i: (bi, i, COL_MQ // MEM_WIDTH)),
                  pl.BlockSpec((1, m, MEM_WIDTH), lambda bi, i: (bi, 0, k_blk)),
                  pl.BlockSpec((1, m, MEM_WIDTH), lambda bi, i: (bi, 0, v_blk))],
        out_specs=pl.BlockSpec((1, tm, MEM_WIDTH), lambda bi, i: (bi, i, 0)),
        out_shape=jax.ShapeDtypeStruct((b, t, MEM_WIDTH), out_dtype),
        compiler_params=pltpu.CompilerParams(dimension_semantics=("parallel", "parallel"),
                                             vmem_limit_bytes=VMEM_LIMIT),
    )(slab3, mk, mv)


def _merge_body(x_ref, on_ref, or_ref, om_ref, g0_ref, g1_ref, g2_ref, wn_ref, wr_ref, wm_ref, wo_ref, o_ref):
    m = _sigmoid(g0_ref[...]) * _dot(on_ref[...].astype(BF16), wn_ref[...])
    m = m + _sigmoid(g1_ref[...]) * _dot(or_ref[...].astype(BF16), wr_ref[...])
    m = m + _sigmoid(g2_ref[...]) * _dot(om_ref[...].astype(BF16), wm_ref[...])
    o_ref[...] = x_ref[...] + _dot(m.astype(BF16), wo_ref[...])


def merge(x, o_nsa, o_rwkv, o_mem, slab, wn, wr, wm, wo, tm):
    n, d = x.shape
    row = lambda w: pl.BlockSpec((tm, w), lambda i: (i, 0))
    full = lambda a: pl.BlockSpec(a.shape, lambda i: (0, 0))
    gate = lambda k: pl.BlockSpec((tm, d), lambda i: (i, COL_MG // d + k))
    return pl.pallas_call(
        _merge_body,
        grid=(n // tm,),
        in_specs=[row(d), row(NSA_WIDTH), row(RWKV_WIDTH), row(MEM_WIDTH), gate(0), gate(1), gate(2),
                  full(wn), full(wr), full(wm), full(wo)],
        out_specs=row(d),
        out_shape=jax.ShapeDtypeStruct((n, d), F32),
        compiler_params=pltpu.CompilerParams(dimension_semantics=("parallel",), vmem_limit_bytes=VMEM_LIMIT),
    )(x, o_nsa, o_rwkv, o_mem, slab, slab, slab, wn, wr, wm, wo)


def _ffn_body(x_ref, gf_ref, wg_ref, wu_ref, wd_ref, gl_ref, o_ref):
    x = x_ref[...]
    hf = _rms(x, gf_ref[...]).astype(BF16)
    gate = _dot(hf, wg_ref[...])
    up = _dot(hf, wu_ref[...])
    act = (gate * _sigmoid(gate) * up).astype(BF16)
    x2 = x + _dot(act, wd_ref[...])
    o_ref[...] = _rms(x2, gl_ref[...])


def ffn(x, gf, wg, wu, wd, gl, tm):
    n, d = x.shape
    row = pl.BlockSpec((tm, d), lambda i: (i, 0))
    full = lambda a: pl.BlockSpec(a.shape, lambda i: (0, 0), pipeline_mode=pl.Buffered(1))
    return pl.pallas_call(
        _ffn_body,
        grid=(n // tm,),
        in_specs=[row, full(gf), full(wg), full(wu), full(wd), full(gl)],
        out_specs=row,
        out_shape=jax.ShapeDtypeStruct((n, d), F32),
        compiler_params=pltpu.CompilerParams(dimension_semantics=("parallel",), vmem_limit_bytes=VMEM_LIMIT),
    )(x, gf, wg, wu, wd, gl)


def _slab_weight(w_in):
    wq, wkv, wgn, wpr, wmq, wmg = jnp.split(w_in, np.cumsum(
        [NSA_WIDTH, NSA_KV_COLS, 3 * NSA_HEADS, RWKV_PROJ, MEM_WIDTH])[:5].tolist(), axis=1)
    d = w_in.shape[0]
    pad = jnp.zeros((d, SLAB_COLS - COL_GN - 3 * NSA_HEADS), w_in.dtype)
    w = jnp.concatenate([wq * (HEAD_DIM ** -0.5 * LOG2E), wkv, wpr, wmg, wmq, wgn, pad], axis=1)
    return w.astype(BF16), wkv.astype(BF16), wkv.T.astype(BF16)


def _channel_major_rows(x):
    b, _, t = x.shape
    return jnp.transpose(x.reshape(b, NSA_GROUPS, HEAD_DIM, t), (0, 3, 1, 2))[None]


def _channel_major_view(x):
    b, t = x.shape[:2]
    return jnp.transpose(x, (0, 2, 3, 1)).reshape(b, GROUP_LANES, t)


def kernel(x_prompt, x_sample, cache_cmp_k, cache_cmp_v, cache_slc_k, cache_slc_v, cache_win_k, cache_win_v, state_rwkv_shift, state_rwkv_wkv, cache_mem_k, cache_mem_v, page_table, mem_prompt, attn_norm, w_in, cmp_pe_k, cmp_w1_k, cmp_b1_k, cmp_w2_k, cmp_pe_v, cmp_w1_v, cmp_b1_v, cmp_w2_v, rwkv_mu, rwkv_w0, rwkv_w2, rwkv_a0, rwkv_a2, rwkv_g2, rwkv_kk, rwkv_ka, rwkv_rk, rwkv_ln_g, rwkv_ln_b, mem_norm, w_mem_kv, w_o_nsa, w_o_rwkv, w_o_mem, w_out, ffn_norm, w_gate, w_up, w_down, final_norm):
    assert w_in.shape[0] == 1, "one layer"
    bp, t, d = x_prompt.shape
    bs, tn, _ = x_sample.shape
    row2 = lambda a: a.reshape(1, -1)
    gl_ = GROUP_LANES

    w_slab, w_kv, w_kvt = _slab_weight(w_in[0])
    cmp_wk = _compress_weights(cmp_pe_k[0], cmp_w1_k[0], cmp_b1_k[0], cmp_w2_k[0])
    cmp_wv = _compress_weights(cmp_pe_v[0], cmp_w1_v[0], cmp_b1_v[0], cmp_w2_v[0])
    rw_params = (row2(rwkv_mu[0]), row2(rwkv_w0[0]), rwkv_w2[0].astype(BF16), row2(rwkv_a0[0]),
                 rwkv_a2[0].astype(BF16), rwkv_g2[0].astype(BF16), row2(rwkv_kk[0]), row2(rwkv_ka[0]),
                 row2(rwkv_rk[0]), row2(rwkv_ln_g[0]), row2(rwkv_ln_b[0]))
    wn, wr, wm, wo = (a[0].astype(BF16) for a in (w_o_nsa, w_o_rwkv, w_o_mem, w_out))
    wg, wu, wd = (a[0].astype(BF16) for a in (w_gate, w_up, w_down))
    gf, gl = row2(ffn_norm[0]), row2(final_norm)

    xp2 = x_prompt.reshape(bp * t, d)
    slab, kvt, ktb, vb, ck, cv = proj_prompt(xp2, row2(attn_norm[0]), w_slab, w_kv, w_kvt, bp, t, 1024, 512)
    slab3 = slab.reshape(bp, t, SLAB_COLS)
    nch = t // CMP_STRIDE
    kc, vc = compress_prompt(ck.reshape(bp, nch, CMP_STRIDE * gl_), cv.reshape(bp, nch, CMP_STRIDE * gl_),
                             cmp_wk, cmp_wv)
    o_nsa = nsa_prompt(slab, kc, vc, ktb, vb.reshape(bp, t, 2 * gl_), bp, t)
    o_rwkv, s_p = rwkv(slab3, jnp.zeros((bp, 1, RWKV_PROJ), F32),
                       jnp.zeros((bp, RWKV_HEADS, RWKV_HEAD_DIM, RWKV_HEAD_DIM), F32), rw_params, 64, BF16)
    mem_n = mem_prompt.shape[1]
    mkv = norm_matmul(mem_prompt.reshape(bp * mem_n, d), row2(mem_norm[0]), w_mem_kv[0].astype(BF16),
                      min(1024, bp * mem_n), 512).reshape(bp, mem_n, 2 * MEM_WIDTH)
    o_mem = mem_attend(slab3, mkv, 0, mkv, 1, 512, BF16)
    x1 = merge(xp2, o_nsa, o_rwkv.reshape(bp * t, RWKV_WIDTH), o_mem.reshape(bp * t, MEM_WIDTH), slab,
               wn, wr, wm, wo, 512)
    y_prompt = ffn(x1, gf, wg, wu, wd, gl, 256).reshape(bp, t, d)

    wp0 = max(t - WINDOW, 0)
    stream = lambda i: kvt[:, i * gl_:(i + 1) * gl_, :]
    p_state = (_channel_major_rows(stream(0)), _channel_major_rows(stream(1)),
               _channel_major_rows(stream(2)), _channel_major_rows(stream(3)),
               _channel_major_rows(stream(4)[:, :, wp0:]), _channel_major_rows(stream(5)[:, :, wp0:]),
               slab3[:, t - 1, COL_PR:COL_PR + RWKV_PROJ][None],
               s_p[None],
               mkv[:, :, :MEM_WIDTH].reshape(1, bp, mem_n, MEM_HEADS, MEM_HEAD_DIM),
               mkv[:, :, MEM_WIDTH:].reshape(1, bp, mem_n, MEM_HEADS, MEM_HEAD_DIM))

    past_len = page_table.shape[1] * cache_cmp_k.shape[2]
    assert page_table.shape[1] % PAGES_PER_STEP == 0 and past_len % SLC_BLOCK == 0
    assert (past_len + tn) // CMP_STRIDE == past_len // CMP_STRIDE and tn <= TOK_PAD
    xs2 = x_sample.reshape(bs * tn, d)
    slab_s = norm_matmul(xs2, row2(attn_norm[0]), w_slab, bs * tn, 512)
    slab_s3 = slab_s.reshape(bs, tn, SLAB_COLS)
    n_phys, page_rows = cache_cmp_k.shape[1], cache_cmp_k.shape[2]
    cpp = page_rows // CMP_STRIDE
    chunked = lambda pool: pool[0].reshape(n_phys, cpp, CMP_STRIDE * gl_)
    kc_s, vc_s = compress_sample(chunked(cache_cmp_k), chunked(cache_cmp_v), page_table, cmp_wk, cmp_wv)
    o_nsa_s = nsa_sample(slab_s3, kc_s, vc_s, _channel_major_view(cache_win_k[0]), _channel_major_view(cache_win_v[0]),
                         _channel_major_view(cache_slc_k[0]), _channel_major_view(cache_slc_v[0]),
                         page_table, past_len)
    o_rwkv_s, s_s = rwkv(slab_s3, state_rwkv_shift[0][:, None, :], state_rwkv_wkv[0], rw_params, 16, F32)
    mem_s = cache_mem_k.shape[2]
    o_mem_s = mem_attend(slab_s3, cache_mem_k[0].reshape(bs, mem_s, MEM_WIDTH), 0,
                         cache_mem_v[0].reshape(bs, mem_s, MEM_WIDTH), 0, tn, F32)
    x1s = merge(xs2, o_nsa_s.reshape(bs * tn, NSA_WIDTH), o_rwkv_s.reshape(bs * tn, RWKV_WIDTH),
                o_mem_s.reshape(bs * tn, MEM_WIDTH), slab_s, wn, wr, wm, wo, min(512, bs * tn))
    y_sample = ffn(x1s, gf, wg, wu, wd, gl, min(256, bs * tn)).reshape(bs, tn, d)
    heads = lambda a: a.reshape(1, a.shape[0], a.shape[1], NSA_GROUPS, HEAD_DIM)
    kv_new = [slab_s3[:, :, COL_KV + i * gl_:COL_KV + (i + 1) * gl_] for i in range(6)]
    s_state = (heads(kv_new[0]), heads(kv_new[1]), heads(kv_new[2]), heads(kv_new[3]),
               jnp.concatenate([cache_win_k[0], heads(kv_new[4])[0]], axis=1)[:, tn:][None],
               jnp.concatenate([cache_win_v[0], heads(kv_new[5])[0]], axis=1)[:, tn:][None],
               slab_s3[:, tn - 1, COL_PR:COL_PR + RWKV_PROJ][None],
               s_s[None])
    return (y_prompt, y_sample) + p_state + s_state
```

```python
import functools

import numpy as np
import jax
import jax.numpy as jnp
from jax import lax
from jax.experimental import pallas as pl
from jax.experimental.pallas import tpu as pltpu

F32 = jnp.float32
BF16 = jnp.bfloat16

D_MODEL = 1024
HEAD_DIM = 64
NSA_WIDTH = 768
NSA_HEADS = 12
NSA_GROUPS = 2
NSA_HPG = 6
CMP_LEN = 32
CMP_STRIDE = 16
CMP_HID = 64
SLC_BLOCK = 64
N_SELECT = 16
WINDOW = 512
Q_BLOCK = 64
FORCE_SCORE = 1e4
RWKV_WIDTH = 768
RWKV_HEAD_DIM = 64
RWKV_HEADS = 12
DECAY_LORA = 64
ICL_LORA = 64
GATE_LORA = 128
RWKV_PROJ = 3 * RWKV_WIDTH + DECAY_LORA + ICL_LORA + GATE_LORA
GN_EPS = 64e-5
MEM_HEADS = 4
MEM_WIDTH = 512
MEM_HEAD_DIM = 128
N_BRANCHES = 3
NSA_KV_COLS = 3 * 2 * NSA_GROUPS * HEAD_DIM
GROUP_LANES = NSA_GROUPS * HEAD_DIM
RMS_EPS = 1e-6
NEG = -1e30
LOG2E = 1.4426950408889634

COL_Q = 0
COL_KV = 768
COL_PR = 1536
COL_MG = 4096
COL_MQ = 7168
COL_GN = 7680
SLAB_COLS = 8192
MAX_PAGES_PER_STEP = 16


def _pages_per_step(n_pages):
    n = min(MAX_PAGES_PER_STEP, n_pages)
    assert n_pages % n == 0
    return n

VMEM_LIMIT = 56 * 1024 * 1024


def _dot(a, b):
    return jnp.dot(a, b, preferred_element_type=F32)


def _dot_nt(a, b):
    return lax.dot_general(a, b, (((1,), (1,)), ((), ())), preferred_element_type=F32)


def _dot_tn(a, b):
    return lax.dot_general(a, b, (((0,), (0,)), ((), ())), preferred_element_type=F32)


def _iota(shape, dim):
    return lax.broadcasted_iota(jnp.int32, shape, dim)


def _eye(n, dtype):
    return (_iota((n, n), 0) == _iota((n, n), 1)).astype(dtype)


def _sigmoid(x):
    return 1.0 / (1.0 + jnp.exp(-x))


def _softplus(z):
    return jnp.maximum(z, 0.0) + jnp.log(1.0 + jnp.exp(-jnp.abs(z)))


def _gelu_tanh(x):
    return 0.5 * x * (1.0 + jnp.tanh(np.sqrt(2.0 / np.pi).astype(np.float32) * (x + 0.044715 * (x * x * x))))


def _rms(x, g):
    ms = jnp.mean(x * x, axis=-1, keepdims=True)
    return (x * lax.rsqrt(ms + RMS_EPS)) * g


def _pad_rows(x, n):
    if x.shape[0] == n:
        return x
    return jnp.concatenate([x, jnp.zeros((n - x.shape[0],) + x.shape[1:], x.dtype)], axis=0)


def _norm_matmul_body(x_ref, g_ref, w_ref, o_ref, h_ref):
    @pl.when(pl.program_id(1) == 0)
    def _():
        h_ref[...] = _rms(x_ref[...], g_ref[...]).astype(BF16)

    o_ref[...] = _dot(h_ref[...], w_ref[...])


def norm_matmul(x, g, w, tm, tn):
    n, d = x.shape
    c = w.shape[1]
    return pl.pallas_call(
        _norm_matmul_body,
        grid=(n // tm, c // tn),
        in_specs=[pl.BlockSpec((tm, d), lambda i, j: (i, 0)),
                  pl.BlockSpec((1, d), lambda i, j: (0, 0)),
                  pl.BlockSpec((d, tn), lambda i, j: (0, j))],
        out_specs=pl.BlockSpec((tm, tn), lambda i, j: (i, j)),
        out_shape=jax.ShapeDtypeStruct((n, c), F32),
        scratch_shapes=[pltpu.VMEM((tm, d), BF16)],
        compiler_params=pltpu.CompilerParams(dimension_semantics=("parallel", "arbitrary"),
                                             vmem_limit_bytes=VMEM_LIMIT),
    )(x, g, w)


def _proj_prompt_body(x_ref, g_ref, w_ref, wkv_ref, wkvt_ref, o_ref, kvt_ref, ktb_ref, vb_ref, ck_ref, cv_ref, h_ref):
    @pl.when(pl.program_id(1) == 0)
    def _():
        h = _rms(x_ref[...], g_ref[...]).astype(BF16)
        h_ref[...] = h
        gl = GROUP_LANES
        kvt = _dot_nt(wkvt_ref[...], h)
        kvt_ref[0] = kvt
        ktb_ref[0] = jnp.concatenate([kvt[2 * gl:3 * gl], kvt[4 * gl:5 * gl]], axis=0).astype(BF16)
        kv = _dot(h, wkv_ref[...])
        ck_ref[...] = kv[:, 0:gl].astype(BF16)
        cv_ref[...] = kv[:, gl:2 * gl].astype(BF16)
        vb_ref[...] = jnp.concatenate([kv[:, 3 * gl:4 * gl], kv[:, 5 * gl:6 * gl]], axis=1).astype(BF16)

    o_ref[...] = _dot(h_ref[...], w_ref[...])


def proj_prompt(x, g, w, wkv, wkvt, b, t, tm, tn):
    n, d = x.shape
    c = w.shape[1]
    tpb = t // tm
    gl = GROUP_LANES
    full = lambda a: pl.BlockSpec(a.shape, lambda i, j: (0, 0))
    rows = lambda width: pl.BlockSpec((tm, width), lambda i, j: (i, 0))
    return pl.pallas_call(
        _proj_prompt_body,
        grid=(n // tm, c // tn),
        in_specs=[pl.BlockSpec((tm, d), lambda i, j: (i, 0)), full(g),
                  pl.BlockSpec((d, tn), lambda i, j: (0, j)), full(wkv), full(wkvt)],
        out_specs=[pl.BlockSpec((tm, tn), lambda i, j: (i, j)),
                   pl.BlockSpec((1, NSA_KV_COLS, tm), lambda i, j: (i // tpb, 0, i % tpb)),
                   pl.BlockSpec((1, 2 * gl, tm), lambda i, j: (i // tpb, 0, i % tpb)),
                   rows(2 * gl), rows(gl), rows(gl)],
        out_shape=[jax.ShapeDtypeStruct((n, c), F32),
                   jax.ShapeDtypeStruct((b, NSA_KV_COLS, t), F32),
                   jax.ShapeDtypeStruct((b, 2 * gl, t), BF16),
                   jax.ShapeDtypeStruct((n, 2 * gl), BF16),
                   jax.ShapeDtypeStruct((n, gl), BF16),
                   jax.ShapeDtypeStruct((n, gl), BF16)],
        scratch_shapes=[pltpu.VMEM((tm, d), BF16)],
        compiler_params=pltpu.CompilerParams(dimension_semantics=("parallel", "arbitrary"),
                                             vmem_limit_bytes=VMEM_LIMIT),
    )(x, g, w, wkv, wkvt)


def _compress_consts(pe_ref, b1_ref, w0_ref, w1_ref):
    pe0 = jnp.broadcast_to(pe_ref[0], (8, pe_ref.shape[2])).astype(BF16)
    pe1 = jnp.broadcast_to(pe_ref[1], (8, pe_ref.shape[2])).astype(BF16)
    c = _dot(pe0, w0_ref[...]) + _dot(pe1, w1_ref[...])
    return c[0:1] + b1_ref[...]


def _compress_finish(u0, u1, cst, w2):
    n = u0.shape[0]
    pre = u0 + pltpu.roll(u1, n - 1, 0) + cst
    out = _dot(_gelu_tanh(pre).astype(BF16), w2)
    return jnp.where(_iota(out.shape, 0) < n - 1, out, 0.0)


def _compress_prompt_body(xk_ref, xv_ref, pek_ref, b1k_ref, w0k_ref, w1k_ref, w2k_ref,
                          pev_ref, b1v_ref, w0v_ref, w1v_ref, w2v_ref, ok_ref, ov_ref):
    for x_ref, pe_ref, b1_ref, w0_ref, w1_ref, w2_ref, o_ref in (
            (xk_ref, pek_ref, b1k_ref, w0k_ref, w1k_ref, w2k_ref, ok_ref),
            (xv_ref, pev_ref, b1v_ref, w0v_ref, w1v_ref, w2v_ref, ov_ref)):
        x = x_ref[0].astype(BF16)
        cst = _compress_consts(pe_ref, b1_ref, w0_ref, w1_ref)
        res = _compress_finish(_dot(x, w0_ref[...]), _dot(x, w1_ref[...]), cst, w2_ref[...])
        for g in range(NSA_GROUPS):
            o_ref[0, g] = res[:, g * HEAD_DIM:(g + 1) * HEAD_DIM].astype(o_ref.dtype)


def _compress_weights(pe, w1, b1, w2):
    r = CMP_LEN // CMP_STRIDE
    eye = jnp.eye(NSA_GROUPS, dtype=F32)
    w1r = w1.reshape(r, CMP_STRIDE, HEAD_DIM, CMP_HID)
    w1e = jnp.einsum('icdh,gk->icgdkh', w1r, eye).reshape(r, CMP_STRIDE * NSA_GROUPS * HEAD_DIM,
                                                         NSA_GROUPS * CMP_HID)
    pee = jnp.broadcast_to(pe.reshape(r, CMP_STRIDE, 1, HEAD_DIM), (r, CMP_STRIDE, NSA_GROUPS, HEAD_DIM))
    pee = pee.reshape(r, 1, CMP_STRIDE * NSA_GROUPS * HEAD_DIM)
    b1e = jnp.tile(b1, NSA_GROUPS).reshape(1, NSA_GROUPS * CMP_HID)
    w2e = jnp.einsum('hd,gk->ghkd', w2, eye).reshape(NSA_GROUPS * CMP_HID, NSA_GROUPS * HEAD_DIM)
    return pee, b1e, w1e[0].astype(BF16), w1e[1].astype(BF16), w2e.astype(BF16)


def compress_prompt(xk, xv, wk, wv):
    b, nch, width = xk.shape
    full = lambda a: pl.BlockSpec(a.shape, lambda i: (0,) * a.ndim)
    xspec = pl.BlockSpec((1, nch, width), lambda i: (i, 0, 0))
    ospec = pl.BlockSpec((1, NSA_GROUPS, nch, HEAD_DIM), lambda i: (i, 0, 0, 0))
    oshape = jax.ShapeDtypeStruct((b, NSA_GROUPS, nch, HEAD_DIM), BF16)
    return pl.pallas_call(
        _compress_prompt_body,
        grid=(b,),
        in_specs=[xspec, xspec] + [full(a) for a in wk] + [full(a) for a in wv],
        out_specs=[ospec, ospec],
        out_shape=[oshape, oshape],
        compiler_params=pltpu.CompilerParams(dimension_semantics=("parallel",), vmem_limit_bytes=VMEM_LIMIT),
    )(xk, xv, *wk, *wv)


def _select_blocks(score_ref, n_rows):
    score = score_ref[...]
    j_idx = _iota(score.shape, 0)

    def body(jp, cnt):
        row = score_ref[pl.ds(jp, 1), :]
        ahead = (row > score) | ((row == score) & (jp < j_idx))
        return cnt + jnp.where(ahead, 1.0, 0.0)

    cnt = lax.fori_loop(0, n_rows, body, jnp.zeros(score.shape, F32))
    return jnp.where((cnt < N_SELECT) & (score > -jnp.inf), 1.0, 0.0)


def _select_blocks_unrolled(score):
    nb = score.shape[0]
    rows = [score[jp:jp + 1, :] for jp in range(nb)]
    cnts = []
    for r in range(nb // 8):
        blk = score[8 * r:8 * r + 8]
        jj = 8 * r + _iota(blk.shape, 0)
        cnt = jnp.zeros(blk.shape, F32)
        for jp in range(nb):
            ge = jnp.where(rows[jp] >= blk, 1.0, 0.0)
            gt = jnp.where(rows[jp] > blk, 1.0, 0.0)
            if jp < 8 * r:
                cnt = cnt + ge
            elif jp >= 8 * r + 8:
                cnt = cnt + gt
            else:
                cnt = cnt + jnp.where(jj > jp, ge, gt)
        cnts.append(cnt)
    cnt = jnp.concatenate(cnts, axis=0)
    return jnp.where((cnt < N_SELECT) & (score > -jnp.inf), 1.0, 0.0)


def _masked_softmax_rows(s, valid):
    s = jnp.where(valid, s, NEG)
    m = jnp.max(s, axis=-1, keepdims=True)
    p = jnp.where(valid, jnp.exp2(s - m), 0.0)
    l = jnp.sum(p, axis=-1, keepdims=True)
    return p / jnp.where(l > 0.0, l, 1.0)


def _online_update(carry, s, valid, v, pv=_dot):
    m, l, acc = carry
    s = jnp.where(valid, s, NEG)
    m_new = jnp.maximum(m, jnp.max(s, axis=-1, keepdims=True))
    alpha = jnp.exp2(m - m_new)
    p = jnp.where(valid, jnp.exp2(s - m_new), 0.0)
    l = alpha * l + jnp.sum(p, axis=-1, keepdims=True)
    acc = alpha * acc + pv(p.astype(BF16), v)
    return m_new, l, acc


def _online_update_biased(carry, s, v):
    m, l, acc = carry
    m_new = jnp.maximum(m, jnp.max(s, axis=-1, keepdims=True))
    alpha = jnp.exp2(m - m_new)
    p = jnp.exp2(s - m_new)
    l = alpha * l + jnp.sum(p, axis=-1, keepdims=True)
    acc = alpha * acc + _dot(p.astype(BF16), v)
    return m_new, l, acc


KV_TILE = 512
WIN_TILE = 640


def _nsa_prompt_body(q_ref, gn_ref, kc_ref, vc_ref, kts_ref, vs_ref, ktw_ref, vw_ref, ovt_ref, o_ref):
    g = pl.program_id(1)
    qb = pl.program_id(2)
    nq = Q_BLOCK
    hpg = NSA_HPG
    rows = hpg * nq
    qf = q_ref[...]
    q2f = jnp.concatenate([qf[:, h * HEAD_DIM:(h + 1) * HEAD_DIM] for h in range(hpg)], axis=0)
    q2 = q2f.astype(BF16)
    zero = jnp.zeros_like(q2f)
    q2w = jnp.where(g == 0, jnp.concatenate([q2f, zero], axis=1), jnp.concatenate([zero, q2f], axis=1)).astype(BF16)
    t_q1 = qb * nq + _iota((nq, 1), 0)
    tile6 = lambda x: jnp.concatenate([x] * hpg, axis=0)
    pick = lambda x: jnp.where(g == 0, x[:, 0:HEAD_DIM], x[:, HEAD_DIM:2 * HEAD_DIM])

    kc = kc_ref[0, 0]
    ncp = kc.shape[0]
    c_end = _iota((nq, ncp), 1) * CMP_STRIDE + (CMP_LEN - 1)
    s_c = _dot_nt(q2, kc) + tile6(jnp.where(c_end <= t_q1, 0.0, NEG))
    e_c = jnp.exp2(s_c - jnp.max(s_c, axis=-1, keepdims=True))
    l_c = jnp.sum(e_c, axis=-1, keepdims=True)
    any_c = tile6(t_q1 >= CMP_LEN - 1)
    p_c = e_c * jnp.where(any_c, 1.0 / l_c, 0.0)
    o_c = _dot(p_c.astype(BF16), vc_ref[0, 0])
    psum = p_c[0:nq]
    for h in range(1, hpg):
        psum = psum + p_c[h * nq:(h + 1) * nq]

    imp_t = lax.dot_general(ovt_ref[...], psum, (((1,), (1,)), ((), ())),
                            precision=lax.Precision.HIGHEST, preferred_element_type=F32)
    j_idx = _iota(imp_t.shape, 0)
    forced = (j_idx == 0) | (j_idx == qb) | (j_idx == qb - 1)
    score = jnp.where(forced, FORCE_SCORE, imp_t)
    sel_t = _select_blocks_unrolled(jnp.where(j_idx <= qb, score, -jnp.inf))
    sel = _dot_nt(_eye(nq, BF16), sel_t.astype(BF16)).astype(BF16)

    w0 = pl.multiple_of(jnp.maximum(qb * nq - WINDOW, 0) // 128 * 128, 128)
    diff = t_q1 - (w0 + _iota((nq, WIN_TILE), 1))
    ok_w = (diff >= 0) & (diff <= WINDOW)
    s_w = _dot(q2w, ktw_ref[0, :, pl.ds(w0, WIN_TILE)]) + tile6(jnp.where(ok_w, 0.0, NEG))
    e_w = jnp.exp2(s_w - jnp.max(s_w, axis=-1, keepdims=True))
    l_w = jnp.sum(e_w, axis=-1, keepdims=True)
    o_w = pick(_dot(e_w.astype(BF16), vw_ref[0, pl.ds(w0, WIN_TILE), :])) / l_w

    bpt = KV_TILE // SLC_BLOCK
    col_blk = _iota((sel.shape[1], KV_TILE), 1) // SLC_BLOCK
    row_blk = _iota((sel.shape[1], KV_TILE), 0)

    def block_mask(kt):
        expand = jnp.where(row_blk == col_blk + kt * bpt, 1.0, 0.0).astype(BF16)
        return _dot(sel, expand) > 0.5

    def scores(kt):
        off = pl.multiple_of(kt * KV_TILE, KV_TILE)
        return _dot(q2w, kts_ref[0, :, pl.ds(off, KV_TILE)])

    def values(kt):
        off = pl.multiple_of(kt * KV_TILE, KV_TILE)
        return vs_ref[0, pl.ds(off, KV_TILE), :]

    nt = qb // bpt
    ok_d = block_mask(nt) & (_iota((nq, KV_TILE), 1) + nt * KV_TILE <= t_q1)
    init = (jnp.full((rows, 1), NEG, F32), jnp.zeros((rows, 1), F32), jnp.zeros((rows, GROUP_LANES), F32))
    carry = _online_update_biased(init, scores(nt) + tile6(jnp.where(ok_d, 0.0, NEG)), values(nt))

    def pair_step(i, carry):
        k0 = 2 * i
        k1 = 2 * i + 1
        s0 = scores(k0)
        s1 = scores(k1)
        b0 = jnp.where(block_mask(k0), 0.0, NEG)
        b1 = jnp.where(block_mask(k1) & (k1 < nt), 0.0, NEG)
        carry = _online_update_biased(carry, s0 + tile6(b0), values(k0))
        return _online_update_biased(carry, s1 + tile6(b1), values(k1))

    _, l_s, acc_s = lax.fori_loop(0, (nt + 1) // 2, pair_step, carry)
    o_s = pick(acc_s) / l_s

    gates = _sigmoid(gn_ref[...])
    per_group = hpg * 3
    gates = jnp.where(g == 0, gates[:, 0:per_group], gates[:, per_group:2 * per_group])
    outs = []
    for h in range(hpg):
        sl = slice(h * nq, (h + 1) * nq)
        outs.append(gates[:, 3 * h:3 * h + 1] * o_c[sl] + gates[:, 3 * h + 1:3 * h + 2] * o_s[sl]
                    + gates[:, 3 * h + 2:3 * h + 3] * o_w[sl])
    o_ref[...] = jnp.concatenate(outs, axis=1).astype(o_ref.dtype)


def _overlap_t(n_blocks, n_cmp_padded, n_cmp):
    i = np.arange(n_cmp_padded)[None, :] * CMP_STRIDE
    j = np.arange(n_blocks)[:, None] * SLC_BLOCK
    ov = (i < j + SLC_BLOCK) & (i + CMP_LEN > j) & (np.arange(n_cmp_padded)[None, :] < n_cmp)
    return jnp.asarray(ov.astype(np.float32))


def nsa_prompt(slab, kc, vc, ktb, vb, b, t):
    nb = t // Q_BLOCK
    gw = NSA_HPG * HEAD_DIM
    gl = GROUP_LANES
    ncp = kc.shape[2]
    ovt = _overlap_t(nb, ncp, ncp - 1)
    kt_spec = lambda k: pl.BlockSpec((1, gl, t), lambda bi, g, qb: (bi, k, 0))
    v_spec = lambda k: pl.BlockSpec((1, t, gl), lambda bi, g, qb: (bi, 0, k))
    cmp_spec = pl.BlockSpec((1, 1, ncp, HEAD_DIM), lambda bi, g, qb: (bi, g, 0, 0))
    return pl.pallas_call(
        _nsa_prompt_body,
        grid=(b, NSA_GROUPS, nb),
        in_specs=[pl.BlockSpec((Q_BLOCK, gw), lambda bi, g, qb: (bi * nb + qb, g)),
                  pl.BlockSpec((Q_BLOCK, 128), lambda bi, g, qb: (bi * nb + qb, COL_GN // 128)),
                  cmp_spec, cmp_spec, kt_spec(0), v_spec(0), kt_spec(1), v_spec(1),
                  pl.BlockSpec(ovt.shape, lambda bi, g, qb: (0, 0))],
        out_specs=pl.BlockSpec((Q_BLOCK, gw), lambda bi, g, qb: (bi * nb + qb, g)),
        out_shape=jax.ShapeDtypeStruct((b * t, NSA_WIDTH), BF16),
        compiler_params=pltpu.CompilerParams(dimension_semantics=("parallel", "parallel", "arbitrary"),
                                             vmem_limit_bytes=VMEM_LIMIT),
    )(slab, slab, kc, vc, ktb, vb, ktb, vb, ovt)


def _compress_sample_body(n, pt_ref, *refs):
    k_pages, v_pages = refs[0:n], refs[n:2 * n]
    (pek_ref, b1k_ref, w0k_ref, w1k_ref, w2k_ref, wck_ref,
     pev_ref, b1v_ref, w0v_ref, w1v_ref, w2v_ref, wcv_ref, ok_ref, ov_ref, uk_ref, uv_ref, xs_ref) = refs[2 * n:]
    j = pl.program_id(1)
    lanes, page_rows = k_pages[0].shape[1], k_pages[0].shape[2]
    rows = n * page_rows // CMP_STRIDE
    off = pl.multiple_of(j * rows, rows)
    eye = _eye(lanes, BF16)
    for pages, wc_ref, u_ref in ((k_pages, wck_ref, uk_ref), (v_pages, wcv_ref, uv_ref)):
        for k, r in enumerate(pages):
            xs_ref[pl.ds(k * page_rows, page_rows), :] = _dot_tn(r[0].astype(BF16), eye)
        x = jnp.concatenate([xs_ref[pl.ds(c, rows, stride=CMP_STRIDE), :] for c in range(CMP_STRIDE)], axis=1)
        u_ref[pl.ds(off, rows), :] = _dot(x.astype(BF16), wc_ref[...])

    @pl.when(j == pl.num_programs(1) - 1)
    def _():
        half = NSA_GROUPS * CMP_HID
        for pe_ref, b1_ref, w0_ref, w1_ref, w2_ref, u_ref, o_ref in (
                (pek_ref, b1k_ref, w0k_ref, w1k_ref, w2k_ref, uk_ref, ok_ref),
                (pev_ref, b1v_ref, w0v_ref, w1v_ref, w2v_ref, uv_ref, ov_ref)):
            cst = _compress_consts(pe_ref, b1_ref, w0_ref, w1_ref)
            u = u_ref[...]
            res = _compress_finish(u[:, 0:half], u[:, half:2 * half], cst, w2_ref[...])
            for g in range(NSA_GROUPS):
                o_ref[0, g] = res[:, g * HEAD_DIM:(g + 1) * HEAD_DIM].astype(o_ref.dtype)


def compress_sample(pool_k, pool_v, page_table, wk, wv):
    bs, n_pages = page_table.shape
    _, lanes, page_rows = pool_k.shape
    n = _pages_per_step(n_pages)
    nch = n_pages * page_rows // CMP_STRIDE
    wck = jnp.concatenate([wk[2], wk[3]], axis=1)
    wcv = jnp.concatenate([wv[2], wv[3]], axis=1)
    page = lambda k: pl.BlockSpec((1, lanes, page_rows), lambda b, j, pt: (pt[b, n * j + k], 0, 0))
    full = lambda a: pl.BlockSpec(a.shape, lambda b, j, pt: (0,) * a.ndim)
    consts = list(wk) + [wck] + list(wv) + [wcv]
    ospec = pl.BlockSpec((1, NSA_GROUPS, nch, HEAD_DIM), lambda b, j, pt: (b, 0, 0, 0))
    oshape = jax.ShapeDtypeStruct((bs, NSA_GROUPS, nch, HEAD_DIM), BF16)
    return pl.pallas_call(
        functools.partial(_compress_sample_body, n),
        grid_spec=pltpu.PrefetchScalarGridSpec(
            num_scalar_prefetch=1,
            grid=(bs, n_pages // n),
            in_specs=[page(k) for k in range(n)] * 2 + [full(a) for a in consts],
            out_specs=[ospec, ospec],
            scratch_shapes=[pltpu.VMEM((nch, 2 * NSA_GROUPS * CMP_HID), F32)] * 2
            + [pltpu.VMEM((n * page_rows, lanes), F32)]),
        out_shape=[oshape, oshape],
        compiler_params=pltpu.CompilerParams(dimension_semantics=("parallel", "arbitrary"),
                                             vmem_limit_bytes=VMEM_LIMIT),
    )(page_table, *([pool_k] * n), *([pool_v] * n), *consts)


TOK_PAD = 8


def _nsa_sample_body(past_len, tn, n, pt_ref, *refs):
    q_ref, gn_ref, skn_ref, svn_ref, wkn_ref, wvn_ref, kc_ref, vc_ref, wkc_ref, wvc_ref = refs[0:10]
    k_pages, v_pages = refs[10:10 + n], refs[10 + n:10 + 2 * n]
    ovt_ref, o_ref, score_ref, selt_ref, m_ref, l_ref, acc_ref, oc_ref, ow_ref = refs[10 + 2 * n:]
    j = pl.program_id(1)
    tp = TOK_PAD
    hpg = NSA_HPG
    grows = hpg * tp
    rows = NSA_GROUPS * grows
    gw = hpg * HEAD_DIM
    lanes = GROUP_LANES
    nsp = score_ref.shape[0]

    q8 = _pad_rows(q_ref[0], tp)
    zero = jnp.zeros((grows, HEAD_DIM), F32)
    q2, q_parts = [], []
    for g in range(NSA_GROUPS):
        qg = jnp.concatenate([q8[:, g * gw + h * HEAD_DIM:g * gw + (h + 1) * HEAD_DIM] for h in range(hpg)], axis=0)
        q2.append(qg.astype(BF16))
        q_parts.append(jnp.concatenate([qg, zero] if g == 0 else [zero, qg], axis=1))
    q_all = jnp.concatenate(q_parts, axis=0).astype(BF16)
    tok = _iota((rows, 1), 0) % tp
    t_q = past_len + tok

    def stack_groups(x):
        return jnp.concatenate([x[g * tp:(g + 1) * tp] for g in range(NSA_GROUPS) for _ in range(hpg)], axis=0)

    def new_keys_valid(width):
        tk = _iota((rows, width), 1)
        return (tk <= tok) & (tk < tn)

    @pl.when(j == 0)
    def _():
        psums = []
        tq_g = t_q[0:grows]
        for g in range(NSA_GROUPS):
            s_c = _dot_nt(q2[g], kc_ref[0, g])
            c_end = _iota(s_c.shape, 1) * CMP_STRIDE + (CMP_LEN - 1)
            p_c = _masked_softmax_rows(s_c, c_end <= tq_g)
            oc_ref[g] = _dot(p_c.astype(BF16), vc_ref[0, g])
            ps = p_c[0:tp]
            for h in range(1, hpg):
                ps = ps + p_c[h * tp:(h + 1) * tp]
            psums.append(ps)
        psum = jnp.concatenate(psums, axis=0)
        imp_t = lax.dot_general(ovt_ref[...], psum, (((1,), (1,)), ((), ())),
                                precision=lax.Precision.HIGHEST, preferred_element_type=F32)
        j_idx = _iota(imp_t.shape, 0)
        cur = (past_len + _iota(imp_t.shape, 1) % tp) // SLC_BLOCK
        forced = (j_idx == 0) | (j_idx == cur) | (j_idx == cur - 1)
        score = jnp.where(forced, FORCE_SCORE, imp_t)
        score_ref[...] = jnp.where(j_idx <= cur, score, -jnp.inf)
        selt_ref[...] = _select_blocks(score_ref, (past_len + tn - 1) // SLC_BLOCK + 1)

        lw = wkc_ref.shape[2]
        kwn = _pad_rows(wkn_ref[0], 16).astype(BF16)
        vwn = _pad_rows(wvn_ref[0], 16).astype(BF16)
        diff = t_q - (past_len - lw + _iota((rows, lw), 1))
        carry = (jnp.full((rows, 1), NEG, F32), jnp.zeros((rows, 1), F32), jnp.zeros((rows, lanes), F32))
        carry = _online_update(carry, _dot(q_all, wkc_ref[0].astype(BF16)), (diff >= 0) & (diff <= WINDOW),
                               wvc_ref[0].astype(BF16), pv=_dot_nt)
        _, l_w, acc_w = _online_update(carry, _dot_nt(q_all, kwn), new_keys_valid(16), vwn)
        ow_ref[...] = acc_w / l_w
        m_ref[...] = jnp.full((rows, 1), NEG, F32)
        l_ref[...] = jnp.zeros((rows, 1), F32)
        acc_ref[...] = jnp.zeros((rows, lanes), F32)

    page_rows = k_pages[0].shape[2]
    nk = n * page_rows
    bps = nk // SLC_BLOCK
    kt = jnp.concatenate([r[0] for r in k_pages], axis=1).astype(BF16)
    vt = jnp.concatenate([r[0] for r in v_pages], axis=1).astype(BF16)
    expand = jnp.where(_iota((bps, nk), 0) == _iota((bps, nk), 1) // SLC_BLOCK, 1.0, 0.0).astype(BF16)
    sel_rows = selt_ref[pl.ds(pl.multiple_of(j * bps, bps), bps), :].astype(BF16)
    bias = stack_groups(jnp.where(_dot_tn(sel_rows, expand) > 0.5, 0.0, NEG))
    m, l, acc = (m_ref[...], l_ref[...], acc_ref[...])
    s = _dot(q_all, kt) + bias
    m_new = jnp.maximum(m, jnp.max(s, axis=-1, keepdims=True))
    alpha = jnp.exp2(m - m_new)
    p = jnp.exp2(s - m_new)
    m_ref[...] = m_new
    l_ref[...] = alpha * l + jnp.sum(p, axis=-1, keepdims=True)
    acc_ref[...] = alpha * acc + _dot_nt(p.astype(BF16), vt)

    @pl.when(j == pl.num_programs(1) - 1)
    def _():
        kn = _pad_rows(skn_ref[0], 16).astype(BF16)
        vn = _pad_rows(svn_ref[0], 16).astype(BF16)
        expand_n = jnp.where(_iota((nsp, 16), 0) == (past_len + _iota((nsp, 16), 1)) // SLC_BLOCK,
                             1.0, 0.0).astype(BF16)
        sel_n = stack_groups(_dot_tn(selt_ref[...].astype(BF16), expand_n)) > 0.5
        _, l_s, acc_s = _online_update((m_ref[...], l_ref[...], acc_ref[...]), _dot_nt(q_all, kn),
                                       sel_n & new_keys_valid(16), vn)
        o_s_all = acc_s / l_s
        o_w_all = ow_ref[...]
        gates = _sigmoid(_pad_rows(gn_ref[0], tp))
        outs = []
        for g in range(NSA_GROUPS):
            gsl = slice(g * HEAD_DIM, (g + 1) * HEAD_DIM)
            o_c = oc_ref[g]
            for h in range(hpg):
                sl = slice(h * tp, (h + 1) * tp)
                asl = slice(g * grows + h * tp, g * grows + (h + 1) * tp)
                c0 = (g * hpg + h) * 3
                outs.append(gates[:, c0:c0 + 1] * o_c[sl] + gates[:, c0 + 1:c0 + 2] * o_s_all[asl, gsl]
                            + gates[:, c0 + 2:c0 + 3] * o_w_all[asl, gsl])
        o_ref[0] = jnp.concatenate(outs, axis=1)[0:tn].astype(o_ref.dtype)


def nsa_sample(slab3, kc, vc, win_kt, win_vt, pool_kt, pool_vt, page_table, past_len):
    bs, tn, _ = slab3.shape
    n_pages = page_table.shape[1]
    n = _pages_per_step(n_pages)
    page_rows = pool_kt.shape[2]
    lanes = GROUP_LANES
    ncp = kc.shape[2]
    ns = -(-(past_len + tn) // SLC_BLOCK)
    nsp = -(-ns // 8) * 8
    ovt = _overlap_t(nsp, ncp, (past_len + tn) // CMP_STRIDE - CMP_LEN // CMP_STRIDE + 1)
    rows = NSA_GROUPS * NSA_HPG * TOK_PAD
    tokblk = lambda width, col: pl.BlockSpec((1, tn, width), lambda b, j, pt: (b, 0, col // width))
    cmp_spec = pl.BlockSpec((1, NSA_GROUPS, ncp, HEAD_DIM), lambda b, j, pt: (b, 0, 0, 0))
    win_spec = pl.BlockSpec((1, lanes, win_kt.shape[2]), lambda b, j, pt: (b, 0, 0))
    page = lambda k: pl.BlockSpec((1, lanes, page_rows), lambda b, j, pt: (pt[b, n * j + k], 0, 0))
    return pl.pallas_call(
        functools.partial(_nsa_sample_body, past_len, tn, n),
        grid_spec=pltpu.PrefetchScalarGridSpec(
            num_scalar_prefetch=1,
            grid=(bs, n_pages // n),
            in_specs=[tokblk(NSA_WIDTH, COL_Q), tokblk(128, COL_GN),
                      tokblk(lanes, COL_KV + 2 * lanes), tokblk(lanes, COL_KV + 3 * lanes),
                      tokblk(lanes, COL_KV + 4 * lanes), tokblk(lanes, COL_KV + 5 * lanes),
                      cmp_spec, cmp_spec, win_spec, win_spec]
            + [page(k) for k in range(n)] * 2
            + [pl.BlockSpec(ovt.shape, lambda b, j, pt: (0, 0))],
            out_specs=pl.BlockSpec((1, tn, NSA_WIDTH), lambda b, j, pt: (b, 0, 0)),
            scratch_shapes=[pltpu.VMEM((nsp, NSA_GROUPS * TOK_PAD), F32),
                            pltpu.VMEM((nsp, NSA_GROUPS * TOK_PAD), F32),
                            pltpu.VMEM((rows, 1), F32),
                            pltpu.VMEM((rows, 1), F32),
                            pltpu.VMEM((rows, lanes), F32),
                            pltpu.VMEM((NSA_GROUPS, NSA_HPG * TOK_PAD, HEAD_DIM), F32),
                            pltpu.VMEM((rows, lanes), F32)]),
        out_shape=jax.ShapeDtypeStruct((bs, tn, NSA_WIDTH), F32),
        compiler_params=pltpu.CompilerParams(dimension_semantics=("parallel", "arbitrary"),
                                             vmem_limit_bytes=VMEM_LIMIT),
    )(page_table, slab3, slab3, slab3, slab3, slab3, slab3, kc, vc, win_kt, win_vt,
      *([pool_kt] * n), *([pool_vt] * n), ovt)


def _cumsum_rows(x):
    n = x.shape[0]
    row = _iota((n, 1), 0)
    k = 1
    while k < n:
        x = x + jnp.where(row >= k, pltpu.roll(x, k, 0), 0.0)
        k *= 2
    return x


def _rwkv_body(n_valid, chunk, p0_ref, p1_ref, p2_ref, p3_ref, p4_ref, prev_ref, s0_ref,
               mu_ref, w0_ref, w2_ref, a0_ref, a2_ref, g2_ref, kk_ref, ka_ref, rk_ref, lng_ref, lnb_ref,
               o_ref, sout_ref, carry_ref, s_ref):
    c = pl.program_id(1)
    hd = RWKV_HEAD_DIM

    @pl.when(c == 0)
    def _():
        carry_ref[...] = jnp.broadcast_to(prev_ref[0], carry_ref.shape)
        s_ref[...] = s0_ref[0]

    p = jnp.concatenate([r[0] for r in (p0_ref, p1_ref, p2_ref, p3_ref, p4_ref)], axis=1)
    p = _pad_rows(p, chunk)
    row = _iota((chunk, 1), 0)
    valid = row < n_valid
    prev = jnp.where(row == 0, carry_ref[0:1, :], pltpu.roll(p, 1, 0))
    xm = p + (prev - p) * mu_ref[...]
    carry_ref[...] = jnp.broadcast_to(p[n_valid - 1:n_valid, :], carry_ref.shape)

    wdt = RWKV_WIDTH
    r_all, k_all, v_all = xm[:, 0:wdt], xm[:, wdt:2 * wdt], xm[:, 2 * wdt:3 * wdt]
    o = 3 * wdt
    wd, ad, gd = xm[:, o:o + DECAY_LORA], xm[:, o + DECAY_LORA:o + DECAY_LORA + ICL_LORA], \
        xm[:, o + DECAY_LORA + ICL_LORA:o + DECAY_LORA + ICL_LORA + GATE_LORA]
    w = w0_ref[...] + _dot(jnp.tanh(wd).astype(BF16), w2_ref[...])
    logw = -jnp.exp(-_softplus(-w) - 0.5)
    a_all = _sigmoid(a0_ref[...] + _dot(ad.astype(BF16), a2_ref[...]))
    g_all = _dot(_sigmoid(gd).astype(BF16), g2_ref[...])
    logw = jnp.where(valid, logw, 0.0)
    cum = _cumsum_rows(logw)
    total = cum[chunk - 1:chunk, :]
    w_in = jnp.exp(cum)
    w_ex = jnp.exp(cum - logw)
    w_inv = jnp.exp(-cum)
    w_rem = jnp.exp(total - cum)
    w_tot = jnp.exp(total)
    kk_all = k_all * kk_ref[...]
    k2_all = k_all * (1.0 + (a_all - 1.0) * ka_ref[...])

    t_i = _iota((chunk, chunk), 0)
    s_i = _iota((chunk, chunk), 1)
    strict = s_i < t_i
    incl = s_i <= t_i
    n_rounds = int(np.log2(chunk))
    heads = range(RWKV_HEADS)
    sls = [slice(h * hd, (h + 1) * hd) for h in heads]

    lr, bt, kt, bk, vb, at, rt = [], [], [], [], [], [], []
    for sl in sls:
        kkh = kk_all[:, sl]
        nrm = jnp.sqrt(jnp.sum(kkh * kkh, axis=-1, keepdims=True))
        kkh = jnp.where(valid, kkh / jnp.maximum(nrm, 1e-12), 0.0)
        k2h = jnp.where(valid, k2_all[:, sl], 0.0)
        vh = jnp.where(valid, v_all[:, sl], 0.0)
        bh = kkh * a_all[:, sl]
        a_t = -kkh * w_ex[:, sl]
        r_t = r_all[:, sl] * w_in[:, sl]
        at.append(a_t.astype(BF16))
        rt.append(r_t.astype(BF16))
        lr.append(jnp.concatenate([a_t, r_t], axis=0).astype(BF16))
        bt.append((bh * w_inv[:, sl]).astype(BF16))
        kt.append((k2h * w_inv[:, sl]).astype(BF16))
        bk.append(jnp.concatenate([bh * w_rem[:, sl], k2h * w_rem[:, sl]], axis=0).astype(BF16))
        vb.append(vh)
    m_b = [_dot_nt(lr[h], bt[h]) for h in heads]
    m_k = [_dot_nt(lr[h], kt[h]) for h in heads]
    a_ab = [jnp.where(strict, m[0:chunk], 0.0) for m in m_b]
    a_rb = [jnp.where(incl, m[chunk:2 * chunk], 0.0).astype(BF16) for m in m_b]
    a_ak = [jnp.where(strict, m[0:chunk], 0.0).astype(BF16) for m in m_k]
    a_rk = [jnp.where(incl, m[chunk:2 * chunk], 0.0).astype(BF16) for m in m_k]
    s0 = [s_ref[h] for h in heads]
    s0b = [x.astype(BF16) for x in s0]
    vbb = [x.astype(BF16) for x in vb]
    u = [_dot_nt(at[h], s0b[h]) + _dot(a_ak[h], vbb[h]) for h in heads]
    pw = a_ab
    for it in range(n_rounds):
        pwb = [x.astype(BF16) for x in pw]
        u = [u[h] + _dot(pwb[h], u[h].astype(BF16)) for h in heads]
        if it + 1 < n_rounds:
            pw = [_dot(x, x) for x in pwb]
    ub = [x.astype(BF16) for x in u]
    y = [_dot_nt(rt[h], s0b[h]) + _dot(a_rb[h], ub[h]) + _dot(a_rk[h], vbb[h]) for h in heads]
    for h in heads:
        uv = jnp.concatenate([u[h], vb[h]], axis=0).astype(BF16)
        s_ref[h] = s0[h] * w_tot[:, sls[h]] + _dot_tn(uv, bk[h])
    outs = []
    for h in heads:
        sl = sls[h]
        mean = jnp.mean(y[h], axis=-1, keepdims=True)
        yc = y[h] - mean
        var = jnp.mean(yc * yc, axis=-1, keepdims=True)
        yn = yc * lax.rsqrt(var + GN_EPS)
        bonus = jnp.sum(r_all[:, sl] * k2_all[:, sl] * rk_ref[:, sl], axis=-1, keepdims=True) * v_all[:, sl]
        outs.append((yn * lng_ref[:, sl] + lnb_ref[:, sl] + bonus) * g_all[:, sl])
    out = jnp.concatenate(outs, axis=1)
    o_ref[0] = out[0:o_ref.shape[1]].astype(o_ref.dtype)

    @pl.when(c == pl.num_programs(1) - 1)
    def _():
        sout_ref[0] = s_ref[...]


def rwkv(slab3, p_prev, s0, params, chunk, out_dtype):
    b, t, _ = slab3.shape
    tc = min(t, chunk)
    nchunks = t // tc
    blk = 512
    pspec = lambda k: pl.BlockSpec((1, tc, blk), lambda bi, c: (bi, c, COL_PR // blk + k))
    full = lambda a: pl.BlockSpec(a.shape, lambda bi, c: (0,) * a.ndim)
    sspec = pl.BlockSpec((1, RWKV_HEADS, RWKV_HEAD_DIM, RWKV_HEAD_DIM), lambda bi, c: (bi, 0, 0, 0))
    return pl.pallas_call(
        functools.partial(_rwkv_body, tc, chunk),
        grid=(b, nchunks),
        in_specs=[pspec(k) for k in range(5)]
        + [pl.BlockSpec((1, 1, RWKV_PROJ), lambda bi, c: (bi, 0, 0)), sspec]
        + [full(a) for a in params],
        out_specs=[pl.BlockSpec((1, tc, RWKV_WIDTH), lambda bi, c: (bi, c, 0)), sspec],
        out_shape=[jax.ShapeDtypeStruct((b, t, RWKV_WIDTH), out_dtype),
                   jax.ShapeDtypeStruct(s0.shape, F32)],
        scratch_shapes=[pltpu.VMEM((8, RWKV_PROJ), F32),
                        pltpu.VMEM((RWKV_HEADS, RWKV_HEAD_DIM, RWKV_HEAD_DIM), F32)],
        compiler_params=pltpu.CompilerParams(dimension_semantics=("parallel", "arbitrary"),
                                             vmem_limit_bytes=VMEM_LIMIT),
    )(slab3, slab3, slab3, slab3, slab3, p_prev, s0, *params)


def _mem_attend_body(q_ref, k_ref, v_ref, o_ref):
    tm = q_ref.shape[1]
    q = _pad_rows(q_ref[0], max(tm, 16)).astype(BF16)
    k = k_ref[0].astype(BF16)
    v = v_ref[0].astype(BF16)
    outs = []
    for h in range(MEM_HEADS):
        sl = slice(h * MEM_HEAD_DIM, (h + 1) * MEM_HEAD_DIM)
        s = _dot_nt(q[:, sl], k[:, sl]) * (MEM_HEAD_DIM ** -0.5)
        m = jnp.max(s, axis=-1, keepdims=True)
        p = jnp.exp(s - m)
        p = p / jnp.sum(p, axis=-1, keepdims=True)
        outs.append(_dot(p.astype(BF16), v[:, sl]))
    o_ref[0] = jnp.concatenate(outs, axis=1)[0:tm].astype(o_ref.dtype)


def mem_attend(slab3, mk, k_blk, mv, v_blk, tm, out_dtype):
    b, t, _ = slab3.shape
    m = mk.shape[1]
    return pl.pallas_call(
        _mem_attend_body,
        grid=(b, t // tm),
        in_specs=[pl.BlockSpec((1, tm, MEM_WIDTH), lambda bi, i: (bi, i, COL_MQ // MEM_WIDTH)),
                  pl.BlockSpec((1, m, MEM_WIDTH), lambda bi, i: (bi, 0, k_blk)),
                  pl.BlockSpec((1, m, MEM_WIDTH), lambda bi, i: (bi, 0, v_blk))],
        out_specs=pl.BlockSpec((1, tm, MEM_WIDTH), lambda bi, i: (bi, i, 0)),
        out_shape=jax.ShapeDtypeStruct((b, t, MEM_WIDTH), out_dtype),
        compiler_params=pltpu.CompilerParams(dimension_semantics=("parallel", "parallel"),
                                             vmem_limit_bytes=VMEM_LIMIT),
    )(slab3, mk, mv)


def _merge_body(x_ref, on_ref, or_ref, om_ref, g0_ref, g1_ref, g2_ref, wn_ref, wr_ref, wm_ref, wo_ref, o_ref):
    m = _sigmoid(g0_ref[...]) * _dot(on_ref[...].astype(BF16), wn_ref[...])
    m = m + _sigmoid(g1_ref[...]) * _dot(or_ref[...].astype(BF16), wr_ref[...])
    m = m + _sigmoid(g2_ref[...]) * _dot(om_ref[...].astype(BF16), wm_ref[...])
    o_ref[...] = x_ref[...] + _dot(m.astype(BF16), wo_ref[...])


def merge(x, o_nsa, o_rwkv, o_mem, slab, wn, wr, wm, wo, tm):
    n, d = x.shape
    row = lambda w: pl.BlockSpec((tm, w), lambda i: (i, 0))
    full = lambda a: pl.BlockSpec(a.shape, lambda i: (0, 0))
    gate = lambda k: pl.BlockSpec((tm, d), lambda i: (i, COL_MG // d + k))
    return pl.pallas_call(
        _merge_body,
        grid=(n // tm,),
        in_specs=[row(d), row(NSA_WIDTH), row(RWKV_WIDTH), row(MEM_WIDTH), gate(0), gate(1), gate(2),
                  full(wn), full(wr), full(wm), full(wo)],
        out_specs=row(d),
        out_shape=jax.ShapeDtypeStruct((n, d), F32),
        compiler_params=pltpu.CompilerParams(dimension_semantics=("parallel",), vmem_limit_bytes=VMEM_LIMIT),
    )(x, o_nsa, o_rwkv, o_mem, slab, slab, slab, wn, wr, wm, wo)


def _ffn_body(x_ref, gf_ref, wg_ref, wu_ref, wd_ref, gl_ref, o_ref):
    x = x_ref[...]
    hf = _rms(x, gf_ref[...]).astype(BF16)
    gate = _dot(hf, wg_ref[...])
    up = _dot(hf, wu_ref[...])
    act = (gate * _sigmoid(gate) * up).astype(BF16)
    x2 = x + _dot(act, wd_ref[...])
    o_ref[...] = _rms(x2, gl_ref[...])


def ffn(x, gf, wg, wu, wd, gl, tm):
    n, d = x.shape
    row = pl.BlockSpec((tm, d), lambda i: (i, 0))
    full = lambda a: pl.BlockSpec(a.shape, lambda i: (0, 0), pipeline_mode=pl.Buffered(1))
    return pl.pallas_call(
        _ffn_body,
        grid=(n // tm,),
        in_specs=[row, full(gf), full(wg), full(wu), full(wd), full(gl)],
        out_specs=row,
        out_shape=jax.ShapeDtypeStruct((n, d), F32),
        compiler_params=pltpu.CompilerParams(dimension_semantics=("parallel",), vmem_limit_bytes=VMEM_LIMIT),
    )(x, gf, wg, wu, wd, gl)


def _slab_weight(w_in):
    wq, wkv, wgn, wpr, wmq, wmg = jnp.split(w_in, np.cumsum(
        [NSA_WIDTH, NSA_KV_COLS, 3 * NSA_HEADS, RWKV_PROJ, MEM_WIDTH])[:5].tolist(), axis=1)
    d = w_in.shape[0]
    pad = jnp.zeros((d, SLAB_COLS - COL_GN - 3 * NSA_HEADS), w_in.dtype)
    w = jnp.concatenate([wq * (HEAD_DIM ** -0.5 * LOG2E), wkv, wpr, wmg, wmq, wgn, pad], axis=1)
    return w.astype(BF16), wkv.astype(BF16), wkv.T.astype(BF16)


def _channel_major_rows(x):
    b, _, t = x.shape
    return jnp.transpose(x.reshape(b, NSA_GROUPS, HEAD_DIM, t), (0, 3, 1, 2))[None]


def _channel_major_view(x):
    b, t = x.shape[:2]
    return jnp.transpose(x, (0, 2, 3, 1)).reshape(b, GROUP_LANES, t)


def kernel(x_prompt, x_sample, cache_cmp_k, cache_cmp_v, cache_slc_k, cache_slc_v, cache_win_k, cache_win_v, state_rwkv_shift, state_rwkv_wkv, cache_mem_k, cache_mem_v, page_table, mem_prompt, attn_norm, w_in, cmp_pe_k, cmp_w1_k, cmp_b1_k, cmp_w2_k, cmp_pe_v, cmp_w1_v, cmp_b1_v, cmp_w2_v, rwkv_mu, rwkv_w0, rwkv_w2, rwkv_a0, rwkv_a2, rwkv_g2, rwkv_kk, rwkv_ka, rwkv_rk, rwkv_ln_g, rwkv_ln_b, mem_norm, w_mem_kv, w_o_nsa, w_o_rwkv, w_o_mem, w_out, ffn_norm, w_gate, w_up, w_down, final_norm):
    assert w_in.shape[0] == 1, "one layer"
    bp, t, d = x_prompt.shape
    bs, tn, _ = x_sample.shape
    row2 = lambda a: a.reshape(1, -1)
    gl_ = GROUP_LANES

    w_slab, w_kv, w_kvt = _slab_weight(w_in[0])
    cmp_wk = _compress_weights(cmp_pe_k[0], cmp_w1_k[0], cmp_b1_k[0], cmp_w2_k[0])
    cmp_wv = _compress_weights(cmp_pe_v[0], cmp_w1_v[0], cmp_b1_v[0], cmp_w2_v[0])
    rw_params = (row2(rwkv_mu[0]), row2(rwkv_w0[0]), rwkv_w2[0].astype(BF16), row2(rwkv_a0[0]),
                 rwkv_a2[0].astype(BF16), rwkv_g2[0].astype(BF16), row2(rwkv_kk[0]), row2(rwkv_ka[0]),
                 row2(rwkv_rk[0]), row2(rwkv_ln_g[0]), row2(rwkv_ln_b[0]))
    wn, wr, wm, wo = (a[0].astype(BF16) for a in (w_o_nsa, w_o_rwkv, w_o_mem, w_out))
    wg, wu, wd = (a[0].astype(BF16) for a in (w_gate, w_up, w_down))
    gf, gl = row2(ffn_norm[0]), row2(final_norm)

    xp2 = x_prompt.reshape(bp * t, d)
    slab, kvt, ktb, vb, ck, cv = proj_prompt(xp2, row2(attn_norm[0]), w_slab, w_kv, w_kvt, bp, t, 1024, 512)
    slab3 = slab.reshape(bp, t, SLAB_COLS)
    nch = t // CMP_STRIDE
    kc, vc = compress_prompt(ck.reshape(bp, nch, CMP_STRIDE * gl_), cv.reshape(bp, nch, CMP_STRIDE * gl_),
                             cmp_wk, cmp_wv)
    o_nsa = nsa_prompt(slab, kc, vc, ktb, vb.reshape(bp, t, 2 * gl_), bp, t)
    o_rwkv, s_p = rwkv(slab3, jnp.zeros((bp, 1, RWKV_PROJ), F32),
                       jnp.zeros((bp, RWKV_HEADS, RWKV_HEAD_DIM, RWKV_HEAD_DIM), F32), rw_params, 64, BF16)
    mem_n = mem_prompt.shape[1]
    mkv = norm_matmul(mem_prompt.reshape(bp * mem_n, d), row2(mem_norm[0]), w_mem_kv[0].astype(BF16),
                      min(1024, bp * mem_n), 512).reshape(bp, mem_n, 2 * MEM_WIDTH)
    o_mem = mem_attend(slab3, mkv, 0, mkv, 1, 512, BF16)
    x1 = merge(xp2, o_nsa, o_rwkv.reshape(bp * t, RWKV_WIDTH), o_mem.reshape(bp * t, MEM_WIDTH), slab,
               wn, wr, wm, wo, 512)
    y_prompt = ffn(x1, gf, wg, wu, wd, gl, 256).reshape(bp, t, d)

    wp0 = max(t - WINDOW, 0)
    stream = lambda i: kvt[:, i * gl_:(i + 1) * gl_, :]
    p_state = (_channel_major_rows(stream(0)), _channel_major_rows(stream(1)),
               _channel_major_rows(stream(2)), _channel_major_rows(stream(3)),
               _channel_major_rows(stream(4)[:, :, wp0:]), _channel_major_rows(stream(5)[:, :, wp0:]),
               slab3[:, t - 1, COL_PR:COL_PR + RWKV_PROJ][None],
               s_p[None],
               mkv[:, :, :MEM_WIDTH].reshape(1, bp, mem_n, MEM_HEADS, MEM_HEAD_DIM),
               mkv[:, :, MEM_WIDTH:].reshape(1, bp, mem_n, MEM_HEADS, MEM_HEAD_DIM))

    past_len = page_table.shape[1] * cache_cmp_k.shape[2]
    assert past_len % SLC_BLOCK == 0
    assert (past_len + tn) // CMP_STRIDE == past_len // CMP_STRIDE and tn <= TOK_PAD
    xs2 = x_sample.reshape(bs * tn, d)
    slab_s = norm_matmul(xs2, row2(attn_norm[0]), w_slab, bs * tn, 512)
    slab_s3 = slab_s.reshape(bs, tn, SLAB_COLS)
    kc_s, vc_s = compress_sample(_channel_major_view(cache_cmp_k[0]), _channel_major_view(cache_cmp_v[0]),
                                 page_table, cmp_wk, cmp_wv)
    o_nsa_s = nsa_sample(slab_s3, kc_s, vc_s, _channel_major_view(cache_win_k[0]), _channel_major_view(cache_win_v[0]),
                         _channel_major_view(cache_slc_k[0]), _channel_major_view(cache_slc_v[0]),
                         page_table, past_len)
    o_rwkv_s, s_s = rwkv(slab_s3, state_rwkv_shift[0][:, None, :], state_rwkv_wkv[0], rw_params, 16, F32)
    mem_s = cache_mem_k.shape[2]
    o_mem_s = mem_attend(slab_s3, cache_mem_k[0].reshape(bs, mem_s, MEM_WIDTH), 0,
                         cache_mem_v[0].reshape(bs, mem_s, MEM_WIDTH), 0, tn, F32)
    x1s = merge(xs2, o_nsa_s.reshape(bs * tn, NSA_WIDTH), o_rwkv_s.reshape(bs * tn, RWKV_WIDTH),
                o_mem_s.reshape(bs * tn, MEM_WIDTH), slab_s, wn, wr, wm, wo, min(512, bs * tn))
    y_sample = ffn(x1s, gf, wg, wu, wd, gl, min(256, bs * tn)).reshape(bs, tn, d)
    heads = lambda a: a.reshape(1, a.shape[0], a.shape[1], NSA_GROUPS, HEAD_DIM)
    kv_new = [slab_s3[:, :, COL_KV + i * gl_:COL_KV + (i + 1) * gl_] for i in range(6)]
    s_state = (heads(kv_new[0]), heads(kv_new[1]), heads(kv_new[2]), heads(kv_new[3]),
               jnp.concatenate([cache_win_k[0], heads(kv_new[4])[0]], axis=1)[:, tn:][None],
               jnp.concatenate([cache_win_v[0], heads(kv_new[5])[0]], axis=1)[:, tn:][None],
               slab_s3[:, tn - 1, COL_PR:COL_PR + RWKV_PROJ][None],
               s_s[None])
    return (y_prompt, y_sample) + p_state + s_state
```

```python
import functools

import numpy as np
import jax
import jax.numpy as jnp
from jax import lax
from jax.experimental import pallas as pl
from jax.experimental.pallas import tpu as pltpu

F32 = jnp.float32
BF16 = jnp.bfloat16

D_MODEL = 1024
HEAD_DIM = 64
NSA_WIDTH = 768
NSA_HEADS = 12
NSA_GROUPS = 2
NSA_HPG = 6
CMP_LEN = 32
CMP_STRIDE = 16
CMP_HID = 64
SLC_BLOCK = 64
N_SELECT = 16
WINDOW = 512
Q_BLOCK = 64
FORCE_SCORE = 1e4
RWKV_WIDTH = 768
RWKV_HEAD_DIM = 64
RWKV_HEADS = 12
DECAY_LORA = 64
ICL_LORA = 64
GATE_LORA = 128
RWKV_PROJ = 3 * RWKV_WIDTH + DECAY_LORA + ICL_LORA + GATE_LORA
GN_EPS = 64e-5
MEM_HEADS = 4
MEM_WIDTH = 512
MEM_HEAD_DIM = 128
N_BRANCHES = 3
NSA_KV_COLS = 3 * 2 * NSA_GROUPS * HEAD_DIM
GROUP_LANES = NSA_GROUPS * HEAD_DIM
RMS_EPS = 1e-6
NEG = -1e30
LOG2E = 1.4426950408889634

COL_MG = 0
COL_PR = 3072
COL_MQ = 5632
COL_Q = 6144
COL_GN = 6912
COL_KV = 7168
SLAB_COLS = 8192
MAX_PAGES_PER_STEP = 16


def _pages_per_step(n_pages):
    n = min(MAX_PAGES_PER_STEP, n_pages)
    assert n_pages % n == 0
    return n

VMEM_LIMIT = 56 * 1024 * 1024


def _dot(a, b):
    return jnp.dot(a, b, preferred_element_type=F32)


def _dot_nt(a, b):
    return lax.dot_general(a, b, (((1,), (1,)), ((), ())), preferred_element_type=F32)


def _dot_tn(a, b):
    return lax.dot_general(a, b, (((0,), (0,)), ((), ())), preferred_element_type=F32)


def _iota(shape, dim):
    return lax.broadcasted_iota(jnp.int32, shape, dim)


def _eye(n, dtype):
    return (_iota((n, n), 0) == _iota((n, n), 1)).astype(dtype)


def _sigmoid(x):
    return 1.0 / (1.0 + jnp.exp(-x))


def _softplus(z):
    return jnp.maximum(z, 0.0) + jnp.log(1.0 + jnp.exp(-jnp.abs(z)))


def _gelu_tanh(x):
    return 0.5 * x * (1.0 + jnp.tanh(np.sqrt(2.0 / np.pi).astype(np.float32) * (x + 0.044715 * (x * x * x))))


def _rms(x, g):
    ms = jnp.mean(x * x, axis=-1, keepdims=True)
    return (x * lax.rsqrt(ms + RMS_EPS)) * g


def _pad_rows(x, n):
    if x.shape[0] == n:
        return x
    return jnp.concatenate([x, jnp.zeros((n - x.shape[0],) + x.shape[1:], x.dtype)], axis=0)


def _norm_matmul_body(x_ref, g_ref, w_ref, o_ref, h_ref):
    @pl.when(pl.program_id(1) == 0)
    def _():
        h_ref[...] = _rms(x_ref[...], g_ref[...]).astype(BF16)

    o_ref[...] = _dot(h_ref[...], w_ref[...])


def norm_matmul(x, g, w, tm, tn):
    n, d = x.shape
    c = w.shape[1]
    return pl.pallas_call(
        _norm_matmul_body,
        grid=(n // tm, c // tn),
        in_specs=[pl.BlockSpec((tm, d), lambda i, j: (i, 0)),
                  pl.BlockSpec((1, d), lambda i, j: (0, 0)),
                  pl.BlockSpec((d, tn), lambda i, j: (0, j))],
        out_specs=pl.BlockSpec((tm, tn), lambda i, j: (i, j)),
        out_shape=jax.ShapeDtypeStruct((n, c), F32),
        scratch_shapes=[pltpu.VMEM((tm, d), BF16)],
        compiler_params=pltpu.CompilerParams(dimension_semantics=("parallel", "arbitrary"),
                                             vmem_limit_bytes=VMEM_LIMIT),
    )(x, g, w)


def _proj_prompt_body(x_ref, g_ref, w_ref, wkv_ref, wkvt_ref, o_ref, kvt_ref, ktb_ref, vb_ref, ck_ref, cv_ref, h_ref):
    @pl.when(pl.program_id(1) == 0)
    def _():
        h = _rms(x_ref[...], g_ref[...]).astype(BF16)
        h_ref[...] = h
        gl = GROUP_LANES
        kvt = _dot_nt(wkvt_ref[...], h)
        kvt_ref[0] = kvt
        ktb_ref[0] = jnp.concatenate([kvt[2 * gl:3 * gl], kvt[4 * gl:5 * gl]], axis=0).astype(BF16)
        kv = _dot(h, wkv_ref[...])
        ck_ref[...] = kv[:, 0:gl].astype(BF16)
        cv_ref[...] = kv[:, gl:2 * gl].astype(BF16)
        vb_ref[...] = jnp.concatenate([kv[:, 3 * gl:4 * gl], kv[:, 5 * gl:6 * gl]], axis=1).astype(BF16)

    o_ref[...] = _dot(h_ref[...], w_ref[...])


def proj_prompt(x, g, w, wkv, wkvt, b, t, tm, tn):
    n, d = x.shape
    c = w.shape[1]
    tpb = t // tm
    gl = GROUP_LANES
    full = lambda a: pl.BlockSpec(a.shape, lambda i, j: (0, 0))
    rows = lambda width: pl.BlockSpec((tm, width), lambda i, j: (i, 0))
    return pl.pallas_call(
        _proj_prompt_body,
        grid=(n // tm, c // tn),
        in_specs=[pl.BlockSpec((tm, d), lambda i, j: (i, 0)), full(g),
                  pl.BlockSpec((d, tn), lambda i, j: (0, j)), full(wkv), full(wkvt)],
        out_specs=[pl.BlockSpec((tm, tn), lambda i, j: (i, j)),
                   pl.BlockSpec((1, NSA_KV_COLS, tm), lambda i, j: (i // tpb, 0, i % tpb)),
                   pl.BlockSpec((1, 2 * gl, tm), lambda i, j: (i // tpb, 0, i % tpb)),
                   rows(2 * gl), rows(gl), rows(gl)],
        out_shape=[jax.ShapeDtypeStruct((n, c), F32),
                   jax.ShapeDtypeStruct((b, NSA_KV_COLS, t), F32),
                   jax.ShapeDtypeStruct((b, 2 * gl, t), BF16),
                   jax.ShapeDtypeStruct((n, 2 * gl), BF16),
                   jax.ShapeDtypeStruct((n, gl), BF16),
                   jax.ShapeDtypeStruct((n, gl), BF16)],
        scratch_shapes=[pltpu.VMEM((tm, d), BF16)],
        compiler_params=pltpu.CompilerParams(dimension_semantics=("parallel", "arbitrary"),
                                             vmem_limit_bytes=VMEM_LIMIT),
    )(x, g, w, wkv, wkvt)


def _compress_consts(pe_ref, b1_ref, w0_ref, w1_ref):
    pe0 = jnp.broadcast_to(pe_ref[0], (8, pe_ref.shape[2])).astype(BF16)
    pe1 = jnp.broadcast_to(pe_ref[1], (8, pe_ref.shape[2])).astype(BF16)
    c = _dot(pe0, w0_ref[...]) + _dot(pe1, w1_ref[...])
    return c[0:1] + b1_ref[...]


def _compress_finish(u0, u1, cst, w2):
    n = u0.shape[0]
    pre = u0 + pltpu.roll(u1, n - 1, 0) + cst
    out = _dot(_gelu_tanh(pre).astype(BF16), w2)
    return jnp.where(_iota(out.shape, 0) < n - 1, out, 0.0)


def _compress_prompt_body(xk_ref, xv_ref, pek_ref, b1k_ref, w0k_ref, w1k_ref, w2k_ref,
                          pev_ref, b1v_ref, w0v_ref, w1v_ref, w2v_ref, ok_ref, ov_ref):
    for x_ref, pe_ref, b1_ref, w0_ref, w1_ref, w2_ref, o_ref in (
            (xk_ref, pek_ref, b1k_ref, w0k_ref, w1k_ref, w2k_ref, ok_ref),
            (xv_ref, pev_ref, b1v_ref, w0v_ref, w1v_ref, w2v_ref, ov_ref)):
        x = x_ref[0].astype(BF16)
        cst = _compress_consts(pe_ref, b1_ref, w0_ref, w1_ref)
        res = _compress_finish(_dot(x, w0_ref[...]), _dot(x, w1_ref[...]), cst, w2_ref[...])
        for g in range(NSA_GROUPS):
            o_ref[0, g] = res[:, g * HEAD_DIM:(g + 1) * HEAD_DIM].astype(o_ref.dtype)


def _compress_weights(pe, w1, b1, w2):
    r = CMP_LEN // CMP_STRIDE
    eye = jnp.eye(NSA_GROUPS, dtype=F32)
    w1r = w1.reshape(r, CMP_STRIDE, HEAD_DIM, CMP_HID)
    w1e = jnp.einsum('icdh,gk->icgdkh', w1r, eye).reshape(r, CMP_STRIDE * NSA_GROUPS * HEAD_DIM,
                                                         NSA_GROUPS * CMP_HID)
    pee = jnp.broadcast_to(pe.reshape(r, CMP_STRIDE, 1, HEAD_DIM), (r, CMP_STRIDE, NSA_GROUPS, HEAD_DIM))
    pee = pee.reshape(r, 1, CMP_STRIDE * NSA_GROUPS * HEAD_DIM)
    b1e = jnp.tile(b1, NSA_GROUPS).reshape(1, NSA_GROUPS * CMP_HID)
    w2e = jnp.einsum('hd,gk->ghkd', w2, eye).reshape(NSA_GROUPS * CMP_HID, NSA_GROUPS * HEAD_DIM)
    return pee, b1e, w1e[0].astype(BF16), w1e[1].astype(BF16), w2e.astype(BF16)


def compress_prompt(xk, xv, wk, wv):
    b, nch, width = xk.shape
    full = lambda a: pl.BlockSpec(a.shape, lambda i: (0,) * a.ndim)
    xspec = pl.BlockSpec((1, nch, width), lambda i: (i, 0, 0))
    ospec = pl.BlockSpec((1, NSA_GROUPS, nch, HEAD_DIM), lambda i: (i, 0, 0, 0))
    oshape = jax.ShapeDtypeStruct((b, NSA_GROUPS, nch, HEAD_DIM), BF16)
    return pl.pallas_call(
        _compress_prompt_body,
        grid=(b,),
        in_specs=[xspec, xspec] + [full(a) for a in wk] + [full(a) for a in wv],
        out_specs=[ospec, ospec],
        out_shape=[oshape, oshape],
        compiler_params=pltpu.CompilerParams(dimension_semantics=("parallel",), vmem_limit_bytes=VMEM_LIMIT),
    )(xk, xv, *wk, *wv)


def _split3(x):
    hi = x.astype(BF16)
    r1 = x - hi.astype(F32)
    mid = r1.astype(BF16)
    lo = (r1 - mid.astype(F32)).astype(BF16)
    return hi, mid, lo


def _select_blocks_lanes(score, n_blocks):
    nq, nbp = score.shape
    cols = [score[:, jp:jp + 1] for jp in range(n_blocks)]
    cnts = []
    for c0 in range(0, nbp, 128):
        blk = score[:, c0:c0 + 128]
        jj = c0 + _iota(blk.shape, 1)
        cnt = jnp.zeros(blk.shape, F32)
        for jp in range(n_blocks):
            ge = jnp.where(cols[jp] >= blk, 1.0, 0.0)
            gt = jnp.where(cols[jp] > blk, 1.0, 0.0)
            if jp < c0:
                cnt = cnt + ge
            elif jp >= c0 + 128:
                cnt = cnt + gt
            else:
                cnt = cnt + jnp.where(jj > jp, ge, gt)
        cnts.append(cnt)
    cnt = jnp.concatenate(cnts, axis=1)
    return jnp.where((cnt < N_SELECT) & (score > -jnp.inf), 1.0, 0.0)


def _select_blocks_unrolled(score):
    nb = score.shape[0]
    rows = [score[jp:jp + 1, :] for jp in range(nb)]
    cnts = []
    for r in range(nb // 8):
        blk = score[8 * r:8 * r + 8]
        jj = 8 * r + _iota(blk.shape, 0)
        cnt = jnp.zeros(blk.shape, F32)
        for jp in range(nb):
            ge = jnp.where(rows[jp] >= blk, 1.0, 0.0)
            gt = jnp.where(rows[jp] > blk, 1.0, 0.0)
            if jp < 8 * r:
                cnt = cnt + ge
            elif jp >= 8 * r + 8:
                cnt = cnt + gt
            else:
                cnt = cnt + jnp.where(jj > jp, ge, gt)
        cnts.append(cnt)
    cnt = jnp.concatenate(cnts, axis=0)
    return jnp.where((cnt < N_SELECT) & (score > -jnp.inf), 1.0, 0.0)


def _masked_softmax_rows(s, valid):
    s = jnp.where(valid, s, NEG)
    m = jnp.max(s, axis=-1, keepdims=True)
    p = jnp.where(valid, jnp.exp2(s - m), 0.0)
    l = jnp.sum(p, axis=-1, keepdims=True)
    return p / jnp.where(l > 0.0, l, 1.0)


def _online_update(carry, s, valid, v, pv=_dot):
    m, l, acc = carry
    s = jnp.where(valid, s, NEG)
    m_new = jnp.maximum(m, jnp.max(s, axis=-1, keepdims=True))
    alpha = jnp.exp2(m - m_new)
    p = jnp.where(valid, jnp.exp2(s - m_new), 0.0)
    l = alpha * l + jnp.sum(p, axis=-1, keepdims=True)
    acc = alpha * acc + pv(p.astype(BF16), v)
    return m_new, l, acc


def _online_update_biased(carry, s, v):
    m, l, acc = carry
    m_new = jnp.maximum(m, jnp.max(s, axis=-1, keepdims=True))
    alpha = jnp.exp2(m - m_new)
    p = jnp.exp2(s - m_new)
    l = alpha * l + jnp.sum(p, axis=-1, keepdims=True)
    acc = alpha * acc + _dot(p.astype(BF16), v)
    return m_new, l, acc


KV_TILE = 512
WIN_TILE = 640


def _nsa_prompt_body(q_ref, gn_ref, kc_ref, vc_ref, kts_ref, vs_ref, ktw_ref, vw_ref, ovt_ref, o_ref):
    g = pl.program_id(1)
    qb = pl.program_id(2)
    nq = Q_BLOCK
    hpg = NSA_HPG
    rows = hpg * nq
    qf = q_ref[...]
    q2f = jnp.concatenate([qf[:, h * HEAD_DIM:(h + 1) * HEAD_DIM] for h in range(hpg)], axis=0)
    q2 = q2f.astype(BF16)
    zero = jnp.zeros_like(q2f)
    q2w = jnp.where(g == 0, jnp.concatenate([q2f, zero], axis=1), jnp.concatenate([zero, q2f], axis=1)).astype(BF16)
    t_q1 = qb * nq + _iota((nq, 1), 0)
    tile6 = lambda x: jnp.concatenate([x] * hpg, axis=0)
    pick = lambda x: jnp.where(g == 0, x[:, 0:HEAD_DIM], x[:, HEAD_DIM:2 * HEAD_DIM])

    kc = kc_ref[0, 0]
    ncp = kc.shape[0]
    c_end = _iota((nq, ncp), 1) * CMP_STRIDE + (CMP_LEN - 1)
    s_c = _dot_nt(q2, kc) + tile6(jnp.where(c_end <= t_q1, 0.0, NEG))
    e_c = jnp.exp2(s_c - jnp.max(s_c, axis=-1, keepdims=True))
    l_c = jnp.sum(e_c, axis=-1, keepdims=True)
    any_c = tile6(t_q1 >= CMP_LEN - 1)
    p_c = e_c * jnp.where(any_c, 1.0 / l_c, 0.0)
    o_c = _dot(p_c.astype(BF16), vc_ref[0, 0])
    psum = p_c[0:nq]
    for h in range(1, hpg):
        psum = psum + p_c[h * nq:(h + 1) * nq]

    imp_t = sum(_dot_nt(ovt_ref[...], part) for part in _split3(psum))
    j_idx = _iota(imp_t.shape, 0)
    forced = (j_idx == 0) | (j_idx == qb) | (j_idx == qb - 1)
    score = jnp.where(forced, FORCE_SCORE, imp_t)
    sel_t = _select_blocks_unrolled(jnp.where(j_idx <= qb, score, -jnp.inf))
    sel = _dot_nt(_eye(nq, BF16), sel_t.astype(BF16)).astype(BF16)

    w0 = pl.multiple_of(jnp.maximum(qb * nq - WINDOW, 0) // 128 * 128, 128)
    diff = t_q1 - (w0 + _iota((nq, WIN_TILE), 1))
    ok_w = (diff >= 0) & (diff <= WINDOW)
    s_w = _dot(q2w, ktw_ref[0, :, pl.ds(w0, WIN_TILE)]) + tile6(jnp.where(ok_w, 0.0, NEG))
    e_w = jnp.exp2(s_w - jnp.max(s_w, axis=-1, keepdims=True))
    l_w = jnp.sum(e_w, axis=-1, keepdims=True)
    o_w = pick(_dot(e_w.astype(BF16), vw_ref[0, pl.ds(w0, WIN_TILE), :])) / l_w

    bpt = KV_TILE // SLC_BLOCK
    col_blk = _iota((sel.shape[1], KV_TILE), 1) // SLC_BLOCK
    row_blk = _iota((sel.shape[1], KV_TILE), 0)

    def block_mask(kt):
        expand = jnp.where(row_blk == col_blk + kt * bpt, 1.0, 0.0).astype(BF16)
        return _dot(sel, expand) > 0.5

    def scores(kt):
        off = pl.multiple_of(kt * KV_TILE, KV_TILE)
        return _dot(q2w, kts_ref[0, :, pl.ds(off, KV_TILE)])

    def values(kt):
        off = pl.multiple_of(kt * KV_TILE, KV_TILE)
        return vs_ref[0, pl.ds(off, KV_TILE), :]

    nt = qb // bpt
    ok_d = block_mask(nt) & (_iota((nq, KV_TILE), 1) + nt * KV_TILE <= t_q1)
    init = (jnp.full((rows, 1), NEG, F32), jnp.zeros((rows, 1), F32), jnp.zeros((rows, GROUP_LANES), F32))
    carry = _online_update_biased(init, scores(nt) + tile6(jnp.where(ok_d, 0.0, NEG)), values(nt))

    def pair_step(i, carry):
        k0 = 2 * i
        k1 = 2 * i + 1
        s0 = scores(k0)
        s1 = scores(k1)
        b0 = jnp.where(block_mask(k0), 0.0, NEG)
        b1 = jnp.where(block_mask(k1) & (k1 < nt), 0.0, NEG)
        carry = _online_update_biased(carry, s0 + tile6(b0), values(k0))
        return _online_update_biased(carry, s1 + tile6(b1), values(k1))

    _, l_s, acc_s = lax.fori_loop(0, (nt + 1) // 2, pair_step, carry)
    o_s = pick(acc_s) / l_s

    gates = _sigmoid(gn_ref[...])
    per_group = hpg * 3
    gates = jnp.where(g == 0, gates[:, 0:per_group], gates[:, per_group:2 * per_group])
    outs = []
    for h in range(hpg):
        sl = slice(h * nq, (h + 1) * nq)
        outs.append(gates[:, 3 * h:3 * h + 1] * o_c[sl] + gates[:, 3 * h + 1:3 * h + 2] * o_s[sl]
                    + gates[:, 3 * h + 2:3 * h + 3] * o_w[sl])
    o_ref[...] = jnp.concatenate(outs, axis=1).astype(o_ref.dtype)


def _overlap_t(n_blocks, n_cmp_padded, n_cmp):
    i = np.arange(n_cmp_padded)[None, :] * CMP_STRIDE
    j = np.arange(n_blocks)[:, None] * SLC_BLOCK
    ov = (i < j + SLC_BLOCK) & (i + CMP_LEN > j) & (np.arange(n_cmp_padded)[None, :] < n_cmp)
    return jnp.asarray(ov.astype(np.float32)).astype(BF16)


def nsa_prompt(slab, kc, vc, ktb, vb, b, t):
    nb = t // Q_BLOCK
    gw = NSA_HPG * HEAD_DIM
    gl = GROUP_LANES
    ncp = kc.shape[2]
    ovt = _overlap_t(nb, ncp, ncp - 1)
    kt_spec = lambda k: pl.BlockSpec((1, gl, t), lambda bi, g, qb: (bi, k, 0))
    v_spec = lambda k: pl.BlockSpec((1, t, gl), lambda bi, g, qb: (bi, 0, k))
    cmp_spec = pl.BlockSpec((1, 1, ncp, HEAD_DIM), lambda bi, g, qb: (bi, g, 0, 0))
    return pl.pallas_call(
        _nsa_prompt_body,
        grid=(b, NSA_GROUPS, nb),
        in_specs=[pl.BlockSpec((Q_BLOCK, gw), lambda bi, g, qb: (bi * nb + qb, COL_Q // gw + g)),
                  pl.BlockSpec((Q_BLOCK, 128), lambda bi, g, qb: (bi * nb + qb, COL_GN // 128)),
                  cmp_spec, cmp_spec, kt_spec(0), v_spec(0), kt_spec(1), v_spec(1),
                  pl.BlockSpec(ovt.shape, lambda bi, g, qb: (0, 0))],
        out_specs=pl.BlockSpec((Q_BLOCK, gw), lambda bi, g, qb: (bi * nb + qb, g)),
        out_shape=jax.ShapeDtypeStruct((b * t, NSA_WIDTH), BF16),
        compiler_params=pltpu.CompilerParams(dimension_semantics=("parallel", "parallel", "arbitrary"),
                                             vmem_limit_bytes=VMEM_LIMIT),
    )(slab, slab, kc, vc, ktb, vb, ktb, vb, ovt)


def _compress_sample_body(n, pt_ref, *refs):
    k_pages, v_pages = refs[0:n], refs[n:2 * n]
    (pek_ref, b1k_ref, w0k_ref, w1k_ref, w2k_ref, wck_ref,
     pev_ref, b1v_ref, w0v_ref, w1v_ref, w2v_ref, wcv_ref, ok_ref, ov_ref, uk_ref, uv_ref) = refs[2 * n:]
    j = pl.program_id(1)
    page_rows = k_pages[0].shape[2]
    cpp = page_rows // CMP_STRIDE
    rows = n * cpp
    off = pl.multiple_of(j * rows, rows)
    rp = _iota((page_rows, page_rows), 0)
    perm = (_iota((page_rows, page_rows), 1) == CMP_STRIDE * (rp % cpp) + rp // cpp).astype(BF16)
    for pages, wc_ref, u_ref in ((k_pages, wck_ref, uk_ref), (v_pages, wcv_ref, uv_ref)):
        xp = [_dot_nt(perm, r[0].astype(BF16)) for r in pages]
        x = jnp.concatenate([jnp.concatenate([p[c * cpp:(c + 1) * cpp] for p in xp], axis=0)
                             for c in range(CMP_STRIDE)], axis=1)
        u_ref[pl.ds(off, rows), :] = _dot(x.astype(BF16), wc_ref[...])

    @pl.when(j == pl.num_programs(1) - 1)
    def _():
        half = NSA_GROUPS * CMP_HID
        for pe_ref, b1_ref, w0_ref, w1_ref, w2_ref, u_ref, o_ref in (
                (pek_ref, b1k_ref, w0k_ref, w1k_ref, w2k_ref, uk_ref, ok_ref),
                (pev_ref, b1v_ref, w0v_ref, w1v_ref, w2v_ref, uv_ref, ov_ref)):
            cst = _compress_consts(pe_ref, b1_ref, w0_ref, w1_ref)
            u = u_ref[...]
            res = _compress_finish(u[:, 0:half], u[:, half:2 * half], cst, w2_ref[...])
            for g in range(NSA_GROUPS):
                o_ref[0, g] = res[:, g * HEAD_DIM:(g + 1) * HEAD_DIM].astype(o_ref.dtype)


def compress_sample(pool_k, pool_v, page_table, wk, wv):
    bs, n_pages = page_table.shape
    _, lanes, page_rows = pool_k.shape
    n = _pages_per_step(n_pages)
    nch = n_pages * page_rows // CMP_STRIDE
    wck = jnp.concatenate([wk[2], wk[3]], axis=1)
    wcv = jnp.concatenate([wv[2], wv[3]], axis=1)
    page = lambda k: pl.BlockSpec((1, lanes, page_rows), lambda b, j, pt: (pt[b, n * j + k], 0, 0))
    full = lambda a: pl.BlockSpec(a.shape, lambda b, j, pt: (0,) * a.ndim)
    consts = list(wk) + [wck] + list(wv) + [wcv]
    ospec = pl.BlockSpec((1, NSA_GROUPS, nch, HEAD_DIM), lambda b, j, pt: (b, 0, 0, 0))
    oshape = jax.ShapeDtypeStruct((bs, NSA_GROUPS, nch, HEAD_DIM), BF16)
    return pl.pallas_call(
        functools.partial(_compress_sample_body, n),
        grid_spec=pltpu.PrefetchScalarGridSpec(
            num_scalar_prefetch=1,
            grid=(bs, n_pages // n),
            in_specs=[page(k) for k in range(n)] * 2 + [full(a) for a in consts],
            out_specs=[ospec, ospec],
            scratch_shapes=[pltpu.VMEM((nch, 2 * NSA_GROUPS * CMP_HID), F32)] * 2),
        out_shape=[oshape, oshape],
        compiler_params=pltpu.CompilerParams(dimension_semantics=("parallel", "arbitrary"),
                                             vmem_limit_bytes=VMEM_LIMIT),
    )(page_table, *([pool_k] * n), *([pool_v] * n), *consts)


TOK_PAD = 8


def _nsa_sample_body(past_len, tn, n, pt_ref, *refs):
    q_ref, gn_ref, skn_ref, svn_ref, wkn_ref, wvn_ref, kc_ref, vc_ref, wkc_ref, wvc_ref = refs[0:10]
    k_pages, v_pages = refs[10:10 + n], refs[10 + n:10 + 2 * n]
    ov_ref, o_ref, selt_ref, m_ref, l_ref, acc_ref, oc_ref, ow_ref = refs[10 + 2 * n:]
    j = pl.program_id(1)
    tp = TOK_PAD
    hpg = NSA_HPG
    grows = hpg * tp
    rows = NSA_GROUPS * grows
    gw = hpg * HEAD_DIM
    lanes = GROUP_LANES
    nsp = selt_ref.shape[0]

    q8 = _pad_rows(q_ref[0], tp)
    zero = jnp.zeros((grows, HEAD_DIM), F32)
    q2, q_parts = [], []
    for g in range(NSA_GROUPS):
        qg = jnp.concatenate([q8[:, g * gw + h * HEAD_DIM:g * gw + (h + 1) * HEAD_DIM] for h in range(hpg)], axis=0)
        q2.append(qg.astype(BF16))
        q_parts.append(jnp.concatenate([qg, zero] if g == 0 else [zero, qg], axis=1))
    q_all = jnp.concatenate(q_parts, axis=0).astype(BF16)
    tok = _iota((rows, 1), 0) % tp
    t_q = past_len + tok

    def stack_groups(x):
        return jnp.concatenate([x[g * tp:(g + 1) * tp] for g in range(NSA_GROUPS) for _ in range(hpg)], axis=0)

    def new_keys_valid(width):
        tk = _iota((rows, width), 1)
        return (tk <= tok) & (tk < tn)

    @pl.when(j == 0)
    def _():
        psums = []
        tq_g = t_q[0:grows]
        for g in range(NSA_GROUPS):
            s_c = _dot_nt(q2[g], kc_ref[0, g])
            c_end = _iota(s_c.shape, 1) * CMP_STRIDE + (CMP_LEN - 1)
            p_c = _masked_softmax_rows(s_c, c_end <= tq_g)
            oc_ref[g] = _dot(p_c.astype(BF16), vc_ref[0, g])
            ps = p_c[0:tp]
            for h in range(1, hpg):
                ps = ps + p_c[h * tp:(h + 1) * tp]
            psums.append(ps)
        psum = jnp.concatenate(psums, axis=0)
        imp = sum(_dot(part, ov_ref[...]) for part in _split3(psum))
        j_idx = _iota(imp.shape, 1)
        cur = (past_len + _iota(imp.shape, 0) % tp) // SLC_BLOCK
        forced = (j_idx == 0) | (j_idx == cur) | (j_idx == cur - 1)
        score = jnp.where(j_idx <= cur, jnp.where(forced, FORCE_SCORE, imp), -jnp.inf)
        sel = _select_blocks_lanes(score, (past_len + tn - 1) // SLC_BLOCK + 1)
        pick_row = (_iota((nsp, sel.shape[1]), 0) == _iota((nsp, sel.shape[1]), 1)).astype(BF16)
        selt_ref[...] = _dot_nt(pick_row, sel.astype(BF16))

        lw = wkc_ref.shape[2]
        kwn = _pad_rows(wkn_ref[0], 16).astype(BF16)
        vwn = _pad_rows(wvn_ref[0], 16).astype(BF16)
        diff = t_q - (past_len - lw + _iota((rows, lw), 1))
        carry = (jnp.full((rows, 1), NEG, F32), jnp.zeros((rows, 1), F32), jnp.zeros((rows, lanes), F32))
        carry = _online_update(carry, _dot(q_all, wkc_ref[0].astype(BF16)), (diff >= 0) & (diff <= WINDOW),
                               wvc_ref[0].astype(BF16), pv=_dot_nt)
        _, l_w, acc_w = _online_update(carry, _dot_nt(q_all, kwn), new_keys_valid(16), vwn)
        ow_ref[...] = acc_w / l_w
        m_ref[...] = jnp.full((rows, 1), NEG, F32)
        l_ref[...] = jnp.zeros((rows, 1), F32)
        acc_ref[...] = jnp.zeros((rows, lanes), F32)

    page_rows = k_pages[0].shape[2]
    nk = n * page_rows
    bps = nk // SLC_BLOCK
    kt = jnp.concatenate([r[0] for r in k_pages], axis=1).astype(BF16)
    vt = jnp.concatenate([r[0] for r in v_pages], axis=1).astype(BF16)
    expand = jnp.where(_iota((bps, nk), 0) == _iota((bps, nk), 1) // SLC_BLOCK, 1.0, 0.0).astype(BF16)
    sel_rows = selt_ref[pl.ds(pl.multiple_of(j * bps, bps), bps), :].astype(BF16)
    bias = stack_groups(jnp.where(_dot_tn(sel_rows, expand) > 0.5, 0.0, NEG))
    m, l, acc = (m_ref[...], l_ref[...], acc_ref[...])
    s = _dot(q_all, kt) + bias
    m_new = jnp.maximum(m, jnp.max(s, axis=-1, keepdims=True))
    alpha = jnp.exp2(m - m_new)
    p = jnp.exp2(s - m_new)
    m_ref[...] = m_new
    l_ref[...] = alpha * l + jnp.sum(p, axis=-1, keepdims=True)
    acc_ref[...] = alpha * acc + _dot_nt(p.astype(BF16), vt)

    @pl.when(j == pl.num_programs(1) - 1)
    def _():
        kn = _pad_rows(skn_ref[0], 16).astype(BF16)
        vn = _pad_rows(svn_ref[0], 16).astype(BF16)
        expand_n = jnp.where(_iota((nsp, 16), 0) == (past_len + _iota((nsp, 16), 1)) // SLC_BLOCK,
                             1.0, 0.0).astype(BF16)
        sel_n = stack_groups(_dot_tn(selt_ref[...].astype(BF16), expand_n)) > 0.5
        _, l_s, acc_s = _online_update((m_ref[...], l_ref[...], acc_ref[...]), _dot_nt(q_all, kn),
                                       sel_n & new_keys_valid(16), vn)
        o_s_all = acc_s / l_s
        o_w_all = ow_ref[...]
        gates = _sigmoid(_pad_rows(gn_ref[0], tp))
        outs = []
        for g in range(NSA_GROUPS):
            gsl = slice(g * HEAD_DIM, (g + 1) * HEAD_DIM)
            o_c = oc_ref[g]
            for h in range(hpg):
                sl = slice(h * tp, (h + 1) * tp)
                asl = slice(g * grows + h * tp, g * grows + (h + 1) * tp)
                c0 = (g * hpg + h) * 3
                outs.append(gates[:, c0:c0 + 1] * o_c[sl] + gates[:, c0 + 1:c0 + 2] * o_s_all[asl, gsl]
                            + gates[:, c0 + 2:c0 + 3] * o_w_all[asl, gsl])
        o_ref[0] = jnp.concatenate(outs, axis=1)[0:tn].astype(o_ref.dtype)


def nsa_sample(slab3, kc, vc, win_kt, win_vt, pool_kt, pool_vt, page_table, past_len):
    bs, tn, _ = slab3.shape
    n_pages = page_table.shape[1]
    n = _pages_per_step(n_pages)
    page_rows = pool_kt.shape[2]
    lanes = GROUP_LANES
    ncp = kc.shape[2]
    ns = -(-(past_len + tn) // SLC_BLOCK)
    nsp = -(-ns // 8) * 8
    ov = _overlap_t(-(-ns // 128) * 128, ncp, (past_len + tn) // CMP_STRIDE - CMP_LEN // CMP_STRIDE + 1).T
    rows = NSA_GROUPS * NSA_HPG * TOK_PAD
    tokblk = lambda width, col: pl.BlockSpec((1, tn, width), lambda b, j, pt: (b, 0, col // width))
    cmp_spec = pl.BlockSpec((1, NSA_GROUPS, ncp, HEAD_DIM), lambda b, j, pt: (b, 0, 0, 0))
    win_spec = pl.BlockSpec((1, lanes, win_kt.shape[2]), lambda b, j, pt: (b, 0, 0))
    page = lambda k: pl.BlockSpec((1, lanes, page_rows), lambda b, j, pt: (pt[b, n * j + k], 0, 0))
    return pl.pallas_call(
        functools.partial(_nsa_sample_body, past_len, tn, n),
        grid_spec=pltpu.PrefetchScalarGridSpec(
            num_scalar_prefetch=1,
            grid=(bs, n_pages // n),
            in_specs=[tokblk(NSA_WIDTH, COL_Q), tokblk(128, COL_GN),
                      tokblk(lanes, COL_KV + 2 * lanes), tokblk(lanes, COL_KV + 3 * lanes),
                      tokblk(lanes, COL_KV + 4 * lanes), tokblk(lanes, COL_KV + 5 * lanes),
                      cmp_spec, cmp_spec, win_spec, win_spec]
            + [page(k) for k in range(n)] * 2
            + [pl.BlockSpec(ov.shape, lambda b, j, pt: (0, 0))],
            out_specs=pl.BlockSpec((1, tn, NSA_WIDTH), lambda b, j, pt: (b, 0, 0)),
            scratch_shapes=[pltpu.VMEM((nsp, NSA_GROUPS * TOK_PAD), F32),
                            pltpu.VMEM((rows, 1), F32),
                            pltpu.VMEM((rows, 1), F32),
                            pltpu.VMEM((rows, lanes), F32),
                            pltpu.VMEM((NSA_GROUPS, NSA_HPG * TOK_PAD, HEAD_DIM), F32),
                            pltpu.VMEM((rows, lanes), F32)]),
        out_shape=jax.ShapeDtypeStruct((bs, tn, NSA_WIDTH), F32),
        compiler_params=pltpu.CompilerParams(dimension_semantics=("parallel", "arbitrary"),
                                             vmem_limit_bytes=VMEM_LIMIT),
    )(page_table, slab3, slab3, slab3, slab3, slab3, slab3, kc, vc, win_kt, win_vt,
      *([pool_kt] * n), *([pool_vt] * n), ov)


def _cumsum_rows(x):
    n = x.shape[0]
    row = _iota((n, 1), 0)
    k = 1
    while k < n:
        x = x + jnp.where(row >= k, pltpu.roll(x, k, 0), 0.0)
        k *= 2
    return x


def _rwkv_body(n_valid, chunk, p0_ref, p1_ref, p2_ref, p3_ref, p4_ref, prev_ref, s0_ref,
               mu_ref, w0_ref, w2_ref, a0_ref, a2_ref, g2_ref, kk_ref, ka_ref, rk_ref, lng_ref, lnb_ref,
               o_ref, sout_ref, carry_ref, s_ref):
    c = pl.program_id(1)
    hd = RWKV_HEAD_DIM

    @pl.when(c == 0)
    def _():
        carry_ref[...] = jnp.broadcast_to(prev_ref[0], carry_ref.shape)
        s_ref[...] = s0_ref[0]

    p = jnp.concatenate([r[0] for r in (p0_ref, p1_ref, p2_ref, p3_ref, p4_ref)], axis=1)
    p = _pad_rows(p, chunk)
    row = _iota((chunk, 1), 0)
    valid = row < n_valid
    prev = jnp.where(row == 0, carry_ref[0:1, :], pltpu.roll(p, 1, 0))
    xm = p + (prev - p) * mu_ref[...]
    carry_ref[...] = jnp.broadcast_to(p[n_valid - 1:n_valid, :], carry_ref.shape)

    wdt = RWKV_WIDTH
    r_all, k_all, v_all = xm[:, 0:wdt], xm[:, wdt:2 * wdt], xm[:, 2 * wdt:3 * wdt]
    o = 3 * wdt
    wd, ad, gd = xm[:, o:o + DECAY_LORA], xm[:, o + DECAY_LORA:o + DECAY_LORA + ICL_LORA], \
        xm[:, o + DECAY_LORA + ICL_LORA:o + DECAY_LORA + ICL_LORA + GATE_LORA]
    w = w0_ref[...] + _dot(jnp.tanh(wd).astype(BF16), w2_ref[...])
    logw = -jnp.exp(-_softplus(-w) - 0.5)
    a_all = _sigmoid(a0_ref[...] + _dot(ad.astype(BF16), a2_ref[...]))
    g_all = _dot(_sigmoid(gd).astype(BF16), g2_ref[...])
    logw = jnp.where(valid, logw, 0.0)
    cum = _cumsum_rows(logw)
    total = cum[chunk - 1:chunk, :]
    w_in = jnp.exp(cum)
    w_ex = jnp.exp(cum - logw)
    w_inv = jnp.exp(-cum)
    w_rem = jnp.exp(total - cum)
    w_tot = jnp.exp(total)
    kk_all = k_all * kk_ref[...]
    k2_all = k_all * (1.0 + (a_all - 1.0) * ka_ref[...])

    t_i = _iota((chunk, chunk), 0)
    s_i = _iota((chunk, chunk), 1)
    strict = s_i < t_i
    incl = s_i <= t_i
    n_rounds = int(np.log2(chunk))
    heads = range(RWKV_HEADS)
    sls = [slice(h * hd, (h + 1) * hd) for h in heads]

    lr, bt, kt, bk, vb, at, rt = [], [], [], [], [], [], []
    for sl in sls:
        kkh = kk_all[:, sl]
        nrm = jnp.sqrt(jnp.sum(kkh * kkh, axis=-1, keepdims=True))
        kkh = jnp.where(valid, kkh / jnp.maximum(nrm, 1e-12), 0.0)
        k2h = jnp.where(valid, k2_all[:, sl], 0.0)
        vh = jnp.where(valid, v_all[:, sl], 0.0)
        bh = kkh * a_all[:, sl]
        a_t = -kkh * w_ex[:, sl]
        r_t = r_all[:, sl] * w_in[:, sl]
        at.append(a_t.astype(BF16))
        rt.append(r_t.astype(BF16))
        lr.append(jnp.concatenate([a_t, r_t], axis=0).astype(BF16))
        bt.append((bh * w_inv[:, sl]).astype(BF16))
        kt.append((k2h * w_inv[:, sl]).astype(BF16))
        bk.append(jnp.concatenate([bh * w_rem[:, sl], k2h * w_rem[:, sl]], axis=0).astype(BF16))
        vb.append(vh)
    m_b = [_dot_nt(lr[h], bt[h]) for h in heads]
    m_k = [_dot_nt(lr[h], kt[h]) for h in heads]
    a_ab = [jnp.where(strict, m[0:chunk], 0.0) for m in m_b]
    a_rb = [jnp.where(incl, m[chunk:2 * chunk], 0.0).astype(BF16) for m in m_b]
    a_ak = [jnp.where(strict, m[0:chunk], 0.0).astype(BF16) for m in m_k]
    a_rk = [jnp.where(incl, m[chunk:2 * chunk], 0.0).astype(BF16) for m in m_k]
    s0 = [s_ref[h] for h in heads]
    s0b = [x.astype(BF16) for x in s0]
    vbb = [x.astype(BF16) for x in vb]
    u = [_dot_nt(at[h], s0b[h]) + _dot(a_ak[h], vbb[h]) for h in heads]
    pw = a_ab
    for it in range(n_rounds):
        pwb = [x.astype(BF16) for x in pw]
        u = [u[h] + _dot(pwb[h], u[h].astype(BF16)) for h in heads]
        if it + 1 < n_rounds:
            pw = [_dot(x, x) for x in pwb]
    ub = [x.astype(BF16) for x in u]
    y = [_dot_nt(rt[h], s0b[h]) + _dot(a_rb[h], ub[h]) + _dot(a_rk[h], vbb[h]) for h in heads]
    for h in heads:
        uv = jnp.concatenate([u[h], vb[h]], axis=0).astype(BF16)
        s_ref[h] = s0[h] * w_tot[:, sls[h]] + _dot_tn(uv, bk[h])
    outs = []
    for h in heads:
        sl = sls[h]
        mean = jnp.mean(y[h], axis=-1, keepdims=True)
        yc = y[h] - mean
        var = jnp.mean(yc * yc, axis=-1, keepdims=True)
        yn = yc * lax.rsqrt(var + GN_EPS)
        bonus = jnp.sum(r_all[:, sl] * k2_all[:, sl] * rk_ref[:, sl], axis=-1, keepdims=True) * v_all[:, sl]
        outs.append((yn * lng_ref[:, sl] + lnb_ref[:, sl] + bonus) * g_all[:, sl])
    out = jnp.concatenate(outs, axis=1)
    o_ref[0] = out[0:o_ref.shape[1]].astype(o_ref.dtype)

    @pl.when(c == pl.num_programs(1) - 1)
    def _():
        sout_ref[0] = s_ref[...]


def rwkv(slab3, p_prev, s0, params, chunk, out_dtype):
    b, t, _ = slab3.shape
    tc = min(t, chunk)
    nchunks = t // tc
    blk = 512
    pspec = lambda k: pl.BlockSpec((1, tc, blk), lambda bi, c: (bi, c, COL_PR // blk + k))
    full = lambda a: pl.BlockSpec(a.shape, lambda bi, c: (0,) * a.ndim)
    sspec = pl.BlockSpec((1, RWKV_HEADS, RWKV_HEAD_DIM, RWKV_HEAD_DIM), lambda bi, c: (bi, 0, 0, 0))
    return pl.pallas_call(
        functools.partial(_rwkv_body, tc, chunk),
        grid=(b, nchunks),
        in_specs=[pspec(k) for k in range(5)]
        + [pl.BlockSpec((1, 1, RWKV_PROJ), lambda bi, c: (bi, 0, 0)), sspec]
        + [full(a) for a in params],
        out_specs=[pl.BlockSpec((1, tc, RWKV_WIDTH), lambda bi, c: (bi, c, 0)), sspec],
        out_shape=[jax.ShapeDtypeStruct((b, t, RWKV_WIDTH), out_dtype),
                   jax.ShapeDtypeStruct(s0.shape, F32)],
        scratch_shapes=[pltpu.VMEM((8, RWKV_PROJ), F32),
                        pltpu.VMEM((RWKV_HEADS, RWKV_HEAD_DIM, RWKV_HEAD_DIM), F32)],
        compiler_params=pltpu.CompilerParams(dimension_semantics=("parallel", "arbitrary"),
                                             vmem_limit_bytes=VMEM_LIMIT),
    )(slab3, slab3, slab3, slab3, slab3, p_prev, s0, *params)


def _mem_attend_body(q_ref, k_ref, v_ref, o_ref):
    tm = q_ref.shape[1]
    q = _pad_rows(q_ref[0], max(tm, 16)).astype(BF16)
    k = k_ref[0].astype(BF16)
    v = v_ref[0].astype(BF16)
    outs = []
    for h in range(MEM_HEADS):
        sl = slice(h * MEM_HEAD_DIM, (h + 1) * MEM_HEAD_DIM)
        s = _dot_nt(q[:, sl], k[:, sl]) * (MEM_HEAD_DIM ** -0.5)
        m = jnp.max(s, axis=-1, keepdims=True)
        p = jnp.exp(s - m)
        p = p / jnp.sum(p, axis=-1, keepdims=True)
        outs.append(_dot(p.astype(BF16), v[:, sl]))
    o_ref[0] = jnp.concatenate(outs, axis=1)[0:tm].astype(o_ref.dtype)


def mem_attend(slab3, mk, k_blk, mv, v_blk, tm, out_dtype):
    b, t, _ = slab3.shape
    m = mk.shape[1]
    return pl.pallas_call(
        _mem_attend_body,
        grid=(b, t // tm),
        in_specs=[pl.BlockSpec((1, tm, MEM_WIDTH), lambda bi, i: (bi, i, COL_MQ // MEM_WIDTH)),
                  pl.BlockSpec((1, m, MEM_WIDTH), lambda bi, i: (bi, 0, k_blk)),
                  pl.BlockSpec((1, m, MEM_WIDTH), lambda bi, i: (bi, 0, v_blk))],
        out_specs=pl.BlockSpec((1, tm, MEM_WIDTH), lambda bi, i: (bi, i, 0)),
        out_shape=jax.ShapeDtypeStruct((b, t, MEM_WIDTH), out_dtype),
        compiler_params=pltpu.CompilerParams(dimension_semantics=("parallel", "parallel"),
                                             vmem_limit_bytes=VMEM_LIMIT),
    )(slab3, mk, mv)


def _merge_body(x_ref, on_ref, or_ref, om_ref, g0_ref, g1_ref, g2_ref, wn_ref, wr_ref, wm_ref, wo_ref, o_ref):
    m = _sigmoid(g0_ref[...]) * _dot(on_ref[...].astype(BF16), wn_ref[...])
    m = m + _sigmoid(g1_ref[...]) * _dot(or_ref[...].astype(BF16), wr_ref[...])
    m = m + _sigmoid(g2_ref[...]) * _dot(om_ref[...].astype(BF16), wm_ref[...])
    o_ref[...] = x_ref[...] + _dot(m.astype(BF16), wo_ref[...])


def merge(x, o_nsa, o_rwkv, o_mem, slab, wn, wr, wm, wo, tm):
    n, d = x.shape
    row = lambda w: pl.BlockSpec((tm, w), lambda i: (i, 0))
    full = lambda a: pl.BlockSpec(a.shape, lambda i: (0, 0))
    gate = lambda k: pl.BlockSpec((tm, d), lambda i: (i, COL_MG // d + k))
    return pl.pallas_call(
        _merge_body,
        grid=(n // tm,),
        in_specs=[row(d), row(NSA_WIDTH), row(RWKV_WIDTH), row(MEM_WIDTH), gate(0), gate(1), gate(2),
                  full(wn), full(wr), full(wm), full(wo)],
        out_specs=row(d),
        out_shape=jax.ShapeDtypeStruct((n, d), F32),
        compiler_params=pltpu.CompilerParams(dimension_semantics=("parallel",), vmem_limit_bytes=VMEM_LIMIT),
    )(x, o_nsa, o_rwkv, o_mem, slab, slab, slab, wn, wr, wm, wo)


def _ffn_body(x_ref, gf_ref, wg_ref, wu_ref, wd_ref, gl_ref, o_ref):
    x = x_ref[...]
    hf = _rms(x, gf_ref[...]).astype(BF16)
    gate = _dot(hf, wg_ref[...])
    up = _dot(hf, wu_ref[...])
    act = (gate * _sigmoid(gate) * up).astype(BF16)
    x2 = x + _dot(act, wd_ref[...])
    o_ref[...] = _rms(x2, gl_ref[...])


def ffn(x, gf, wg, wu, wd, gl, tm):
    n, d = x.shape
    row = pl.BlockSpec((tm, d), lambda i: (i, 0))
    full = lambda a: pl.BlockSpec(a.shape, lambda i: (0, 0), pipeline_mode=pl.Buffered(1))
    return pl.pallas_call(
        _ffn_body,
        grid=(n // tm,),
        in_specs=[row, full(gf), full(wg), full(wu), full(wd), full(gl)],
        out_specs=row,
        out_shape=jax.ShapeDtypeStruct((n, d), F32),
        compiler_params=pltpu.CompilerParams(dimension_semantics=("parallel",), vmem_limit_bytes=VMEM_LIMIT),
    )(x, gf, wg, wu, wd, gl)


def _slab_weight(w_in):
    wq, wkv, wgn, wpr, wmq, wmg = jnp.split(w_in, np.cumsum(
        [NSA_WIDTH, NSA_KV_COLS, 3 * NSA_HEADS, RWKV_PROJ, MEM_WIDTH])[:5].tolist(), axis=1)
    d = w_in.shape[0]
    zeros = lambda n: jnp.zeros((d, n), w_in.dtype)
    w = jnp.concatenate([wmg, wpr, wmq, wq * (HEAD_DIM ** -0.5 * LOG2E), wgn, zeros(COL_KV - COL_GN - 3 * NSA_HEADS),
                         wkv, zeros(SLAB_COLS - COL_KV - NSA_KV_COLS)], axis=1)
    return w.astype(BF16), wkv.astype(BF16), wkv.T.astype(BF16)


def _channel_major_rows(x):
    b, _, t = x.shape
    return jnp.transpose(x.reshape(b, NSA_GROUPS, HEAD_DIM, t), (0, 3, 1, 2))[None]


def _channel_major_view(x):
    b, t = x.shape[:2]
    return jnp.transpose(x, (0, 2, 3, 1)).reshape(b, GROUP_LANES, t)


def kernel(x_prompt, x_sample, cache_cmp_k, cache_cmp_v, cache_slc_k, cache_slc_v, cache_win_k, cache_win_v, state_rwkv_shift, state_rwkv_wkv, cache_mem_k, cache_mem_v, page_table, mem_prompt, attn_norm, w_in, cmp_pe_k, cmp_w1_k, cmp_b1_k, cmp_w2_k, cmp_pe_v, cmp_w1_v, cmp_b1_v, cmp_w2_v, rwkv_mu, rwkv_w0, rwkv_w2, rwkv_a0, rwkv_a2, rwkv_g2, rwkv_kk, rwkv_ka, rwkv_rk, rwkv_ln_g, rwkv_ln_b, mem_norm, w_mem_kv, w_o_nsa, w_o_rwkv, w_o_mem, w_out, ffn_norm, w_gate, w_up, w_down, final_norm):
    assert w_in.shape[0] == 1, "one layer"
    bp, t, d = x_prompt.shape
    bs, tn, _ = x_sample.shape
    row2 = lambda a: a.reshape(1, -1)
    gl_ = GROUP_LANES

    w_slab, w_kv, w_kvt = _slab_weight(w_in[0])
    cmp_wk = _compress_weights(cmp_pe_k[0], cmp_w1_k[0], cmp_b1_k[0], cmp_w2_k[0])
    cmp_wv = _compress_weights(cmp_pe_v[0], cmp_w1_v[0], cmp_b1_v[0], cmp_w2_v[0])
    rw_params = (row2(rwkv_mu[0]), row2(rwkv_w0[0]), rwkv_w2[0].astype(BF16), row2(rwkv_a0[0]),
                 rwkv_a2[0].astype(BF16), rwkv_g2[0].astype(BF16), row2(rwkv_kk[0]), row2(rwkv_ka[0]),
                 row2(rwkv_rk[0]), row2(rwkv_ln_g[0]), row2(rwkv_ln_b[0]))
    wn, wr, wm, wo = (a[0].astype(BF16) for a in (w_o_nsa, w_o_rwkv, w_o_mem, w_out))
    wg, wu, wd = (a[0].astype(BF16) for a in (w_gate, w_up, w_down))
    gf, gl = row2(ffn_norm[0]), row2(final_norm)

    xp2 = x_prompt.reshape(bp * t, d)
    slab, kvt, ktb, vb, ck, cv = proj_prompt(xp2, row2(attn_norm[0]), w_slab[:, :COL_KV], w_kv, w_kvt, bp, t,
                                             1024, 512)
    slab3 = slab.reshape(bp, t, COL_KV)
    nch = t // CMP_STRIDE
    kc, vc = compress_prompt(ck.reshape(bp, nch, CMP_STRIDE * gl_), cv.reshape(bp, nch, CMP_STRIDE * gl_),
                             cmp_wk, cmp_wv)
    o_nsa = nsa_prompt(slab, kc, vc, ktb, vb.reshape(bp, t, 2 * gl_), bp, t)
    o_rwkv, s_p = rwkv(slab3, jnp.zeros((bp, 1, RWKV_PROJ), F32),
                       jnp.zeros((bp, RWKV_HEADS, RWKV_HEAD_DIM, RWKV_HEAD_DIM), F32), rw_params, 64, BF16)
    mem_n = mem_prompt.shape[1]
    mkv = norm_matmul(mem_prompt.reshape(bp * mem_n, d), row2(mem_norm[0]), w_mem_kv[0].astype(BF16),
                      min(1024, bp * mem_n), 512).reshape(bp, mem_n, 2 * MEM_WIDTH)
    o_mem = mem_attend(slab3, mkv, 0, mkv, 1, 512, BF16)
    x1 = merge(xp2, o_nsa, o_rwkv.reshape(bp * t, RWKV_WIDTH), o_mem.reshape(bp * t, MEM_WIDTH), slab,
               wn, wr, wm, wo, 512)
    y_prompt = ffn(x1, gf, wg, wu, wd, gl, 256).reshape(bp, t, d)

    wp0 = max(t - WINDOW, 0)
    stream = lambda i: kvt[:, i * gl_:(i + 1) * gl_, :]
    p_state = (_channel_major_rows(stream(0)), _channel_major_rows(stream(1)),
               _channel_major_rows(stream(2)), _channel_major_rows(stream(3)),
               _channel_major_rows(stream(4)[:, :, wp0:]), _channel_major_rows(stream(5)[:, :, wp0:]),
               slab3[:, t - 1, COL_PR:COL_PR + RWKV_PROJ][None],
               s_p[None],
               mkv[:, :, :MEM_WIDTH].reshape(1, bp, mem_n, MEM_HEADS, MEM_HEAD_DIM),
               mkv[:, :, MEM_WIDTH:].reshape(1, bp, mem_n, MEM_HEADS, MEM_HEAD_DIM))

    past_len = page_table.shape[1] * cache_cmp_k.shape[2]
    assert past_len % SLC_BLOCK == 0
    assert (past_len + tn) // CMP_STRIDE == past_len // CMP_STRIDE and tn <= TOK_PAD
    xs2 = x_sample.reshape(bs * tn, d)
    slab_s = norm_matmul(xs2, row2(attn_norm[0]), w_slab, bs * tn, 512)
    slab_s3 = slab_s.reshape(bs, tn, SLAB_COLS)
    kc_s, vc_s = compress_sample(_channel_major_view(cache_cmp_k[0]), _channel_major_view(cache_cmp_v[0]),
                                 page_table, cmp_wk, cmp_wv)
    o_nsa_s = nsa_sample(slab_s3, kc_s, vc_s, _channel_major_view(cache_win_k[0]), _channel_major_view(cache_win_v[0]),
                         _channel_major_view(cache_slc_k[0]), _channel_major_view(cache_slc_v[0]),
                         page_table, past_len)
    o_rwkv_s, s_s = rwkv(slab_s3, state_rwkv_shift[0][:, None, :], state_rwkv_wkv[0], rw_params, 16, F32)
    mem_s = cache_mem_k.shape[2]
    o_mem_s = mem_attend(slab_s3, cache_mem_k[0].reshape(bs, mem_s, MEM_WIDTH), 0,
                         cache_mem_v[0].reshape(bs, mem_s, MEM_WIDTH), 0, tn, F32)
    x1s = merge(xs2, o_nsa_s.reshape(bs * tn, NSA_WIDTH), o_rwkv_s.reshape(bs * tn, RWKV_WIDTH),
                o_mem_s.reshape(bs * tn, MEM_WIDTH), slab_s, wn, wr, wm, wo, min(512, bs * tn))
    y_sample = ffn(x1s, gf, wg, wu, wd, gl, min(256, bs * tn)).reshape(bs, tn, d)
    heads = lambda a: a.reshape(1, a.shape[0], a.shape[1], NSA_GROUPS, HEAD_DIM)
    kv_new = [slab_s3[:, :, COL_KV + i * gl_:COL_KV + (i + 1) * gl_] for i in range(6)]
    s_state = (heads(kv_new[0]), heads(kv_new[1]), heads(kv_new[2]), heads(kv_new[3]),
               jnp.concatenate([cache_win_k[0], heads(kv_new[4])[0]], axis=1)[:, tn:][None],
               jnp.concatenate([cache_win_v[0], heads(kv_new[5])[0]], axis=1)[:, tn:][None],
               slab_s3[:, tn - 1, COL_PR:COL_PR + RWKV_PROJ][None],
               s_s[None])
    return (y_prompt, y_sample) + p_state + s_state
```

```python
import functools

import numpy as np
import jax
import jax.numpy as jnp
from jax import lax
from jax.experimental import pallas as pl
from jax.experimental.pallas import tpu as pltpu

F32 = jnp.float32
BF16 = jnp.bfloat16

D_MODEL = 1024
HEAD_DIM = 64
NSA_WIDTH = 768
NSA_HEADS = 12
NSA_GROUPS = 2
NSA_HPG = 6
CMP_LEN = 32
CMP_STRIDE = 16
CMP_HID = 64
SLC_BLOCK = 64
N_SELECT = 16
WINDOW = 512
Q_BLOCK = 64
FORCE_SCORE = 1e4
RWKV_WIDTH = 768
RWKV_HEAD_DIM = 64
RWKV_HEADS = 12
DECAY_LORA = 64
ICL_LORA = 64
GATE_LORA = 128
RWKV_PROJ = 3 * RWKV_WIDTH + DECAY_LORA + ICL_LORA + GATE_LORA
GN_EPS = 64e-5
MEM_HEADS = 4
MEM_WIDTH = 512
MEM_HEAD_DIM = 128
N_BRANCHES = 3
NSA_KV_COLS = 3 * 2 * NSA_GROUPS * HEAD_DIM
GROUP_LANES = NSA_GROUPS * HEAD_DIM
RMS_EPS = 1e-6
NEG = -1e30
LOG2E = 1.4426950408889634

COL_MG = 0
COL_PR = 3072
COL_MQ = 5632
COL_Q = 6144
COL_GN = 6912
COL_KV = 7168
SLAB_COLS = 8192
MAX_PAGES_PER_STEP = 16


def _pages_per_step(n_pages):
    n = min(MAX_PAGES_PER_STEP, n_pages)
    assert n_pages % n == 0
    return n

VMEM_LIMIT = 56 * 1024 * 1024


def _dot(a, b):
    return jnp.dot(a, b, preferred_element_type=F32)


def _dot_nt(a, b):
    return lax.dot_general(a, b, (((1,), (1,)), ((), ())), preferred_element_type=F32)


def _dot_tn(a, b):
    return lax.dot_general(a, b, (((0,), (0,)), ((), ())), preferred_element_type=F32)


def _iota(shape, dim):
    return lax.broadcasted_iota(jnp.int32, shape, dim)


def _eye(n, dtype):
    return (_iota((n, n), 0) == _iota((n, n), 1)).astype(dtype)


def _sigmoid(x):
    return 1.0 / (1.0 + jnp.exp(-x))


def _tanh(x):
    t = jnp.exp(-2.0 * jnp.abs(x))
    r = (1.0 - t) / (1.0 + t)
    return jnp.where(x < 0.0, -r, r)


def _gelu_tanh(x):
    return 0.5 * x * (1.0 + jnp.tanh(np.sqrt(2.0 / np.pi).astype(np.float32) * (x + 0.044715 * (x * x * x))))


def _rms(x, g):
    ms = jnp.mean(x * x, axis=-1, keepdims=True)
    return (x * lax.rsqrt(ms + RMS_EPS)) * g


def _pad_rows(x, n):
    if x.shape[0] == n:
        return x
    return jnp.concatenate([x, jnp.zeros((n - x.shape[0],) + x.shape[1:], x.dtype)], axis=0)


def _norm_matmul_body(x_ref, g_ref, w_ref, o_ref, h_ref):
    @pl.when(pl.program_id(1) == 0)
    def _():
        h_ref[...] = _rms(x_ref[...], g_ref[...]).astype(BF16)

    o_ref[...] = _dot(h_ref[...], w_ref[...])


def norm_matmul(x, g, w, tm, tn):
    n, d = x.shape
    c = w.shape[1]
    return pl.pallas_call(
        _norm_matmul_body,
        grid=(n // tm, c // tn),
        in_specs=[pl.BlockSpec((tm, d), lambda i, j: (i, 0)),
                  pl.BlockSpec((1, d), lambda i, j: (0, 0)),
                  pl.BlockSpec((d, tn), lambda i, j: (0, j))],
        out_specs=pl.BlockSpec((tm, tn), lambda i, j: (i, j)),
        out_shape=jax.ShapeDtypeStruct((n, c), F32),
        scratch_shapes=[pltpu.VMEM((tm, d), BF16)],
        compiler_params=pltpu.CompilerParams(dimension_semantics=("parallel", "arbitrary"),
                                             vmem_limit_bytes=VMEM_LIMIT),
    )(x, g, w)


def _proj_prompt_body(x_ref, g_ref, w_ref, wkv_ref, wkvt_ref, o_ref, kvt_ref, ktb_ref, vb_ref, ck_ref, cv_ref, h_ref):
    @pl.when(pl.program_id(1) == 0)
    def _():
        h = _rms(x_ref[...], g_ref[...]).astype(BF16)
        h_ref[...] = h
        gl = GROUP_LANES
        kvt = _dot_nt(wkvt_ref[...], h)
        kvt_ref[0] = kvt
        ktb_ref[0] = jnp.concatenate([kvt[2 * gl:3 * gl], kvt[4 * gl:5 * gl]], axis=0).astype(BF16)
        kv = _dot(h, wkv_ref[...])
        ck_ref[...] = kv[:, 0:gl].astype(BF16)
        cv_ref[...] = kv[:, gl:2 * gl].astype(BF16)
        vb_ref[...] = jnp.concatenate([kv[:, 3 * gl:4 * gl], kv[:, 5 * gl:6 * gl]], axis=1).astype(BF16)

    o_ref[...] = _dot(h_ref[...], w_ref[...])


def proj_prompt(x, g, w, wkv, wkvt, b, t, tm, tn):
    n, d = x.shape
    c = w.shape[1]
    tpb = t // tm
    gl = GROUP_LANES
    full = lambda a: pl.BlockSpec(a.shape, lambda i, j: (0, 0))
    rows = lambda width: pl.BlockSpec((tm, width), lambda i, j: (i, 0))
    return pl.pallas_call(
        _proj_prompt_body,
        grid=(n // tm, c // tn),
        in_specs=[pl.BlockSpec((tm, d), lambda i, j: (i, 0)), full(g),
                  pl.BlockSpec((d, tn), lambda i, j: (0, j)), full(wkv), full(wkvt)],
        out_specs=[pl.BlockSpec((tm, tn), lambda i, j: (i, j)),
                   pl.BlockSpec((1, NSA_KV_COLS, tm), lambda i, j: (i // tpb, 0, i % tpb)),
                   pl.BlockSpec((1, 2 * gl, tm), lambda i, j: (i // tpb, 0, i % tpb)),
                   rows(2 * gl), rows(gl), rows(gl)],
        out_shape=[jax.ShapeDtypeStruct((n, c), F32),
                   jax.ShapeDtypeStruct((b, NSA_KV_COLS, t), F32),
                   jax.ShapeDtypeStruct((b, 2 * gl, t), BF16),
                   jax.ShapeDtypeStruct((n, 2 * gl), BF16),
                   jax.ShapeDtypeStruct((n, gl), BF16),
                   jax.ShapeDtypeStruct((n, gl), BF16)],
        scratch_shapes=[pltpu.VMEM((tm, d), BF16)],
        compiler_params=pltpu.CompilerParams(dimension_semantics=("parallel", "arbitrary"),
                                             vmem_limit_bytes=VMEM_LIMIT),
    )(x, g, w, wkv, wkvt)


def _compress_consts(pe_ref, b1_ref, w0_ref, w1_ref):
    pe0 = jnp.broadcast_to(pe_ref[0], (8, pe_ref.shape[2])).astype(BF16)
    pe1 = jnp.broadcast_to(pe_ref[1], (8, pe_ref.shape[2])).astype(BF16)
    c = _dot(pe0, w0_ref[...]) + _dot(pe1, w1_ref[...])
    return c[0:1] + b1_ref[...]


def _compress_finish(u0, u1, cst, w2):
    n = u0.shape[0]
    pre = u0 + pltpu.roll(u1, n - 1, 0) + cst
    out = _dot(_gelu_tanh(pre).astype(BF16), w2)
    return jnp.where(_iota(out.shape, 0) < n - 1, out, 0.0)


def _compress_prompt_body(xk_ref, xv_ref, pek_ref, b1k_ref, w0k_ref, w1k_ref, w2k_ref,
                          pev_ref, b1v_ref, w0v_ref, w1v_ref, w2v_ref, ok_ref, ov_ref):
    for x_ref, pe_ref, b1_ref, w0_ref, w1_ref, w2_ref, o_ref in (
            (xk_ref, pek_ref, b1k_ref, w0k_ref, w1k_ref, w2k_ref, ok_ref),
            (xv_ref, pev_ref, b1v_ref, w0v_ref, w1v_ref, w2v_ref, ov_ref)):
        x = x_ref[0].astype(BF16)
        cst = _compress_consts(pe_ref, b1_ref, w0_ref, w1_ref)
        res = _compress_finish(_dot(x, w0_ref[...]), _dot(x, w1_ref[...]), cst, w2_ref[...])
        for g in range(NSA_GROUPS):
            o_ref[0, g] = res[:, g * HEAD_DIM:(g + 1) * HEAD_DIM].astype(o_ref.dtype)


def _compress_weights(pe, w1, b1, w2):
    r = CMP_LEN // CMP_STRIDE
    eye = jnp.eye(NSA_GROUPS, dtype=F32)
    w1r = w1.reshape(r, CMP_STRIDE, HEAD_DIM, CMP_HID)
    w1e = jnp.einsum('icdh,gk->icgdkh', w1r, eye).reshape(r, CMP_STRIDE * NSA_GROUPS * HEAD_DIM,
                                                         NSA_GROUPS * CMP_HID)
    pee = jnp.broadcast_to(pe.reshape(r, CMP_STRIDE, 1, HEAD_DIM), (r, CMP_STRIDE, NSA_GROUPS, HEAD_DIM))
    pee = pee.reshape(r, 1, CMP_STRIDE * NSA_GROUPS * HEAD_DIM)
    b1e = jnp.tile(b1, NSA_GROUPS).reshape(1, NSA_GROUPS * CMP_HID)
    w2e = jnp.einsum('hd,gk->ghkd', w2, eye).reshape(NSA_GROUPS * CMP_HID, NSA_GROUPS * HEAD_DIM)
    return pee, b1e, w1e[0].astype(BF16), w1e[1].astype(BF16), w2e.astype(BF16)


def compress_prompt(xk, xv, wk, wv):
    b, nch, width = xk.shape
    full = lambda a: pl.BlockSpec(a.shape, lambda i: (0,) * a.ndim)
    xspec = pl.BlockSpec((1, nch, width), lambda i: (i, 0, 0))
    ospec = pl.BlockSpec((1, NSA_GROUPS, nch, HEAD_DIM), lambda i: (i, 0, 0, 0))
    oshape = jax.ShapeDtypeStruct((b, NSA_GROUPS, nch, HEAD_DIM), BF16)
    return pl.pallas_call(
        _compress_prompt_body,
        grid=(b,),
        in_specs=[xspec, xspec] + [full(a) for a in wk] + [full(a) for a in wv],
        out_specs=[ospec, ospec],
        out_shape=[oshape, oshape],
        compiler_params=pltpu.CompilerParams(dimension_semantics=("parallel",), vmem_limit_bytes=VMEM_LIMIT),
    )(xk, xv, *wk, *wv)


def _split3(x):
    hi = x.astype(BF16)
    r1 = x - hi.astype(F32)
    mid = r1.astype(BF16)
    lo = (r1 - mid.astype(F32)).astype(BF16)
    return hi, mid, lo


def _select_blocks_lanes(score, n_blocks):
    nq, nbp = score.shape
    cols = [score[:, jp:jp + 1] for jp in range(n_blocks)]
    cnts = []
    for c0 in range(0, nbp, 128):
        blk = score[:, c0:c0 + 128]
        jj = c0 + _iota(blk.shape, 1)
        cnt = jnp.zeros(blk.shape, F32)
        for jp in range(n_blocks):
            ge = jnp.where(cols[jp] >= blk, 1.0, 0.0)
            gt = jnp.where(cols[jp] > blk, 1.0, 0.0)
            if jp < c0:
                cnt = cnt + ge
            elif jp >= c0 + 128:
                cnt = cnt + gt
            else:
                cnt = cnt + jnp.where(jj > jp, ge, gt)
        cnts.append(cnt)
    cnt = jnp.concatenate(cnts, axis=1)
    return jnp.where((cnt < N_SELECT) & (score > -jnp.inf), 1.0, 0.0)


def _select_blocks_unrolled(score):
    nb = score.shape[0]
    rows = [score[jp:jp + 1, :] for jp in range(nb)]
    cnts = []
    for r in range(nb // 8):
        blk = score[8 * r:8 * r + 8]
        jj = 8 * r + _iota(blk.shape, 0)
        cnt = jnp.zeros(blk.shape, F32)
        for jp in range(nb):
            ge = jnp.where(rows[jp] >= blk, 1.0, 0.0)
            gt = jnp.where(rows[jp] > blk, 1.0, 0.0)
            if jp < 8 * r:
                cnt = cnt + ge
            elif jp >= 8 * r + 8:
                cnt = cnt + gt
            else:
                cnt = cnt + jnp.where(jj > jp, ge, gt)
        cnts.append(cnt)
    cnt = jnp.concatenate(cnts, axis=0)
    return jnp.where((cnt < N_SELECT) & (score > -jnp.inf), 1.0, 0.0)


def _masked_softmax_rows(s, valid):
    s = jnp.where(valid, s, NEG)
    m = jnp.max(s, axis=-1, keepdims=True)
    p = jnp.where(valid, jnp.exp2(s - m), 0.0)
    l = jnp.sum(p, axis=-1, keepdims=True)
    return p / jnp.where(l > 0.0, l, 1.0)


def _online_update(carry, s, valid, v, pv=_dot):
    m, l, acc = carry
    s = jnp.where(valid, s, NEG)
    m_new = jnp.maximum(m, jnp.max(s, axis=-1, keepdims=True))
    alpha = jnp.exp2(m - m_new)
    p = jnp.where(valid, jnp.exp2(s - m_new), 0.0)
    l = alpha * l + jnp.sum(p, axis=-1, keepdims=True)
    acc = alpha * acc + pv(p.astype(BF16), v)
    return m_new, l, acc


def _online_update_biased(carry, s, v):
    m, l, acc = carry
    m_new = jnp.maximum(m, jnp.max(s, axis=-1, keepdims=True))
    alpha = jnp.exp2(m - m_new)
    p = jnp.exp2(s - m_new)
    l = alpha * l + jnp.sum(p, axis=-1, keepdims=True)
    acc = alpha * acc + _dot(p.astype(BF16), v)
    return m_new, l, acc


KV_TILE = 512
WIN_TILE = 640


def _nsa_prompt_body(q_ref, gn_ref, kc_ref, vc_ref, kts_ref, vs_ref, ktw_ref, vw_ref, ovt_ref, o_ref):
    g = pl.program_id(1)
    qb = pl.program_id(2)
    nq = Q_BLOCK
    hpg = NSA_HPG
    rows = hpg * nq
    qf = q_ref[...]
    q2f = jnp.concatenate([qf[:, h * HEAD_DIM:(h + 1) * HEAD_DIM] for h in range(hpg)], axis=0)
    q2 = q2f.astype(BF16)
    zero = jnp.zeros_like(q2f)
    q2w = jnp.where(g == 0, jnp.concatenate([q2f, zero], axis=1), jnp.concatenate([zero, q2f], axis=1)).astype(BF16)
    t_q1 = qb * nq + _iota((nq, 1), 0)
    tile6 = lambda x: jnp.concatenate([x] * hpg, axis=0)
    pick = lambda x: jnp.where(g == 0, x[:, 0:HEAD_DIM], x[:, HEAD_DIM:2 * HEAD_DIM])

    kc = kc_ref[0, 0]
    ncp = kc.shape[0]
    c_end = _iota((nq, ncp), 1) * CMP_STRIDE + (CMP_LEN - 1)
    s_c = _dot_nt(q2, kc) + tile6(jnp.where(c_end <= t_q1, 0.0, NEG))
    e_c = jnp.exp2(s_c - jnp.max(s_c, axis=-1, keepdims=True))
    l_c = jnp.sum(e_c, axis=-1, keepdims=True)
    any_c = tile6(t_q1 >= CMP_LEN - 1)
    p_c = e_c * jnp.where(any_c, 1.0 / l_c, 0.0)
    o_c = _dot(p_c.astype(BF16), vc_ref[0, 0])
    psum = p_c[0:nq]
    for h in range(1, hpg):
        psum = psum + p_c[h * nq:(h + 1) * nq]

    imp_t = sum(_dot_nt(ovt_ref[...], part) for part in _split3(psum))
    j_idx = _iota(imp_t.shape, 0)
    forced = (j_idx == 0) | (j_idx == qb) | (j_idx == qb - 1)
    score = jnp.where(forced, FORCE_SCORE, imp_t)
    sel_t = _select_blocks_unrolled(jnp.where(j_idx <= qb, score, -jnp.inf))
    sel = _dot_nt(_eye(nq, BF16), sel_t.astype(BF16)).astype(BF16)

    w0 = pl.multiple_of(jnp.maximum(qb * nq - WINDOW, 0) // 128 * 128, 128)
    diff = t_q1 - (w0 + _iota((nq, WIN_TILE), 1))
    ok_w = (diff >= 0) & (diff <= WINDOW)
    s_w = _dot(q2w, ktw_ref[0, :, pl.ds(w0, WIN_TILE)]) + tile6(jnp.where(ok_w, 0.0, NEG))
    e_w = jnp.exp2(s_w - jnp.max(s_w, axis=-1, keepdims=True))
    l_w = jnp.sum(e_w, axis=-1, keepdims=True)
    o_w = pick(_dot(e_w.astype(BF16), vw_ref[0, pl.ds(w0, WIN_TILE), :])) / l_w

    bpt = KV_TILE // SLC_BLOCK
    col_blk = _iota((sel.shape[1], KV_TILE), 1) // SLC_BLOCK
    row_blk = _iota((sel.shape[1], KV_TILE), 0)

    def block_mask(kt):
        expand = jnp.where(row_blk == col_blk + kt * bpt, 1.0, 0.0).astype(BF16)
        return _dot(sel, expand) > 0.5

    def scores(kt):
        off = pl.multiple_of(kt * KV_TILE, KV_TILE)
        return _dot(q2w, kts_ref[0, :, pl.ds(off, KV_TILE)])

    def values(kt):
        off = pl.multiple_of(kt * KV_TILE, KV_TILE)
        return vs_ref[0, pl.ds(off, KV_TILE), :]

    nt = qb // bpt
    ok_d = block_mask(nt) & (_iota((nq, KV_TILE), 1) + nt * KV_TILE <= t_q1)
    init = (jnp.full((rows, 1), NEG, F32), jnp.zeros((rows, 1), F32), jnp.zeros((rows, GROUP_LANES), F32))
    carry = _online_update_biased(init, scores(nt) + tile6(jnp.where(ok_d, 0.0, NEG)), values(nt))

    def pair_step(i, carry):
        k0 = 2 * i
        k1 = 2 * i + 1
        s0 = scores(k0)
        s1 = scores(k1)
        b0 = jnp.where(block_mask(k0), 0.0, NEG)
        b1 = jnp.where(block_mask(k1) & (k1 < nt), 0.0, NEG)
        carry = _online_update_biased(carry, s0 + tile6(b0), values(k0))
        return _online_update_biased(carry, s1 + tile6(b1), values(k1))

    _, l_s, acc_s = lax.fori_loop(0, (nt + 1) // 2, pair_step, carry)
    o_s = pick(acc_s) / l_s

    gates = _sigmoid(gn_ref[...])
    per_group = hpg * 3
    gates = jnp.where(g == 0, gates[:, 0:per_group], gates[:, per_group:2 * per_group])
    outs = []
    for h in range(hpg):
        sl = slice(h * nq, (h + 1) * nq)
        outs.append(gates[:, 3 * h:3 * h + 1] * o_c[sl] + gates[:, 3 * h + 1:3 * h + 2] * o_s[sl]
                    + gates[:, 3 * h + 2:3 * h + 3] * o_w[sl])
    o_ref[...] = jnp.concatenate(outs, axis=1).astype(o_ref.dtype)


def _overlap_t(n_blocks, n_cmp_padded, n_cmp):
    i = np.arange(n_cmp_padded)[None, :] * CMP_STRIDE
    j = np.arange(n_blocks)[:, None] * SLC_BLOCK
    ov = (i < j + SLC_BLOCK) & (i + CMP_LEN > j) & (np.arange(n_cmp_padded)[None, :] < n_cmp)
    return jnp.asarray(ov.astype(np.float32)).astype(BF16)


def nsa_prompt(slab, kc, vc, ktb, vb, b, t):
    nb = t // Q_BLOCK
    gw = NSA_HPG * HEAD_DIM
    gl = GROUP_LANES
    ncp = kc.shape[2]
    ovt = _overlap_t(nb, ncp, ncp - 1)
    kt_spec = lambda k: pl.BlockSpec((1, gl, t), lambda bi, g, qb: (bi, k, 0))
    v_spec = lambda k: pl.BlockSpec((1, t, gl), lambda bi, g, qb: (bi, 0, k))
    cmp_spec = pl.BlockSpec((1, 1, ncp, HEAD_DIM), lambda bi, g, qb: (bi, g, 0, 0))
    return pl.pallas_call(
        _nsa_prompt_body,
        grid=(b, NSA_GROUPS, nb),
        in_specs=[pl.BlockSpec((Q_BLOCK, gw), lambda bi, g, qb: (bi * nb + qb, COL_Q // gw + g)),
                  pl.BlockSpec((Q_BLOCK, 128), lambda bi, g, qb: (bi * nb + qb, COL_GN // 128)),
                  cmp_spec, cmp_spec, kt_spec(0), v_spec(0), kt_spec(1), v_spec(1),
                  pl.BlockSpec(ovt.shape, lambda bi, g, qb: (0, 0))],
        out_specs=pl.BlockSpec((Q_BLOCK, gw), lambda bi, g, qb: (bi * nb + qb, g)),
        out_shape=jax.ShapeDtypeStruct((b * t, NSA_WIDTH), BF16),
        compiler_params=pltpu.CompilerParams(dimension_semantics=("parallel", "parallel", "arbitrary"),
                                             vmem_limit_bytes=VMEM_LIMIT),
    )(slab, slab, kc, vc, ktb, vb, ktb, vb, ovt)


def _compress_sample_body(n, pt_ref, *refs):
    k_pages, v_pages = refs[0:n], refs[n:2 * n]
    (pek_ref, b1k_ref, w0k_ref, w1k_ref, w2k_ref, wck_ref,
     pev_ref, b1v_ref, w0v_ref, w1v_ref, w2v_ref, wcv_ref, ok_ref, ov_ref, uk_ref, uv_ref) = refs[2 * n:]
    j = pl.program_id(1)
    page_rows = k_pages[0].shape[2]
    cpp = page_rows // CMP_STRIDE
    rows = n * cpp
    off = pl.multiple_of(j * rows, rows)
    rp = _iota((page_rows, page_rows), 0)
    perm = (_iota((page_rows, page_rows), 1) == CMP_STRIDE * (rp % cpp) + rp // cpp).astype(BF16)
    for pages, wc_ref, u_ref in ((k_pages, wck_ref, uk_ref), (v_pages, wcv_ref, uv_ref)):
        xp = [_dot_nt(perm, r[0].astype(BF16)) for r in pages]
        x = jnp.concatenate([jnp.concatenate([p[c * cpp:(c + 1) * cpp] for p in xp], axis=0)
                             for c in range(CMP_STRIDE)], axis=1)
        u_ref[pl.ds(off, rows), :] = _dot(x.astype(BF16), wc_ref[...])

    @pl.when(j == pl.num_programs(1) - 1)
    def _():
        half = NSA_GROUPS * CMP_HID
        for pe_ref, b1_ref, w0_ref, w1_ref, w2_ref, u_ref, o_ref in (
                (pek_ref, b1k_ref, w0k_ref, w1k_ref, w2k_ref, uk_ref, ok_ref),
                (pev_ref, b1v_ref, w0v_ref, w1v_ref, w2v_ref, uv_ref, ov_ref)):
            cst = _compress_consts(pe_ref, b1_ref, w0_ref, w1_ref)
            u = u_ref[...]
            res = _compress_finish(u[:, 0:half], u[:, half:2 * half], cst, w2_ref[...])
            for g in range(NSA_GROUPS):
                o_ref[0, g] = res[:, g * HEAD_DIM:(g + 1) * HEAD_DIM].astype(o_ref.dtype)


def compress_sample(pool_k, pool_v, page_table, wk, wv):
    bs, n_pages = page_table.shape
    _, lanes, page_rows = pool_k.shape
    n = _pages_per_step(n_pages)
    nch = n_pages * page_rows // CMP_STRIDE
    wck = jnp.concatenate([wk[2], wk[3]], axis=1)
    wcv = jnp.concatenate([wv[2], wv[3]], axis=1)
    page = lambda k: pl.BlockSpec((1, lanes, page_rows), lambda b, j, pt: (pt[b, n * j + k], 0, 0))
    full = lambda a: pl.BlockSpec(a.shape, lambda b, j, pt: (0,) * a.ndim)
    consts = list(wk) + [wck] + list(wv) + [wcv]
    ospec = pl.BlockSpec((1, NSA_GROUPS, nch, HEAD_DIM), lambda b, j, pt: (b, 0, 0, 0))
    oshape = jax.ShapeDtypeStruct((bs, NSA_GROUPS, nch, HEAD_DIM), BF16)
    return pl.pallas_call(
        functools.partial(_compress_sample_body, n),
        grid_spec=pltpu.PrefetchScalarGridSpec(
            num_scalar_prefetch=1,
            grid=(bs, n_pages // n),
            in_specs=[page(k) for k in range(n)] * 2 + [full(a) for a in consts],
            out_specs=[ospec, ospec],
            scratch_shapes=[pltpu.VMEM((nch, 2 * NSA_GROUPS * CMP_HID), F32)] * 2),
        out_shape=[oshape, oshape],
        compiler_params=pltpu.CompilerParams(dimension_semantics=("parallel", "arbitrary"),
                                             vmem_limit_bytes=VMEM_LIMIT),
    )(page_table, *([pool_k] * n), *([pool_v] * n), *consts)


TOK_PAD = 8


def _nsa_sample_body(past_len, tn, n, pt_ref, *refs):
    q_ref, gn_ref, skn_ref, svn_ref, wkn_ref, wvn_ref, kc_ref, vc_ref, wkc_ref, wvc_ref = refs[0:10]
    k_pages, v_pages = refs[10:10 + n], refs[10 + n:10 + 2 * n]
    ov_ref, o_ref, selt_ref, m_ref, l_ref, acc_ref, oc_ref, ow_ref = refs[10 + 2 * n:]
    j = pl.program_id(1)
    tp = TOK_PAD
    hpg = NSA_HPG
    grows = hpg * tp
    rows = NSA_GROUPS * grows
    gw = hpg * HEAD_DIM
    lanes = GROUP_LANES
    nsp = selt_ref.shape[0]

    q8 = _pad_rows(q_ref[0], tp)
    zero = jnp.zeros((grows, HEAD_DIM), F32)
    q2, q_parts = [], []
    for g in range(NSA_GROUPS):
        qg = jnp.concatenate([q8[:, g * gw + h * HEAD_DIM:g * gw + (h + 1) * HEAD_DIM] for h in range(hpg)], axis=0)
        q2.append(qg.astype(BF16))
        q_parts.append(jnp.concatenate([qg, zero] if g == 0 else [zero, qg], axis=1))
    q_all = jnp.concatenate(q_parts, axis=0).astype(BF16)
    tok = _iota((rows, 1), 0) % tp
    t_q = past_len + tok

    def stack_groups(x):
        return jnp.concatenate([x[g * tp:(g + 1) * tp] for g in range(NSA_GROUPS) for _ in range(hpg)], axis=0)

    def new_keys_valid(width):
        tk = _iota((rows, width), 1)
        return (tk <= tok) & (tk < tn)

    @pl.when(j == 0)
    def _():
        psums = []
        tq_g = t_q[0:grows]
        for g in range(NSA_GROUPS):
            s_c = _dot_nt(q2[g], kc_ref[0, g])
            c_end = _iota(s_c.shape, 1) * CMP_STRIDE + (CMP_LEN - 1)
            p_c = _masked_softmax_rows(s_c, c_end <= tq_g)
            oc_ref[g] = _dot(p_c.astype(BF16), vc_ref[0, g])
            ps = p_c[0:tp]
            for h in range(1, hpg):
                ps = ps + p_c[h * tp:(h + 1) * tp]
            psums.append(ps)
        psum = jnp.concatenate(psums, axis=0)
        imp = sum(_dot(part, ov_ref[...]) for part in _split3(psum))
        j_idx = _iota(imp.shape, 1)
        cur = (past_len + _iota(imp.shape, 0) % tp) // SLC_BLOCK
        forced = (j_idx == 0) | (j_idx == cur) | (j_idx == cur - 1)
        score = jnp.where(j_idx <= cur, jnp.where(forced, FORCE_SCORE, imp), -jnp.inf)
        sel = _select_blocks_lanes(score, (past_len + tn - 1) // SLC_BLOCK + 1)
        pick_row = (_iota((nsp, sel.shape[1]), 0) == _iota((nsp, sel.shape[1]), 1)).astype(BF16)
        selt_ref[...] = _dot_nt(pick_row, sel.astype(BF16))

        lw = wkc_ref.shape[2]
        kwn = _pad_rows(wkn_ref[0], 16).astype(BF16)
        vwn = _pad_rows(wvn_ref[0], 16).astype(BF16)
        diff = t_q - (past_len - lw + _iota((rows, lw), 1))
        carry = (jnp.full((rows, 1), NEG, F32), jnp.zeros((rows, 1), F32), jnp.zeros((rows, lanes), F32))
        carry = _online_update(carry, _dot(q_all, wkc_ref[0].astype(BF16)), (diff >= 0) & (diff <= WINDOW),
                               wvc_ref[0].astype(BF16), pv=_dot_nt)
        _, l_w, acc_w = _online_update(carry, _dot_nt(q_all, kwn), new_keys_valid(16), vwn)
        ow_ref[...] = acc_w / l_w
        m_ref[...] = jnp.full((rows, 1), NEG, F32)
        l_ref[...] = jnp.zeros((rows, 1), F32)
        acc_ref[...] = jnp.zeros((rows, lanes), F32)

    page_rows = k_pages[0].shape[2]
    nk = n * page_rows
    bps = nk // SLC_BLOCK
    kt = jnp.concatenate([r[0] for r in k_pages], axis=1).astype(BF16)
    vt = jnp.concatenate([r[0] for r in v_pages], axis=1).astype(BF16)
    expand = jnp.where(_iota((bps, nk), 0) == _iota((bps, nk), 1) // SLC_BLOCK, 1.0, 0.0).astype(BF16)
    sel_rows = selt_ref[pl.ds(pl.multiple_of(j * bps, bps), bps), :].astype(BF16)
    bias = stack_groups(jnp.where(_dot_tn(sel_rows, expand) > 0.5, 0.0, NEG))
    m, l, acc = (m_ref[...], l_ref[...], acc_ref[...])
    s = _dot(q_all, kt) + bias
    m_new = jnp.maximum(m, jnp.max(s, axis=-1, keepdims=True))
    alpha = jnp.exp2(m - m_new)
    p = jnp.exp2(s - m_new)
    m_ref[...] = m_new
    l_ref[...] = alpha * l + jnp.sum(p, axis=-1, keepdims=True)
    acc_ref[...] = alpha * acc + _dot_nt(p.astype(BF16), vt)

    @pl.when(j == pl.num_programs(1) - 1)
    def _():
        kn = _pad_rows(skn_ref[0], 16).astype(BF16)
        vn = _pad_rows(svn_ref[0], 16).astype(BF16)
        expand_n = jnp.where(_iota((nsp, 16), 0) == (past_len + _iota((nsp, 16), 1)) // SLC_BLOCK,
                             1.0, 0.0).astype(BF16)
        sel_n = stack_groups(_dot_tn(selt_ref[...].astype(BF16), expand_n)) > 0.5
        _, l_s, acc_s = _online_update((m_ref[...], l_ref[...], acc_ref[...]), _dot_nt(q_all, kn),
                                       sel_n & new_keys_valid(16), vn)
        o_s_all = acc_s / l_s
        o_w_all = ow_ref[...]
        gates = _sigmoid(_pad_rows(gn_ref[0], tp))
        outs = []
        for g in range(NSA_GROUPS):
            gsl = slice(g * HEAD_DIM, (g + 1) * HEAD_DIM)
            o_c = oc_ref[g]
            for h in range(hpg):
                sl = slice(h * tp, (h + 1) * tp)
                asl = slice(g * grows + h * tp, g * grows + (h + 1) * tp)
                c0 = (g * hpg + h) * 3
                outs.append(gates[:, c0:c0 + 1] * o_c[sl] + gates[:, c0 + 1:c0 + 2] * o_s_all[asl, gsl]
                            + gates[:, c0 + 2:c0 + 3] * o_w_all[asl, gsl])
        o_ref[0] = jnp.concatenate(outs, axis=1)[0:tn].astype(o_ref.dtype)


def nsa_sample(slab3, kc, vc, win_kt, win_vt, pool_kt, pool_vt, page_table, past_len):
    bs, tn, _ = slab3.shape
    n_pages = page_table.shape[1]
    n = _pages_per_step(n_pages)
    page_rows = pool_kt.shape[2]
    lanes = GROUP_LANES
    ncp = kc.shape[2]
    ns = -(-(past_len + tn) // SLC_BLOCK)
    nsp = -(-ns // 8) * 8
    ov = _overlap_t(-(-ns // 128) * 128, ncp, (past_len + tn) // CMP_STRIDE - CMP_LEN // CMP_STRIDE + 1).T
    rows = NSA_GROUPS * NSA_HPG * TOK_PAD
    tokblk = lambda width, col: pl.BlockSpec((1, tn, width), lambda b, j, pt: (b, 0, col // width))
    cmp_spec = pl.BlockSpec((1, NSA_GROUPS, ncp, HEAD_DIM), lambda b, j, pt: (b, 0, 0, 0))
    win_spec = pl.BlockSpec((1, lanes, win_kt.shape[2]), lambda b, j, pt: (b, 0, 0))
    page = lambda k: pl.BlockSpec((1, lanes, page_rows), lambda b, j, pt: (pt[b, n * j + k], 0, 0))
    return pl.pallas_call(
        functools.partial(_nsa_sample_body, past_len, tn, n),
        grid_spec=pltpu.PrefetchScalarGridSpec(
            num_scalar_prefetch=1,
            grid=(bs, n_pages // n),
            in_specs=[tokblk(NSA_WIDTH, COL_Q), tokblk(128, COL_GN),
                      tokblk(lanes, COL_KV + 2 * lanes), tokblk(lanes, COL_KV + 3 * lanes),
                      tokblk(lanes, COL_KV + 4 * lanes), tokblk(lanes, COL_KV + 5 * lanes),
                      cmp_spec, cmp_spec, win_spec, win_spec]
            + [page(k) for k in range(n)] * 2
            + [pl.BlockSpec(ov.shape, lambda b, j, pt: (0, 0))],
            out_specs=pl.BlockSpec((1, tn, NSA_WIDTH), lambda b, j, pt: (b, 0, 0)),
            scratch_shapes=[pltpu.VMEM((nsp, NSA_GROUPS * TOK_PAD), F32),
                            pltpu.VMEM((rows, 1), F32),
                            pltpu.VMEM((rows, 1), F32),
                            pltpu.VMEM((rows, lanes), F32),
                            pltpu.VMEM((NSA_GROUPS, NSA_HPG * TOK_PAD, HEAD_DIM), F32),
                            pltpu.VMEM((rows, lanes), F32)]),
        out_shape=jax.ShapeDtypeStruct((bs, tn, NSA_WIDTH), F32),
        compiler_params=pltpu.CompilerParams(dimension_semantics=("parallel", "arbitrary"),
                                             vmem_limit_bytes=VMEM_LIMIT),
    )(page_table, slab3, slab3, slab3, slab3, slab3, slab3, kc, vc, win_kt, win_vt,
      *([pool_kt] * n), *([pool_vt] * n), ov)


def _cumsum_rows(x):
    n = x.shape[0]
    row = _iota((n, 1), 0)
    k = 1
    while k < n:
        x = x + jnp.where(row >= k, pltpu.roll(x, k, 0), 0.0)
        k *= 2
    return x


def _rwkv_body(n_valid, chunk, p0_ref, p1_ref, p2_ref, p3_ref, p4_ref, prev_ref, s0_ref,
               mu_ref, w0_ref, w2_ref, a0_ref, a2_ref, g2_ref, kk_ref, ka_ref, rk_ref, lng_ref, lnb_ref,
               o_ref, sout_ref, carry_ref, s_ref):
    c = pl.program_id(1)
    hd = RWKV_HEAD_DIM

    @pl.when(c == 0)
    def _():
        carry_ref[...] = jnp.broadcast_to(prev_ref[0], carry_ref.shape)
        s_ref[...] = s0_ref[0]

    p = jnp.concatenate([r[0] for r in (p0_ref, p1_ref, p2_ref, p3_ref, p4_ref)], axis=1)
    p = _pad_rows(p, chunk)
    row = _iota((chunk, 1), 0)
    valid = row < n_valid
    prev = jnp.where(row == 0, carry_ref[0:1, :], pltpu.roll(p, 1, 0))
    xm = p + (prev - p) * mu_ref[...]
    carry_ref[...] = jnp.broadcast_to(p[n_valid - 1:n_valid, :], carry_ref.shape)

    wdt = RWKV_WIDTH
    r_all, k_all, v_all = xm[:, 0:wdt], xm[:, wdt:2 * wdt], xm[:, 2 * wdt:3 * wdt]
    o = 3 * wdt
    wd, ad, gd = xm[:, o:o + DECAY_LORA], xm[:, o + DECAY_LORA:o + DECAY_LORA + ICL_LORA], \
        xm[:, o + DECAY_LORA + ICL_LORA:o + DECAY_LORA + ICL_LORA + GATE_LORA]
    w = w0_ref[...] + _dot(_tanh(wd).astype(BF16), w2_ref[...])
    logw = -_sigmoid(w) * float(np.exp(-0.5))
    a_all = _sigmoid(a0_ref[...] + _dot(ad.astype(BF16), a2_ref[...]))
    g_all = _dot(_sigmoid(gd).astype(BF16), g2_ref[...])
    logw = jnp.where(valid, logw, 0.0)
    cum = _cumsum_rows(logw)
    total = cum[chunk - 1:chunk, :]
    w_in = jnp.exp(cum)
    w_ex = jnp.exp(cum - logw)
    w_inv = jnp.exp(-cum)
    w_rem = jnp.exp(total - cum)
    w_tot = jnp.exp(total)
    kk_all = k_all * kk_ref[...]
    k2_all = k_all * (1.0 + (a_all - 1.0) * ka_ref[...])

    t_i = _iota((chunk, chunk), 0)
    s_i = _iota((chunk, chunk), 1)
    strict = s_i < t_i
    incl = s_i <= t_i
    n_rounds = int(np.log2(chunk))
    heads = range(RWKV_HEADS)
    sls = [slice(h * hd, (h + 1) * hd) for h in heads]

    lr, bt, kt, bk, vb, at, rt = [], [], [], [], [], [], []
    head_of = (_iota((wdt, 128), 0) // hd == _iota((wdt, 128), 1)).astype(BF16)
    ssq = sum(_dot(part, head_of) for part in _split3(kk_all * kk_all))
    inv = sum(_dot_nt(part, head_of) for part in _split3(lax.rsqrt(jnp.maximum(ssq, 1e-24))))
    kkn_all = jnp.where(valid, kk_all * inv, 0.0)
    k2m_all = jnp.where(valid, k2_all, 0.0)
    vm_all = jnp.where(valid, v_all, 0.0)
    b_all = kkn_all * a_all
    at_all = -kkn_all * w_ex
    rt_all = r_all * w_in
    bt_all = b_all * w_inv
    kt_all = k2m_all * w_inv
    bp_all = b_all * w_rem
    kp_all = k2m_all * w_rem
    for sl in sls:
        at.append(at_all[:, sl].astype(BF16))
        rt.append(rt_all[:, sl].astype(BF16))
        lr.append(jnp.concatenate([at_all[:, sl], rt_all[:, sl]], axis=0).astype(BF16))
        bt.append(bt_all[:, sl].astype(BF16))
        kt.append(kt_all[:, sl].astype(BF16))
        bk.append(jnp.concatenate([bp_all[:, sl], kp_all[:, sl]], axis=0).astype(BF16))
        vb.append(vm_all[:, sl])
    m_b = [_dot_nt(lr[h], bt[h]) for h in heads]
    m_k = [_dot_nt(lr[h], kt[h]) for h in heads]
    a_ab = [jnp.where(strict, m[0:chunk], 0.0) for m in m_b]
    a_rb = [jnp.where(incl, m[chunk:2 * chunk], 0.0).astype(BF16) for m in m_b]
    a_ak = [jnp.where(strict, m[0:chunk], 0.0).astype(BF16) for m in m_k]
    a_rk = [jnp.where(incl, m[chunk:2 * chunk], 0.0).astype(BF16) for m in m_k]
    s0 = [s_ref[h] for h in heads]
    s0b = [x.astype(BF16) for x in s0]
    vbb = [x.astype(BF16) for x in vb]
    u = [_dot_nt(at[h], s0b[h]) + _dot(a_ak[h], vbb[h]) for h in heads]
    pw = a_ab
    for it in range(n_rounds):
        pwb = [x.astype(BF16) for x in pw]
        u = [u[h] + _dot(pwb[h], u[h].astype(BF16)) for h in heads]
        if it + 1 < n_rounds:
            pw = [_dot(x, x) for x in pwb]
    ub = [x.astype(BF16) for x in u]
    y = [_dot_nt(rt[h], s0b[h]) + _dot(a_rb[h], ub[h]) + _dot(a_rk[h], vbb[h]) for h in heads]
    for h in heads:
        uv = jnp.concatenate([u[h], vb[h]], axis=0).astype(BF16)
        s_ref[h] = s0[h] * w_tot[:, sls[h]] + _dot_tn(uv, bk[h])
    outs = []
    for h in heads:
        sl = sls[h]
        mean = jnp.mean(y[h], axis=-1, keepdims=True)
        yc = y[h] - mean
        var = jnp.mean(yc * yc, axis=-1, keepdims=True)
        yn = yc * lax.rsqrt(var + GN_EPS)
        bonus = jnp.sum(r_all[:, sl] * k2_all[:, sl] * rk_ref[:, sl], axis=-1, keepdims=True) * v_all[:, sl]
        outs.append((yn * lng_ref[:, sl] + lnb_ref[:, sl] + bonus) * g_all[:, sl])
    out = jnp.concatenate(outs, axis=1)
    o_ref[0] = out[0:o_ref.shape[1]].astype(o_ref.dtype)

    @pl.when(c == pl.num_programs(1) - 1)
    def _():
        sout_ref[0] = s_ref[...]


def rwkv(slab3, p_prev, s0, params, chunk, out_dtype):
    b, t, _ = slab3.shape
    tc = min(t, chunk)
    nchunks = t // tc
    blk = 512
    pspec = lambda k: pl.BlockSpec((1, tc, blk), lambda bi, c: (bi, c, COL_PR // blk + k))
    full = lambda a: pl.BlockSpec(a.shape, lambda bi, c: (0,) * a.ndim)
    sspec = pl.BlockSpec((1, RWKV_HEADS, RWKV_HEAD_DIM, RWKV_HEAD_DIM), lambda bi, c: (bi, 0, 0, 0))
    return pl.pallas_call(
        functools.partial(_rwkv_body, tc, chunk),
        grid=(b, nchunks),
        in_specs=[pspec(k) for k in range(5)]
        + [pl.BlockSpec((1, 1, RWKV_PROJ), lambda bi, c: (bi, 0, 0)), sspec]
        + [full(a) for a in params],
        out_specs=[pl.BlockSpec((1, tc, RWKV_WIDTH), lambda bi, c: (bi, c, 0)), sspec],
        out_shape=[jax.ShapeDtypeStruct((b, t, RWKV_WIDTH), out_dtype),
                   jax.ShapeDtypeStruct(s0.shape, F32)],
        scratch_shapes=[pltpu.VMEM((8, RWKV_PROJ), F32),
                        pltpu.VMEM((RWKV_HEADS, RWKV_HEAD_DIM, RWKV_HEAD_DIM), F32)],
        compiler_params=pltpu.CompilerParams(dimension_semantics=("parallel", "arbitrary"),
                                             vmem_limit_bytes=VMEM_LIMIT),
    )(slab3, slab3, slab3, slab3, slab3, p_prev, s0, *params)


def _mem_attend_body(q_ref, k_ref, v_ref, o_ref):
    nseq, tm = q_ref.shape[0], q_ref.shape[1]
    for bb in range(nseq):
        q = _pad_rows(q_ref[bb], max(tm, 16)).astype(BF16)
        k = k_ref[bb].astype(BF16)
        v = v_ref[bb].astype(BF16)
        outs = []
        for h in range(MEM_HEADS):
            sl = slice(h * MEM_HEAD_DIM, (h + 1) * MEM_HEAD_DIM)
            s = _dot_nt(q[:, sl], k[:, sl]) * (MEM_HEAD_DIM ** -0.5)
            m = jnp.max(s, axis=-1, keepdims=True)
            p = jnp.exp(s - m)
            p = p / jnp.sum(p, axis=-1, keepdims=True)
            outs.append(_dot(p.astype(BF16), v[:, sl]))
        o_ref[bb] = jnp.concatenate(outs, axis=1)[0:tm].astype(o_ref.dtype)


def mem_attend(slab3, mk, k_blk, mv, v_blk, tm, out_dtype, nseq=1):
    b, t, _ = slab3.shape
    m = mk.shape[1]
    return pl.pallas_call(
        _mem_attend_body,
        grid=(b // nseq, t // tm),
        in_specs=[pl.BlockSpec((nseq, tm, MEM_WIDTH), lambda bi, i: (bi, i, COL_MQ // MEM_WIDTH)),
                  pl.BlockSpec((nseq, m, MEM_WIDTH), lambda bi, i: (bi, 0, k_blk)),
                  pl.BlockSpec((nseq, m, MEM_WIDTH), lambda bi, i: (bi, 0, v_blk))],
        out_specs=pl.BlockSpec((nseq, tm, MEM_WIDTH), lambda bi, i: (bi, i, 0)),
        out_shape=jax.ShapeDtypeStruct((b, t, MEM_WIDTH), out_dtype),
        compiler_params=pltpu.CompilerParams(dimension_semantics=("parallel", "parallel"),
                                             vmem_limit_bytes=VMEM_LIMIT),
    )(slab3, mk, mv)


def _merge_body(x_ref, on_ref, or_ref, om_ref, g0_ref, g1_ref, g2_ref, wn_ref, wr_ref, wm_ref, wo_ref, o_ref):
    m = _sigmoid(g0_ref[...]) * _dot(on_ref[...].astype(BF16), wn_ref[...])
    m = m + _sigmoid(g1_ref[...]) * _dot(or_ref[...].astype(BF16), wr_ref[...])
    m = m + _sigmoid(g2_ref[...]) * _dot(om_ref[...].astype(BF16), wm_ref[...])
    o_ref[...] = x_ref[...] + _dot(m.astype(BF16), wo_ref[...])


def merge(x, o_nsa, o_rwkv, o_mem, slab, wn, wr, wm, wo, tm):
    n, d = x.shape
    row = lambda w: pl.BlockSpec((tm, w), lambda i: (i, 0))
    full = lambda a: pl.BlockSpec(a.shape, lambda i: (0, 0))
    gate = lambda k: pl.BlockSpec((tm, d), lambda i: (i, COL_MG // d + k))
    return pl.pallas_call(
        _merge_body,
        grid=(n // tm,),
        in_specs=[row(d), row(NSA_WIDTH), row(RWKV_WIDTH), row(MEM_WIDTH), gate(0), gate(1), gate(2),
                  full(wn), full(wr), full(wm), full(wo)],
        out_specs=row(d),
        out_shape=jax.ShapeDtypeStruct((n, d), F32),
        compiler_params=pltpu.CompilerParams(dimension_semantics=("parallel",), vmem_limit_bytes=VMEM_LIMIT),
    )(x, o_nsa, o_rwkv, o_mem, slab, slab, slab, wn, wr, wm, wo)


def _ffn_body(x_ref, gf_ref, wg_ref, wu_ref, wd_ref, gl_ref, o_ref):
    x = x_ref[...]
    hf = _rms(x, gf_ref[...]).astype(BF16)
    gate = _dot(hf, wg_ref[...])
    up = _dot(hf, wu_ref[...])
    act = (gate * _sigmoid(gate) * up).astype(BF16)
    x2 = x + _dot(act, wd_ref[...])
    o_ref[...] = _rms(x2, gl_ref[...])


def ffn(x, gf, wg, wu, wd, gl, tm):
    n, d = x.shape
    row = pl.BlockSpec((tm, d), lambda i: (i, 0))
    full = lambda a: pl.BlockSpec(a.shape, lambda i: (0, 0), pipeline_mode=pl.Buffered(1))
    return pl.pallas_call(
        _ffn_body,
        grid=(n // tm,),
        in_specs=[row, full(gf), full(wg), full(wu), full(wd), full(gl)],
        out_specs=row,
        out_shape=jax.ShapeDtypeStruct((n, d), F32),
        compiler_params=pltpu.CompilerParams(dimension_semantics=("parallel",), vmem_limit_bytes=VMEM_LIMIT),
    )(x, gf, wg, wu, wd, gl)


def _slab_weight(w_in):
    wq, wkv, wgn, wpr, wmq, wmg = jnp.split(w_in, np.cumsum(
        [NSA_WIDTH, NSA_KV_COLS, 3 * NSA_HEADS, RWKV_PROJ, MEM_WIDTH])[:5].tolist(), axis=1)
    d = w_in.shape[0]
    zeros = lambda n: jnp.zeros((d, n), w_in.dtype)
    w = jnp.concatenate([wmg, wpr, wmq, wq * (HEAD_DIM ** -0.5 * LOG2E), wgn, zeros(COL_KV - COL_GN - 3 * NSA_HEADS),
                         wkv, zeros(SLAB_COLS - COL_KV - NSA_KV_COLS)], axis=1)
    return w.astype(BF16), wkv.astype(BF16), wkv.T.astype(BF16)


def _channel_major_rows(x):
    b, _, t = x.shape
    return jnp.transpose(x.reshape(b, NSA_GROUPS, HEAD_DIM, t), (0, 3, 1, 2))[None]


def _channel_major_view(x):
    b, t = x.shape[:2]
    return jnp.transpose(x, (0, 2, 3, 1)).reshape(b, GROUP_LANES, t)


def kernel(x_prompt, x_sample, cache_cmp_k, cache_cmp_v, cache_slc_k, cache_slc_v, cache_win_k, cache_win_v, state_rwkv_shift, state_rwkv_wkv, cache_mem_k, cache_mem_v, page_table, mem_prompt, attn_norm, w_in, cmp_pe_k, cmp_w1_k, cmp_b1_k, cmp_w2_k, cmp_pe_v, cmp_w1_v, cmp_b1_v, cmp_w2_v, rwkv_mu, rwkv_w0, rwkv_w2, rwkv_a0, rwkv_a2, rwkv_g2, rwkv_kk, rwkv_ka, rwkv_rk, rwkv_ln_g, rwkv_ln_b, mem_norm, w_mem_kv, w_o_nsa, w_o_rwkv, w_o_mem, w_out, ffn_norm, w_gate, w_up, w_down, final_norm):
    assert w_in.shape[0] == 1, "one layer"
    bp, t, d = x_prompt.shape
    bs, tn, _ = x_sample.shape
    row2 = lambda a: a.reshape(1, -1)
    gl_ = GROUP_LANES

    w_slab, w_kv, w_kvt = _slab_weight(w_in[0])
    cmp_wk = _compress_weights(cmp_pe_k[0], cmp_w1_k[0], cmp_b1_k[0], cmp_w2_k[0])
    cmp_wv = _compress_weights(cmp_pe_v[0], cmp_w1_v[0], cmp_b1_v[0], cmp_w2_v[0])
    rw_params = (row2(rwkv_mu[0]), row2(rwkv_w0[0]), rwkv_w2[0].astype(BF16), row2(rwkv_a0[0]),
                 rwkv_a2[0].astype(BF16), rwkv_g2[0].astype(BF16), row2(rwkv_kk[0]), row2(rwkv_ka[0]),
                 row2(rwkv_rk[0]), row2(rwkv_ln_g[0]), row2(rwkv_ln_b[0]))
    wn, wr, wm, wo = (a[0].astype(BF16) for a in (w_o_nsa, w_o_rwkv, w_o_mem, w_out))
    wg, wu, wd = (a[0].astype(BF16) for a in (w_gate, w_up, w_down))
    gf, gl = row2(ffn_norm[0]), row2(final_norm)

    xp2 = x_prompt.reshape(bp * t, d)
    slab, kvt, ktb, vb, ck, cv = proj_prompt(xp2, row2(attn_norm[0]), w_slab[:, :COL_KV], w_kv, w_kvt, bp, t,
                                             1024, 1024)
    slab3 = slab.reshape(bp, t, COL_KV)
    nch = t // CMP_STRIDE
    kc, vc = compress_prompt(ck.reshape(bp, nch, CMP_STRIDE * gl_), cv.reshape(bp, nch, CMP_STRIDE * gl_),
                             cmp_wk, cmp_wv)
    o_nsa = nsa_prompt(slab, kc, vc, ktb, vb.reshape(bp, t, 2 * gl_), bp, t)
    o_rwkv, s_p = rwkv(slab3, jnp.zeros((bp, 1, RWKV_PROJ), F32),
                       jnp.zeros((bp, RWKV_HEADS, RWKV_HEAD_DIM, RWKV_HEAD_DIM), F32), rw_params, 64, BF16)
    mem_n = mem_prompt.shape[1]
    mkv = norm_matmul(mem_prompt.reshape(bp * mem_n, d), row2(mem_norm[0]), w_mem_kv[0].astype(BF16),
                      min(1024, bp * mem_n), 512).reshape(bp, mem_n, 2 * MEM_WIDTH)
    o_mem = mem_attend(slab3, mkv, 0, mkv, 1, 512, BF16)
    x1 = merge(xp2, o_nsa, o_rwkv.reshape(bp * t, RWKV_WIDTH), o_mem.reshape(bp * t, MEM_WIDTH), slab,
               wn, wr, wm, wo, 512)
    y_prompt = ffn(x1, gf, wg, wu, wd, gl, 256).reshape(bp, t, d)

    wp0 = max(t - WINDOW, 0)
    stream = lambda i: kvt[:, i * gl_:(i + 1) * gl_, :]
    p_state = (_channel_major_rows(stream(0)), _channel_major_rows(stream(1)),
               _channel_major_rows(stream(2)), _channel_major_rows(stream(3)),
               _channel_major_rows(stream(4)[:, :, wp0:]), _channel_major_rows(stream(5)[:, :, wp0:]),
               slab3[:, t - 1, COL_PR:COL_PR + RWKV_PROJ][None],
               s_p[None],
               mkv[:, :, :MEM_WIDTH].reshape(1, bp, mem_n, MEM_HEADS, MEM_HEAD_DIM),
               mkv[:, :, MEM_WIDTH:].reshape(1, bp, mem_n, MEM_HEADS, MEM_HEAD_DIM))

    past_len = page_table.shape[1] * cache_cmp_k.shape[2]
    assert past_len % SLC_BLOCK == 0
    assert (past_len + tn) // CMP_STRIDE == past_len // CMP_STRIDE and tn <= TOK_PAD
    xs2 = x_sample.reshape(bs * tn, d)
    slab_s = norm_matmul(xs2, row2(attn_norm[0]), w_slab, bs * tn, 512)
    slab_s3 = slab_s.reshape(bs, tn, SLAB_COLS)
    kc_s, vc_s = compress_sample(_channel_major_view(cache_cmp_k[0]), _channel_major_view(cache_cmp_v[0]),
                                 page_table, cmp_wk, cmp_wv)
    o_nsa_s = nsa_sample(slab_s3, kc_s, vc_s, _channel_major_view(cache_win_k[0]), _channel_major_view(cache_win_v[0]),
                         _channel_major_view(cache_slc_k[0]), _channel_major_view(cache_slc_v[0]),
                         page_table, past_len)
    o_rwkv_s, s_s = rwkv(slab_s3, state_rwkv_shift[0][:, None, :], state_rwkv_wkv[0], rw_params, 16, F32)
    mem_s = cache_mem_k.shape[2]
    o_mem_s = mem_attend(slab_s3, cache_mem_k[0].reshape(bs, mem_s, MEM_WIDTH), 0,
                         cache_mem_v[0].reshape(bs, mem_s, MEM_WIDTH), 0, tn, F32,
                         nseq=8 if bs % 8 == 0 else 1)
    x1s = merge(xs2, o_nsa_s.reshape(bs * tn, NSA_WIDTH), o_rwkv_s.reshape(bs * tn, RWKV_WIDTH),
                o_mem_s.reshape(bs * tn, MEM_WIDTH), slab_s, wn, wr, wm, wo, min(512, bs * tn))
    y_sample = ffn(x1s, gf, wg, wu, wd, gl, min(256, bs * tn)).reshape(bs, tn, d)
    heads = lambda a: a.reshape(1, a.shape[0], a.shape[1], NSA_GROUPS, HEAD_DIM)
    kv_new = [slab_s3[:, :, COL_KV + i * gl_:COL_KV + (i + 1) * gl_] for i in range(6)]
    s_state = (heads(kv_new[0]), heads(kv_new[1]), heads(kv_new[2]), heads(kv_new[3]),
               jnp.concatenate([cache_win_k[0], heads(kv_new[4])[0]], axis=1)[:, tn:][None],
               jnp.concatenate([cache_win_v[0], heads(kv_new[5])[0]], axis=1)[:, tn:][None],
               slab_s3[:, tn - 1, COL_PR:COL_PR + RWKV_PROJ][None],
               s_s[None])
    return (y_prompt, y_sample) + p_state + s_state
```

```python
import functools

import numpy as np
import jax
import jax.numpy as jnp
from jax import lax
from jax.experimental import pallas as pl
from jax.experimental.pallas import tpu as pltpu

F32 = jnp.float32
BF16 = jnp.bfloat16

D_MODEL = 1024
HEAD_DIM = 64
NSA_WIDTH = 768
NSA_HEADS = 12
NSA_GROUPS = 2
NSA_HPG = 6
CMP_LEN = 32
CMP_STRIDE = 16
CMP_HID = 64
SLC_BLOCK = 64
N_SELECT = 16
WINDOW = 512
Q_BLOCK = 64
FORCE_SCORE = 1e4
RWKV_WIDTH = 768
RWKV_HEAD_DIM = 64
RWKV_HEADS = 12
DECAY_LORA = 64
ICL_LORA = 64
GATE_LORA = 128
RWKV_PROJ = 3 * RWKV_WIDTH + DECAY_LORA + ICL_LORA + GATE_LORA
GN_EPS = 64e-5
MEM_HEADS = 4
MEM_WIDTH = 512
MEM_HEAD_DIM = 128
N_BRANCHES = 3
NSA_KV_COLS = 3 * 2 * NSA_GROUPS * HEAD_DIM
GROUP_LANES = NSA_GROUPS * HEAD_DIM
RMS_EPS = 1e-6
NEG = -1e30
LOG2E = 1.4426950408889634

COL_MG = 0
COL_PR = 3072
COL_MQ = 5632
COL_Q = 6144
COL_GN = 6912
COL_KV = 7168
SLAB_COLS = 8192
MAX_PAGES_PER_STEP = 16


def _pages_per_step(n_pages):
    n = min(MAX_PAGES_PER_STEP, n_pages)
    assert n_pages % n == 0
    return n

VMEM_LIMIT = 56 * 1024 * 1024


def _dot(a, b):
    return jnp.dot(a, b, preferred_element_type=F32)


def _dot_nt(a, b):
    return lax.dot_general(a, b, (((1,), (1,)), ((), ())), preferred_element_type=F32)


def _dot_tn(a, b):
    return lax.dot_general(a, b, (((0,), (0,)), ((), ())), preferred_element_type=F32)


def _iota(shape, dim):
    return lax.broadcasted_iota(jnp.int32, shape, dim)


def _eye(n, dtype):
    return (_iota((n, n), 0) == _iota((n, n), 1)).astype(dtype)


def _sigmoid(x):
    return 1.0 / (1.0 + jnp.exp(-x))


def _tanh(x):
    t = jnp.exp(-2.0 * jnp.abs(x))
    r = (1.0 - t) / (1.0 + t)
    return jnp.where(x < 0.0, -r, r)


def _gelu_tanh(x):
    return 0.5 * x * (1.0 + jnp.tanh(np.sqrt(2.0 / np.pi).astype(np.float32) * (x + 0.044715 * (x * x * x))))


def _rms(x, g):
    ms = jnp.mean(x * x, axis=-1, keepdims=True)
    return (x * lax.rsqrt(ms + RMS_EPS)) * g


def _pad_rows(x, n):
    if x.shape[0] == n:
        return x
    return jnp.concatenate([x, jnp.zeros((n - x.shape[0],) + x.shape[1:], x.dtype)], axis=0)


def _norm_matmul_body(x_ref, g_ref, w_ref, o_ref, h_ref):
    @pl.when(pl.program_id(1) == 0)
    def _():
        h_ref[...] = _rms(x_ref[...], g_ref[...]).astype(BF16)

    o_ref[...] = _dot(h_ref[...], w_ref[...])


def norm_matmul(x, g, w, tm, tn):
    n, d = x.shape
    c = w.shape[1]
    return pl.pallas_call(
        _norm_matmul_body,
        grid=(n // tm, c // tn),
        in_specs=[pl.BlockSpec((tm, d), lambda i, j: (i, 0)),
                  pl.BlockSpec((1, d), lambda i, j: (0, 0)),
                  pl.BlockSpec((d, tn), lambda i, j: (0, j))],
        out_specs=pl.BlockSpec((tm, tn), lambda i, j: (i, j)),
        out_shape=jax.ShapeDtypeStruct((n, c), F32),
        scratch_shapes=[pltpu.VMEM((tm, d), BF16)],
        compiler_params=pltpu.CompilerParams(dimension_semantics=("parallel", "arbitrary"),
                                             vmem_limit_bytes=VMEM_LIMIT),
    )(x, g, w)


def _proj_prompt_body(x_ref, g_ref, w_ref, wkv_ref, wkvt_ref, o_ref, kvt_ref, ktb_ref, vb_ref, ck_ref, cv_ref, h_ref):
    @pl.when(pl.program_id(1) == 0)
    def _():
        h = _rms(x_ref[...], g_ref[...]).astype(BF16)
        h_ref[...] = h
        gl = GROUP_LANES
        kvt = _dot_nt(wkvt_ref[...], h)
        kvt_ref[0] = kvt
        ktb_ref[0] = jnp.concatenate([kvt[2 * gl:3 * gl], kvt[4 * gl:5 * gl]], axis=0).astype(BF16)
        kv = _dot(h, wkv_ref[...])
        ck_ref[...] = kv[:, 0:gl].astype(BF16)
        cv_ref[...] = kv[:, gl:2 * gl].astype(BF16)
        vb_ref[...] = jnp.concatenate([kv[:, 3 * gl:4 * gl], kv[:, 5 * gl:6 * gl]], axis=1).astype(BF16)

    o_ref[...] = _dot(h_ref[...], w_ref[...])


def proj_prompt(x, g, w, wkv, wkvt, b, t, tm, tn):
    n, d = x.shape
    c = w.shape[1]
    tpb = t // tm
    gl = GROUP_LANES
    full = lambda a: pl.BlockSpec(a.shape, lambda i, j: (0, 0))
    rows = lambda width: pl.BlockSpec((tm, width), lambda i, j: (i, 0))
    return pl.pallas_call(
        _proj_prompt_body,
        grid=(n // tm, c // tn),
        in_specs=[pl.BlockSpec((tm, d), lambda i, j: (i, 0)), full(g),
                  pl.BlockSpec((d, tn), lambda i, j: (0, j)), full(wkv), full(wkvt)],
        out_specs=[pl.BlockSpec((tm, tn), lambda i, j: (i, j)),
                   pl.BlockSpec((1, NSA_KV_COLS, tm), lambda i, j: (i // tpb, 0, i % tpb)),
                   pl.BlockSpec((1, 2 * gl, tm), lambda i, j: (i // tpb, 0, i % tpb)),
                   rows(2 * gl), rows(gl), rows(gl)],
        out_shape=[jax.ShapeDtypeStruct((n, c), F32),
                   jax.ShapeDtypeStruct((b, NSA_KV_COLS, t), F32),
                   jax.ShapeDtypeStruct((b, 2 * gl, t), BF16),
                   jax.ShapeDtypeStruct((n, 2 * gl), BF16),
                   jax.ShapeDtypeStruct((n, gl), BF16),
                   jax.ShapeDtypeStruct((n, gl), BF16)],
        scratch_shapes=[pltpu.VMEM((tm, d), BF16)],
        compiler_params=pltpu.CompilerParams(dimension_semantics=("parallel", "arbitrary"),
                                             vmem_limit_bytes=VMEM_LIMIT),
    )(x, g, w, wkv, wkvt)


def _compress_consts(pe_ref, b1_ref, w0_ref, w1_ref):
    pe0 = jnp.broadcast_to(pe_ref[0], (8, pe_ref.shape[2])).astype(BF16)
    pe1 = jnp.broadcast_to(pe_ref[1], (8, pe_ref.shape[2])).astype(BF16)
    c = _dot(pe0, w0_ref[...]) + _dot(pe1, w1_ref[...])
    return c[0:1] + b1_ref[...]


def _compress_finish(u0, u1, cst, w2):
    n = u0.shape[0]
    pre = u0 + pltpu.roll(u1, n - 1, 0) + cst
    out = _dot(_gelu_tanh(pre).astype(BF16), w2)
    return jnp.where(_iota(out.shape, 0) < n - 1, out, 0.0)


def _compress_prompt_body(xk_ref, xv_ref, pek_ref, b1k_ref, w0k_ref, w1k_ref, w2k_ref,
                          pev_ref, b1v_ref, w0v_ref, w1v_ref, w2v_ref, ok_ref, ov_ref):
    for x_ref, pe_ref, b1_ref, w0_ref, w1_ref, w2_ref, o_ref in (
            (xk_ref, pek_ref, b1k_ref, w0k_ref, w1k_ref, w2k_ref, ok_ref),
            (xv_ref, pev_ref, b1v_ref, w0v_ref, w1v_ref, w2v_ref, ov_ref)):
        x = x_ref[0].astype(BF16)
        cst = _compress_consts(pe_ref, b1_ref, w0_ref, w1_ref)
        res = _compress_finish(_dot(x, w0_ref[...]), _dot(x, w1_ref[...]), cst, w2_ref[...])
        for g in range(NSA_GROUPS):
            o_ref[0, g] = res[:, g * HEAD_DIM:(g + 1) * HEAD_DIM].astype(o_ref.dtype)


def _compress_weights(pe, w1, b1, w2):
    r = CMP_LEN // CMP_STRIDE
    eye = jnp.eye(NSA_GROUPS, dtype=F32)
    w1r = w1.reshape(r, CMP_STRIDE, HEAD_DIM, CMP_HID)
    w1e = jnp.einsum('icdh,gk->icgdkh', w1r, eye).reshape(r, CMP_STRIDE * NSA_GROUPS * HEAD_DIM,
                                                         NSA_GROUPS * CMP_HID)
    pee = jnp.broadcast_to(pe.reshape(r, CMP_STRIDE, 1, HEAD_DIM), (r, CMP_STRIDE, NSA_GROUPS, HEAD_DIM))
    pee = pee.reshape(r, 1, CMP_STRIDE * NSA_GROUPS * HEAD_DIM)
    b1e = jnp.tile(b1, NSA_GROUPS).reshape(1, NSA_GROUPS * CMP_HID)
    w2e = jnp.einsum('hd,gk->ghkd', w2, eye).reshape(NSA_GROUPS * CMP_HID, NSA_GROUPS * HEAD_DIM)
    return pee, b1e, w1e[0].astype(BF16), w1e[1].astype(BF16), w2e.astype(BF16)


def compress_prompt(xk, xv, wk, wv):
    b, nch, width = xk.shape
    full = lambda a: pl.BlockSpec(a.shape, lambda i: (0,) * a.ndim)
    xspec = pl.BlockSpec((1, nch, width), lambda i: (i, 0, 0))
    ospec = pl.BlockSpec((1, NSA_GROUPS, nch, HEAD_DIM), lambda i: (i, 0, 0, 0))
    oshape = jax.ShapeDtypeStruct((b, NSA_GROUPS, nch, HEAD_DIM), BF16)
    return pl.pallas_call(
        _compress_prompt_body,
        grid=(b,),
        in_specs=[xspec, xspec] + [full(a) for a in wk] + [full(a) for a in wv],
        out_specs=[ospec, ospec],
        out_shape=[oshape, oshape],
        compiler_params=pltpu.CompilerParams(dimension_semantics=("parallel",), vmem_limit_bytes=VMEM_LIMIT),
    )(xk, xv, *wk, *wv)


def _split3(x):
    hi = x.astype(BF16)
    r1 = x - hi.astype(F32)
    mid = r1.astype(BF16)
    lo = (r1 - mid.astype(F32)).astype(BF16)
    return hi, mid, lo


def _select_blocks_lanes(score, n_blocks):
    nq, nbp = score.shape
    cols = [score[:, jp:jp + 1] for jp in range(n_blocks)]
    cnts = []
    for c0 in range(0, nbp, 128):
        blk = score[:, c0:c0 + 128]
        jj = c0 + _iota(blk.shape, 1)
        cnt = jnp.zeros(blk.shape, F32)
        for jp in range(n_blocks):
            ge = jnp.where(cols[jp] >= blk, 1.0, 0.0)
            gt = jnp.where(cols[jp] > blk, 1.0, 0.0)
            if jp < c0:
                cnt = cnt + ge
            elif jp >= c0 + 128:
                cnt = cnt + gt
            else:
                cnt = cnt + jnp.where(jj > jp, ge, gt)
        cnts.append(cnt)
    cnt = jnp.concatenate(cnts, axis=1)
    return jnp.where((cnt < N_SELECT) & (score > -jnp.inf), 1.0, 0.0)


def _select_blocks_unrolled(score):
    nb = score.shape[0]
    rows = [score[jp:jp + 1, :] for jp in range(nb)]
    cnts = []
    for r in range(nb // 8):
        blk = score[8 * r:8 * r + 8]
        jj = 8 * r + _iota(blk.shape, 0)
        cnt = jnp.zeros(blk.shape, F32)
        for jp in range(nb):
            ge = jnp.where(rows[jp] >= blk, 1.0, 0.0)
            gt = jnp.where(rows[jp] > blk, 1.0, 0.0)
            if jp < 8 * r:
                cnt = cnt + ge
            elif jp >= 8 * r + 8:
                cnt = cnt + gt
            else:
                cnt = cnt + jnp.where(jj > jp, ge, gt)
        cnts.append(cnt)
    cnt = jnp.concatenate(cnts, axis=0)
    return jnp.where((cnt < N_SELECT) & (score > -jnp.inf), 1.0, 0.0)


def _masked_softmax_rows(s, valid):
    s = jnp.where(valid, s, NEG)
    m = jnp.max(s, axis=-1, keepdims=True)
    p = jnp.where(valid, jnp.exp2(s - m), 0.0)
    l = jnp.sum(p, axis=-1, keepdims=True)
    return p / jnp.where(l > 0.0, l, 1.0)


def _online_update(carry, s, valid, v, pv=_dot):
    m, l, acc = carry
    s = jnp.where(valid, s, NEG)
    m_new = jnp.maximum(m, jnp.max(s, axis=-1, keepdims=True))
    alpha = jnp.exp2(m - m_new)
    p = jnp.where(valid, jnp.exp2(s - m_new), 0.0)
    l = alpha * l + jnp.sum(p, axis=-1, keepdims=True)
    acc = alpha * acc + pv(p.astype(BF16), v)
    return m_new, l, acc


def _online_update_biased(carry, s, v):
    m, l, acc = carry
    m_new = jnp.maximum(m, jnp.max(s, axis=-1, keepdims=True))
    alpha = jnp.exp2(m - m_new)
    p = jnp.exp2(s - m_new)
    l = alpha * l + jnp.sum(p, axis=-1, keepdims=True)
    acc = alpha * acc + _dot(p.astype(BF16), v)
    return m_new, l, acc


KV_TILE = 1024
WIN_TILE = 640


def _nsa_prompt_body(q_ref, gn_ref, kc_ref, vc_ref, kts_ref, vs_ref, ktw_ref, vw_ref, ovt_ref, o_ref):
    g = pl.program_id(1)
    qb = pl.program_id(2)
    nq = Q_BLOCK
    hpg = NSA_HPG
    rows = hpg * nq
    qf = q_ref[...]
    q2f = jnp.concatenate([qf[:, h * HEAD_DIM:(h + 1) * HEAD_DIM] for h in range(hpg)], axis=0)
    q2 = q2f.astype(BF16)
    zero = jnp.zeros_like(q2f)
    q2w = jnp.where(g == 0, jnp.concatenate([q2f, zero], axis=1), jnp.concatenate([zero, q2f], axis=1)).astype(BF16)
    t_q1 = qb * nq + _iota((nq, 1), 0)
    tile6 = lambda x: jnp.concatenate([x] * hpg, axis=0)
    pick = lambda x: jnp.where(g == 0, x[:, 0:HEAD_DIM], x[:, HEAD_DIM:2 * HEAD_DIM])

    kc = kc_ref[0, 0]
    ncp = kc.shape[0]
    c_end = _iota((nq, ncp), 1) * CMP_STRIDE + (CMP_LEN - 1)
    s_c = _dot_nt(q2, kc) + tile6(jnp.where(c_end <= t_q1, 0.0, NEG))
    e_c = jnp.exp2(s_c - jnp.max(s_c, axis=-1, keepdims=True))
    l_c = jnp.sum(e_c, axis=-1, keepdims=True)
    any_c = tile6(t_q1 >= CMP_LEN - 1)
    p_c = e_c * jnp.where(any_c, 1.0 / l_c, 0.0)
    o_c = _dot(p_c.astype(BF16), vc_ref[0, 0])
    psum = p_c[0:nq]
    for h in range(1, hpg):
        psum = psum + p_c[h * nq:(h + 1) * nq]

    imp_t = sum(_dot_nt(ovt_ref[...], part) for part in _split3(psum))
    j_idx = _iota(imp_t.shape, 0)
    forced = (j_idx == 0) | (j_idx == qb) | (j_idx == qb - 1)
    score = jnp.where(forced, FORCE_SCORE, imp_t)
    sel_t = _select_blocks_unrolled(jnp.where(j_idx <= qb, score, -jnp.inf))
    sel = _dot_nt(_eye(nq, BF16), sel_t.astype(BF16)).astype(BF16)

    w0 = pl.multiple_of(jnp.maximum(qb * nq - WINDOW, 0) // 128 * 128, 128)
    diff = t_q1 - (w0 + _iota((nq, WIN_TILE), 1))
    ok_w = (diff >= 0) & (diff <= WINDOW)
    s_w = _dot(q2w, ktw_ref[0, :, pl.ds(w0, WIN_TILE)]) + tile6(jnp.where(ok_w, 0.0, NEG))
    e_w = jnp.exp2(s_w - jnp.max(s_w, axis=-1, keepdims=True))
    l_w = jnp.sum(e_w, axis=-1, keepdims=True)
    o_w = pick(_dot(e_w.astype(BF16), vw_ref[0, pl.ds(w0, WIN_TILE), :])) / l_w

    bpt = KV_TILE // SLC_BLOCK
    col_blk = _iota((sel.shape[1], KV_TILE), 1) // SLC_BLOCK
    row_blk = _iota((sel.shape[1], KV_TILE), 0)

    def block_mask(kt):
        expand = jnp.where(row_blk == col_blk + kt * bpt, 1.0, 0.0).astype(BF16)
        return _dot(sel, expand) > 0.5

    def scores(kt):
        off = pl.multiple_of(kt * KV_TILE, KV_TILE)
        return _dot(q2w, kts_ref[0, :, pl.ds(off, KV_TILE)])

    def values(kt):
        off = pl.multiple_of(kt * KV_TILE, KV_TILE)
        return vs_ref[0, pl.ds(off, KV_TILE), :]

    nt = qb // bpt
    ok_d = block_mask(nt) & (_iota((nq, KV_TILE), 1) + nt * KV_TILE <= t_q1)
    init = (jnp.full((rows, 1), NEG, F32), jnp.zeros((rows, 1), F32), jnp.zeros((rows, GROUP_LANES), F32))
    carry = _online_update_biased(init, scores(nt) + tile6(jnp.where(ok_d, 0.0, NEG)), values(nt))

    def tile_step(kt, carry):
        return _online_update_biased(carry, scores(kt) + tile6(jnp.where(block_mask(kt), 0.0, NEG)), values(kt))

    _, l_s, acc_s = lax.fori_loop(0, nt, tile_step, carry)
    o_s = pick(acc_s) / l_s

    gates = _sigmoid(gn_ref[...])
    per_group = hpg * 3
    gates = jnp.where(g == 0, gates[:, 0:per_group], gates[:, per_group:2 * per_group])
    outs = []
    for h in range(hpg):
        sl = slice(h * nq, (h + 1) * nq)
        outs.append(gates[:, 3 * h:3 * h + 1] * o_c[sl] + gates[:, 3 * h + 1:3 * h + 2] * o_s[sl]
                    + gates[:, 3 * h + 2:3 * h + 3] * o_w[sl])
    o_ref[...] = jnp.concatenate(outs, axis=1).astype(o_ref.dtype)


def _overlap_t(n_blocks, n_cmp_padded, n_cmp):
    i = np.arange(n_cmp_padded)[None, :] * CMP_STRIDE
    j = np.arange(n_blocks)[:, None] * SLC_BLOCK
    ov = (i < j + SLC_BLOCK) & (i + CMP_LEN > j) & (np.arange(n_cmp_padded)[None, :] < n_cmp)
    return jnp.asarray(ov.astype(np.float32)).astype(BF16)


def nsa_prompt(slab, kc, vc, ktb, vb, b, t):
    nb = t // Q_BLOCK
    gw = NSA_HPG * HEAD_DIM
    gl = GROUP_LANES
    ncp = kc.shape[2]
    ovt = _overlap_t(nb, ncp, ncp - 1)
    kt_spec = lambda k: pl.BlockSpec((1, gl, t), lambda bi, g, qb: (bi, k, 0))
    v_spec = lambda k: pl.BlockSpec((1, t, gl), lambda bi, g, qb: (bi, 0, k))
    cmp_spec = pl.BlockSpec((1, 1, ncp, HEAD_DIM), lambda bi, g, qb: (bi, g, 0, 0))
    return pl.pallas_call(
        _nsa_prompt_body,
        grid=(b, NSA_GROUPS, nb),
        in_specs=[pl.BlockSpec((Q_BLOCK, gw), lambda bi, g, qb: (bi * nb + qb, COL_Q // gw + g)),
                  pl.BlockSpec((Q_BLOCK, 128), lambda bi, g, qb: (bi * nb + qb, COL_GN // 128)),
                  cmp_spec, cmp_spec, kt_spec(0), v_spec(0), kt_spec(1), v_spec(1),
                  pl.BlockSpec(ovt.shape, lambda bi, g, qb: (0, 0))],
        out_specs=pl.BlockSpec((Q_BLOCK, gw), lambda bi, g, qb: (bi * nb + qb, g)),
        out_shape=jax.ShapeDtypeStruct((b * t, NSA_WIDTH), BF16),
        compiler_params=pltpu.CompilerParams(dimension_semantics=("parallel", "parallel", "arbitrary"),
                                             vmem_limit_bytes=VMEM_LIMIT),
    )(slab, slab, kc, vc, ktb, vb, ktb, vb, ovt)


def _compress_sample_body(n, pt_ref, *refs):
    k_pages, v_pages = refs[0:n], refs[n:2 * n]
    (pek_ref, b1k_ref, w0k_ref, w1k_ref, w2k_ref, wck_ref,
     pev_ref, b1v_ref, w0v_ref, w1v_ref, w2v_ref, wcv_ref, ok_ref, ov_ref, uk_ref, uv_ref) = refs[2 * n:]
    j = pl.program_id(1)
    page_rows = k_pages[0].shape[2]
    cpp = page_rows // CMP_STRIDE
    rows = n * cpp
    off = pl.multiple_of(j * rows, rows)
    rp = _iota((page_rows, page_rows), 0)
    perm = (_iota((page_rows, page_rows), 1) == CMP_STRIDE * (rp % cpp) + rp // cpp).astype(BF16)
    for pages, wc_ref, u_ref in ((k_pages, wck_ref, uk_ref), (v_pages, wcv_ref, uv_ref)):
        xp = [_dot_nt(perm, r[0].astype(BF16)) for r in pages]
        x = jnp.concatenate([jnp.concatenate([p[c * cpp:(c + 1) * cpp] for p in xp], axis=0)
                             for c in range(CMP_STRIDE)], axis=1)
        u_ref[pl.ds(off, rows), :] = _dot(x.astype(BF16), wc_ref[...])

    @pl.when(j == pl.num_programs(1) - 1)
    def _():
        half = NSA_GROUPS * CMP_HID
        for pe_ref, b1_ref, w0_ref, w1_ref, w2_ref, u_ref, o_ref in (
                (pek_ref, b1k_ref, w0k_ref, w1k_ref, w2k_ref, uk_ref, ok_ref),
                (pev_ref, b1v_ref, w0v_ref, w1v_ref, w2v_ref, uv_ref, ov_ref)):
            cst = _compress_consts(pe_ref, b1_ref, w0_ref, w1_ref)
            u = u_ref[...]
            res = _compress_finish(u[:, 0:half], u[:, half:2 * half], cst, w2_ref[...])
            for g in range(NSA_GROUPS):
                o_ref[0, g] = res[:, g * HEAD_DIM:(g + 1) * HEAD_DIM].astype(o_ref.dtype)


def compress_sample(pool_k, pool_v, page_table, wk, wv):
    bs, n_pages = page_table.shape
    _, lanes, page_rows = pool_k.shape
    n = _pages_per_step(n_pages)
    nch = n_pages * page_rows // CMP_STRIDE
    wck = jnp.concatenate([wk[2], wk[3]], axis=1)
    wcv = jnp.concatenate([wv[2], wv[3]], axis=1)
    page = lambda k: pl.BlockSpec((1, lanes, page_rows), lambda b, j, pt: (pt[b, n * j + k], 0, 0))
    full = lambda a: pl.BlockSpec(a.shape, lambda b, j, pt: (0,) * a.ndim)
    consts = list(wk) + [wck] + list(wv) + [wcv]
    ospec = pl.BlockSpec((1, NSA_GROUPS, nch, HEAD_DIM), lambda b, j, pt: (b, 0, 0, 0))
    oshape = jax.ShapeDtypeStruct((bs, NSA_GROUPS, nch, HEAD_DIM), BF16)
    return pl.pallas_call(
        functools.partial(_compress_sample_body, n),
        grid_spec=pltpu.PrefetchScalarGridSpec(
            num_scalar_prefetch=1,
            grid=(bs, n_pages // n),
            in_specs=[page(k) for k in range(n)] * 2 + [full(a) for a in consts],
            out_specs=[ospec, ospec],
            scratch_shapes=[pltpu.VMEM((nch, 2 * NSA_GROUPS * CMP_HID), F32)] * 2),
        out_shape=[oshape, oshape],
        compiler_params=pltpu.CompilerParams(dimension_semantics=("parallel", "arbitrary"),
                                             vmem_limit_bytes=VMEM_LIMIT),
    )(page_table, *([pool_k] * n), *([pool_v] * n), *consts)


TOK_PAD = 8


def _nsa_sample_body(past_len, tn, n, pt_ref, *refs):
    q_ref, gn_ref, skn_ref, svn_ref, wkn_ref, wvn_ref, kc_ref, vc_ref, wkc_ref, wvc_ref = refs[0:10]
    k_pages, v_pages = refs[10:10 + n], refs[10 + n:10 + 2 * n]
    ov_ref, o_ref, selt_ref, m_ref, l_ref, acc_ref, oc_ref, ow_ref = refs[10 + 2 * n:]
    j = pl.program_id(1)
    tp = TOK_PAD
    hpg = NSA_HPG
    grows = hpg * tp
    rows = NSA_GROUPS * grows
    gw = hpg * HEAD_DIM
    lanes = GROUP_LANES
    nsp = selt_ref.shape[0]

    q8 = _pad_rows(q_ref[0], tp)
    zero = jnp.zeros((grows, HEAD_DIM), F32)
    q2, q_parts = [], []
    for g in range(NSA_GROUPS):
        qg = jnp.concatenate([q8[:, g * gw + h * HEAD_DIM:g * gw + (h + 1) * HEAD_DIM] for h in range(hpg)], axis=0)
        q2.append(qg.astype(BF16))
        q_parts.append(jnp.concatenate([qg, zero] if g == 0 else [zero, qg], axis=1))
    q_all = jnp.concatenate(q_parts, axis=0).astype(BF16)
    tok = _iota((rows, 1), 0) % tp
    t_q = past_len + tok

    def stack_groups(x):
        return jnp.concatenate([x[g * tp:(g + 1) * tp] for g in range(NSA_GROUPS) for _ in range(hpg)], axis=0)

    def new_keys_valid(width):
        tk = _iota((rows, width), 1)
        return (tk <= tok) & (tk < tn)

    @pl.when(j == 0)
    def _():
        psums = []
        tq_g = t_q[0:grows]
        for g in range(NSA_GROUPS):
            s_c = _dot_nt(q2[g], kc_ref[0, g])
            c_end = _iota(s_c.shape, 1) * CMP_STRIDE + (CMP_LEN - 1)
            p_c = _masked_softmax_rows(s_c, c_end <= tq_g)
            oc_ref[g] = _dot(p_c.astype(BF16), vc_ref[0, g])
            ps = p_c[0:tp]
            for h in range(1, hpg):
                ps = ps + p_c[h * tp:(h + 1) * tp]
            psums.append(ps)
        psum = jnp.concatenate(psums, axis=0)
        imp = sum(_dot(part, ov_ref[...]) for part in _split3(psum))
        j_idx = _iota(imp.shape, 1)
        cur = (past_len + _iota(imp.shape, 0) % tp) // SLC_BLOCK
        forced = (j_idx == 0) | (j_idx == cur) | (j_idx == cur - 1)
        score = jnp.where(j_idx <= cur, jnp.where(forced, FORCE_SCORE, imp), -jnp.inf)
        sel = _select_blocks_lanes(score, (past_len + tn - 1) // SLC_BLOCK + 1)
        pick_row = (_iota((nsp, sel.shape[1]), 0) == _iota((nsp, sel.shape[1]), 1)).astype(BF16)
        selt_ref[...] = _dot_nt(pick_row, sel.astype(BF16))

        lw = wkc_ref.shape[2]
        kwn = _pad_rows(wkn_ref[0], 16).astype(BF16)
        vwn = _pad_rows(wvn_ref[0], 16).astype(BF16)
        diff = t_q - (past_len - lw + _iota((rows, lw), 1))
        carry = (jnp.full((rows, 1), NEG, F32), jnp.zeros((rows, 1), F32), jnp.zeros((rows, lanes), F32))
        carry = _online_update(carry, _dot(q_all, wkc_ref[0].astype(BF16)), (diff >= 0) & (diff <= WINDOW),
                               wvc_ref[0].astype(BF16), pv=_dot_nt)
        _, l_w, acc_w = _online_update(carry, _dot_nt(q_all, kwn), new_keys_valid(16), vwn)
        ow_ref[...] = acc_w / l_w
        m_ref[...] = jnp.full((rows, 1), NEG, F32)
        l_ref[...] = jnp.zeros((rows, 1), F32)
        acc_ref[...] = jnp.zeros((rows, lanes), F32)

    page_rows = k_pages[0].shape[2]
    nk = n * page_rows
    bps = nk // SLC_BLOCK
    kt = jnp.concatenate([r[0] for r in k_pages], axis=1).astype(BF16)
    vt = jnp.concatenate([r[0] for r in v_pages], axis=1).astype(BF16)
    expand = jnp.where(_iota((bps, nk), 0) == _iota((bps, nk), 1) // SLC_BLOCK, 1.0, 0.0).astype(BF16)
    sel_rows = selt_ref[pl.ds(pl.multiple_of(j * bps, bps), bps), :].astype(BF16)
    bias = stack_groups(jnp.where(_dot_tn(sel_rows, expand) > 0.5, 0.0, NEG))
    m, l, acc = (m_ref[...], l_ref[...], acc_ref[...])
    s = _dot(q_all, kt) + bias
    m_new = jnp.maximum(m, jnp.max(s, axis=-1, keepdims=True))
    alpha = jnp.exp2(m - m_new)
    p = jnp.exp2(s - m_new)
    m_ref[...] = m_new
    l_ref[...] = alpha * l + jnp.sum(p, axis=-1, keepdims=True)
    acc_ref[...] = alpha * acc + _dot_nt(p.astype(BF16), vt)

    @pl.when(j == pl.num_programs(1) - 1)
    def _():
        kn = _pad_rows(skn_ref[0], 16).astype(BF16)
        vn = _pad_rows(svn_ref[0], 16).astype(BF16)
        expand_n = jnp.where(_iota((nsp, 16), 0) == (past_len + _iota((nsp, 16), 1)) // SLC_BLOCK,
                             1.0, 0.0).astype(BF16)
        sel_n = stack_groups(_dot_tn(selt_ref[...].astype(BF16), expand_n)) > 0.5
        _, l_s, acc_s = _online_update((m_ref[...], l_ref[...], acc_ref[...]), _dot_nt(q_all, kn),
                                       sel_n & new_keys_valid(16), vn)
        o_s_all = acc_s / l_s
        o_w_all = ow_ref[...]
        gates = _sigmoid(_pad_rows(gn_ref[0], tp))
        outs = []
        for g in range(NSA_GROUPS):
            gsl = slice(g * HEAD_DIM, (g + 1) * HEAD_DIM)
            o_c = oc_ref[g]
            for h in range(hpg):
                sl = slice(h * tp, (h + 1) * tp)
                asl = slice(g * grows + h * tp, g * grows + (h + 1) * tp)
                c0 = (g * hpg + h) * 3
                outs.append(gates[:, c0:c0 + 1] * o_c[sl] + gates[:, c0 + 1:c0 + 2] * o_s_all[asl, gsl]
                            + gates[:, c0 + 2:c0 + 3] * o_w_all[asl, gsl])
        o_ref[0] = jnp.concatenate(outs, axis=1)[0:tn].astype(o_ref.dtype)


def nsa_sample(slab3, kc, vc, win_kt, win_vt, pool_kt, pool_vt, page_table, past_len):
    bs, tn, _ = slab3.shape
    n_pages = page_table.shape[1]
    n = _pages_per_step(n_pages)
    page_rows = pool_kt.shape[2]
    lanes = GROUP_LANES
    ncp = kc.shape[2]
    ns = -(-(past_len + tn) // SLC_BLOCK)
    nsp = -(-ns // 8) * 8
    ov = _overlap_t(-(-ns // 128) * 128, ncp, (past_len + tn) // CMP_STRIDE - CMP_LEN // CMP_STRIDE + 1).T
    rows = NSA_GROUPS * NSA_HPG * TOK_PAD
    tokblk = lambda width, col: pl.BlockSpec((1, tn, width), lambda b, j, pt: (b, 0, col // width))
    cmp_spec = pl.BlockSpec((1, NSA_GROUPS, ncp, HEAD_DIM), lambda b, j, pt: (b, 0, 0, 0))
    win_spec = pl.BlockSpec((1, lanes, win_kt.shape[2]), lambda b, j, pt: (b, 0, 0))
    page = lambda k: pl.BlockSpec((1, lanes, page_rows), lambda b, j, pt: (pt[b, n * j + k], 0, 0))
    return pl.pallas_call(
        functools.partial(_nsa_sample_body, past_len, tn, n),
        grid_spec=pltpu.PrefetchScalarGridSpec(
            num_scalar_prefetch=1,
            grid=(bs, n_pages // n),
            in_specs=[tokblk(NSA_WIDTH, COL_Q), tokblk(128, COL_GN),
                      tokblk(lanes, COL_KV + 2 * lanes), tokblk(lanes, COL_KV + 3 * lanes),
                      tokblk(lanes, COL_KV + 4 * lanes), tokblk(lanes, COL_KV + 5 * lanes),
                      cmp_spec, cmp_spec, win_spec, win_spec]
            + [page(k) for k in range(n)] * 2
            + [pl.BlockSpec(ov.shape, lambda b, j, pt: (0, 0))],
            out_specs=pl.BlockSpec((1, tn, NSA_WIDTH), lambda b, j, pt: (b, 0, 0)),
            scratch_shapes=[pltpu.VMEM((nsp, NSA_GROUPS * TOK_PAD), F32),
                            pltpu.VMEM((rows, 1), F32),
                            pltpu.VMEM((rows, 1), F32),
                            pltpu.VMEM((rows, lanes), F32),
                            pltpu.VMEM((NSA_GROUPS, NSA_HPG * TOK_PAD, HEAD_DIM), F32),
                            pltpu.VMEM((rows, lanes), F32)]),
        out_shape=jax.ShapeDtypeStruct((bs, tn, NSA_WIDTH), F32),
        compiler_params=pltpu.CompilerParams(dimension_semantics=("parallel", "arbitrary"),
                                             vmem_limit_bytes=VMEM_LIMIT),
    )(page_table, slab3, slab3, slab3, slab3, slab3, slab3, kc, vc, win_kt, win_vt,
      *([pool_kt] * n), *([pool_vt] * n), ov)


def _cumsum_rows(x):
    n = x.shape[0]
    row = _iota((n, 1), 0)
    k = 1
    while k < n:
        x = x + jnp.where(row >= k, pltpu.roll(x, k, 0), 0.0)
        k *= 2
    return x


def _rwkv_body(n_valid, chunk, p0_ref, p1_ref, p2_ref, p3_ref, p4_ref, prev_ref, s0_ref,
               mu_ref, w0_ref, w2_ref, a0_ref, a2_ref, g2_ref, kk_ref, ka_ref, rk_ref, lng_ref, lnb_ref,
               o_ref, sout_ref, carry_ref, s_ref):
    c = pl.program_id(1)
    hd = RWKV_HEAD_DIM

    @pl.when(c == 0)
    def _():
        carry_ref[...] = jnp.broadcast_to(prev_ref[0], carry_ref.shape)
        s_ref[...] = s0_ref[0]

    p = jnp.concatenate([r[0] for r in (p0_ref, p1_ref, p2_ref, p3_ref, p4_ref)], axis=1)
    p = _pad_rows(p, chunk)
    row = _iota((chunk, 1), 0)
    valid = row < n_valid
    prev = jnp.where(row == 0, carry_ref[0:1, :], pltpu.roll(p, 1, 0))
    xm = p + (prev - p) * mu_ref[...]
    carry_ref[...] = jnp.broadcast_to(p[n_valid - 1:n_valid, :], carry_ref.shape)

    wdt = RWKV_WIDTH
    r_all, k_all, v_all = xm[:, 0:wdt], xm[:, wdt:2 * wdt], xm[:, 2 * wdt:3 * wdt]
    o = 3 * wdt
    wd, ad, gd = xm[:, o:o + DECAY_LORA], xm[:, o + DECAY_LORA:o + DECAY_LORA + ICL_LORA], \
        xm[:, o + DECAY_LORA + ICL_LORA:o + DECAY_LORA + ICL_LORA + GATE_LORA]
    w = w0_ref[...] + _dot(_tanh(wd).astype(BF16), w2_ref[...])
    logw = -_sigmoid(w) * float(np.exp(-0.5))
    a_all = _sigmoid(a0_ref[...] + _dot(ad.astype(BF16), a2_ref[...]))
    g_all = _dot(_sigmoid(gd).astype(BF16), g2_ref[...])
    logw = jnp.where(valid, logw, 0.0)
    cum = _cumsum_rows(logw)
    total = cum[chunk - 1:chunk, :]
    w_in = jnp.exp(cum)
    w_ex = jnp.exp(cum - logw)
    w_inv = jnp.exp(-cum)
    w_rem = jnp.exp(total - cum)
    w_tot = jnp.exp(total)
    kk_all = k_all * kk_ref[...]
    k2_all = k_all * (1.0 + (a_all - 1.0) * ka_ref[...])

    t_i = _iota((chunk, chunk), 0)
    s_i = _iota((chunk, chunk), 1)
    strict = s_i < t_i
    incl = s_i <= t_i
    n_rounds = int(np.log2(chunk))
    heads = range(RWKV_HEADS)
    sls = [slice(h * hd, (h + 1) * hd) for h in heads]

    lr, bt, kt, bk, vb, at, rt = [], [], [], [], [], [], []
    head_of = (_iota((wdt, 128), 0) // hd == _iota((wdt, 128), 1)).astype(BF16)
    ssq = sum(_dot(part, head_of) for part in _split3(kk_all * kk_all))
    inv = sum(_dot_nt(part, head_of) for part in _split3(lax.rsqrt(jnp.maximum(ssq, 1e-24))))
    kkn_all = jnp.where(valid, kk_all * inv, 0.0)
    k2m_all = jnp.where(valid, k2_all, 0.0)
    vm_all = jnp.where(valid, v_all, 0.0)
    b_all = kkn_all * a_all
    at_all = -kkn_all * w_ex
    rt_all = r_all * w_in
    bt_all = b_all * w_inv
    kt_all = k2m_all * w_inv
    bp_all = b_all * w_rem
    kp_all = k2m_all * w_rem
    for sl in sls:
        at.append(at_all[:, sl].astype(BF16))
        rt.append(rt_all[:, sl].astype(BF16))
        lr.append(jnp.concatenate([at_all[:, sl], rt_all[:, sl]], axis=0).astype(BF16))
        bt.append(bt_all[:, sl].astype(BF16))
        kt.append(kt_all[:, sl].astype(BF16))
        bk.append(jnp.concatenate([bp_all[:, sl], kp_all[:, sl]], axis=0).astype(BF16))
        vb.append(vm_all[:, sl])
    m_b = [_dot_nt(lr[h], bt[h]) for h in heads]
    m_k = [_dot_nt(lr[h], kt[h]) for h in heads]
    a_ab = [jnp.where(strict, m[0:chunk], 0.0) for m in m_b]
    a_rb = [jnp.where(incl, m[chunk:2 * chunk], 0.0).astype(BF16) for m in m_b]
    a_ak = [jnp.where(strict, m[0:chunk], 0.0).astype(BF16) for m in m_k]
    a_rk = [jnp.where(incl, m[chunk:2 * chunk], 0.0).astype(BF16) for m in m_k]
    s0 = [s_ref[h] for h in heads]
    s0b = [x.astype(BF16) for x in s0]
    vbb = [x.astype(BF16) for x in vb]
    u = [_dot_nt(at[h], s0b[h]) + _dot(a_ak[h], vbb[h]) for h in heads]
    pw = a_ab
    for it in range(n_rounds):
        pwb = [x.astype(BF16) for x in pw]
        u = [u[h] + _dot(pwb[h], u[h].astype(BF16)) for h in heads]
        if it + 1 < n_rounds:
            pw = [_dot(x, x) for x in pwb]
    ub = [x.astype(BF16) for x in u]
    y = [_dot_nt(rt[h], s0b[h]) + _dot(a_rb[h], ub[h]) + _dot(a_rk[h], vbb[h]) for h in heads]
    for h in heads:
        uv = jnp.concatenate([u[h], vb[h]], axis=0).astype(BF16)
        s_ref[h] = s0[h] * w_tot[:, sls[h]] + _dot_tn(uv, bk[h])
    outs = []
    for h in heads:
        sl = sls[h]
        mean = jnp.mean(y[h], axis=-1, keepdims=True)
        yc = y[h] - mean
        var = jnp.mean(yc * yc, axis=-1, keepdims=True)
        yn = yc * lax.rsqrt(var + GN_EPS)
        bonus = jnp.sum(r_all[:, sl] * k2_all[:, sl] * rk_ref[:, sl], axis=-1, keepdims=True) * v_all[:, sl]
        outs.append((yn * lng_ref[:, sl] + lnb_ref[:, sl] + bonus) * g_all[:, sl])
    out = jnp.concatenate(outs, axis=1)
    o_ref[0] = out[0:o_ref.shape[1]].astype(o_ref.dtype)

    @pl.when(c == pl.num_programs(1) - 1)
    def _():
        sout_ref[0] = s_ref[...]


def rwkv(slab3, p_prev, s0, params, chunk, out_dtype):
    b, t, _ = slab3.shape
    tc = min(t, chunk)
    nchunks = t // tc
    blk = 512
    pspec = lambda k: pl.BlockSpec((1, tc, blk), lambda bi, c: (bi, c, COL_PR // blk + k))
    full = lambda a: pl.BlockSpec(a.shape, lambda bi, c: (0,) * a.ndim)
    sspec = pl.BlockSpec((1, RWKV_HEADS, RWKV_HEAD_DIM, RWKV_HEAD_DIM), lambda bi, c: (bi, 0, 0, 0))
    return pl.pallas_call(
        functools.partial(_rwkv_body, tc, chunk),
        grid=(b, nchunks),
        in_specs=[pspec(k) for k in range(5)]
        + [pl.BlockSpec((1, 1, RWKV_PROJ), lambda bi, c: (bi, 0, 0)), sspec]
        + [full(a) for a in params],
        out_specs=[pl.BlockSpec((1, tc, RWKV_WIDTH), lambda bi, c: (bi, c, 0)), sspec],
        out_shape=[jax.ShapeDtypeStruct((b, t, RWKV_WIDTH), out_dtype),
                   jax.ShapeDtypeStruct(s0.shape, F32)],
        scratch_shapes=[pltpu.VMEM((8, RWKV_PROJ), F32),
                        pltpu.VMEM((RWKV_HEADS, RWKV_HEAD_DIM, RWKV_HEAD_DIM), F32)],
        compiler_params=pltpu.CompilerParams(dimension_semantics=("parallel", "arbitrary"),
                                             vmem_limit_bytes=VMEM_LIMIT),
    )(slab3, slab3, slab3, slab3, slab3, p_prev, s0, *params)


def _mem_attend_body(q_ref, k_ref, v_ref, o_ref):
    nseq, tm = q_ref.shape[0], q_ref.shape[1]
    for bb in range(nseq):
        q = _pad_rows(q_ref[bb], max(tm, 16)).astype(BF16)
        k = k_ref[bb].astype(BF16)
        v = v_ref[bb].astype(BF16)
        outs = []
        for h in range(MEM_HEADS):
            sl = slice(h * MEM_HEAD_DIM, (h + 1) * MEM_HEAD_DIM)
            s = _dot_nt(q[:, sl], k[:, sl]) * (MEM_HEAD_DIM ** -0.5)
            m = jnp.max(s, axis=-1, keepdims=True)
            p = jnp.exp(s - m)
            p = p / jnp.sum(p, axis=-1, keepdims=True)
            outs.append(_dot(p.astype(BF16), v[:, sl]))
        o_ref[bb] = jnp.concatenate(outs, axis=1)[0:tm].astype(o_ref.dtype)


def mem_attend(slab3, mk, k_blk, mv, v_blk, tm, out_dtype, nseq=1):
    b, t, _ = slab3.shape
    m = mk.shape[1]
    return pl.pallas_call(
        _mem_attend_body,
        grid=(b // nseq, t // tm),
        in_specs=[pl.BlockSpec((nseq, tm, MEM_WIDTH), lambda bi, i: (bi, i, COL_MQ // MEM_WIDTH)),
                  pl.BlockSpec((nseq, m, MEM_WIDTH), lambda bi, i: (bi, 0, k_blk)),
                  pl.BlockSpec((nseq, m, MEM_WIDTH), lambda bi, i: (bi, 0, v_blk))],
        out_specs=pl.BlockSpec((nseq, tm, MEM_WIDTH), lambda bi, i: (bi, i, 0)),
        out_shape=jax.ShapeDtypeStruct((b, t, MEM_WIDTH), out_dtype),
        compiler_params=pltpu.CompilerParams(dimension_semantics=("parallel", "parallel"),
                                             vmem_limit_bytes=VMEM_LIMIT),
    )(slab3, mk, mv)


def _merge_body(x_ref, on_ref, or_ref, om_ref, g0_ref, g1_ref, g2_ref, wn_ref, wr_ref, wm_ref, wo_ref, o_ref):
    m = _sigmoid(g0_ref[...]) * _dot(on_ref[...].astype(BF16), wn_ref[...])
    m = m + _sigmoid(g1_ref[...]) * _dot(or_ref[...].astype(BF16), wr_ref[...])
    m = m + _sigmoid(g2_ref[...]) * _dot(om_ref[...].astype(BF16), wm_ref[...])
    o_ref[...] = x_ref[...] + _dot(m.astype(BF16), wo_ref[...])


def merge(x, o_nsa, o_rwkv, o_mem, slab, wn, wr, wm, wo, tm):
    n, d = x.shape
    row = lambda w: pl.BlockSpec((tm, w), lambda i: (i, 0))
    full = lambda a: pl.BlockSpec(a.shape, lambda i: (0, 0))
    gate = lambda k: pl.BlockSpec((tm, d), lambda i: (i, COL_MG // d + k))
    return pl.pallas_call(
        _merge_body,
        grid=(n // tm,),
        in_specs=[row(d), row(NSA_WIDTH), row(RWKV_WIDTH), row(MEM_WIDTH), gate(0), gate(1), gate(2),
                  full(wn), full(wr), full(wm), full(wo)],
        out_specs=row(d),
        out_shape=jax.ShapeDtypeStruct((n, d), F32),
        compiler_params=pltpu.CompilerParams(dimension_semantics=("parallel",), vmem_limit_bytes=VMEM_LIMIT),
    )(x, o_nsa, o_rwkv, o_mem, slab, slab, slab, wn, wr, wm, wo)


def _ffn_body(x_ref, gf_ref, wg_ref, wu_ref, wd_ref, gl_ref, o_ref):
    x = x_ref[...]
    hf = _rms(x, gf_ref[...]).astype(BF16)
    gate = _dot(hf, wg_ref[...])
    up = _dot(hf, wu_ref[...])
    act = (gate * _sigmoid(gate) * up).astype(BF16)
    x2 = x + _dot(act, wd_ref[...])
    o_ref[...] = _rms(x2, gl_ref[...])


def ffn(x, gf, wg, wu, wd, gl, tm):
    n, d = x.shape
    row = pl.BlockSpec((tm, d), lambda i: (i, 0))
    full = lambda a: pl.BlockSpec(a.shape, lambda i: (0, 0), pipeline_mode=pl.Buffered(1))
    return pl.pallas_call(
        _ffn_body,
        grid=(n // tm,),
        in_specs=[row, full(gf), full(wg), full(wu), full(wd), full(gl)],
        out_specs=row,
        out_shape=jax.ShapeDtypeStruct((n, d), F32),
        compiler_params=pltpu.CompilerParams(dimension_semantics=("parallel",), vmem_limit_bytes=VMEM_LIMIT),
    )(x, gf, wg, wu, wd, gl)


def _slab_weight(w_in):
    wq, wkv, wgn, wpr, wmq, wmg = jnp.split(w_in, np.cumsum(
        [NSA_WIDTH, NSA_KV_COLS, 3 * NSA_HEADS, RWKV_PROJ, MEM_WIDTH])[:5].tolist(), axis=1)
    d = w_in.shape[0]
    zeros = lambda n: jnp.zeros((d, n), w_in.dtype)
    w = jnp.concatenate([wmg, wpr, wmq, wq * (HEAD_DIM ** -0.5 * LOG2E), wgn, zeros(COL_KV - COL_GN - 3 * NSA_HEADS),
                         wkv, zeros(SLAB_COLS - COL_KV - NSA_KV_COLS)], axis=1)
    return w.astype(BF16), wkv.astype(BF16), wkv.T.astype(BF16)


def _channel_major_rows(x):
    b, _, t = x.shape
    return jnp.transpose(x.reshape(b, NSA_GROUPS, HEAD_DIM, t), (0, 3, 1, 2))[None]


def _channel_major_view(x):
    b, t = x.shape[:2]
    return jnp.transpose(x, (0, 2, 3, 1)).reshape(b, GROUP_LANES, t)


def kernel(x_prompt, x_sample, cache_cmp_k, cache_cmp_v, cache_slc_k, cache_slc_v, cache_win_k, cache_win_v, state_rwkv_shift, state_rwkv_wkv, cache_mem_k, cache_mem_v, page_table, mem_prompt, attn_norm, w_in, cmp_pe_k, cmp_w1_k, cmp_b1_k, cmp_w2_k, cmp_pe_v, cmp_w1_v, cmp_b1_v, cmp_w2_v, rwkv_mu, rwkv_w0, rwkv_w2, rwkv_a0, rwkv_a2, rwkv_g2, rwkv_kk, rwkv_ka, rwkv_rk, rwkv_ln_g, rwkv_ln_b, mem_norm, w_mem_kv, w_o_nsa, w_o_rwkv, w_o_mem, w_out, ffn_norm, w_gate, w_up, w_down, final_norm):
    assert w_in.shape[0] == 1, "one layer"
    bp, t, d = x_prompt.shape
    bs, tn, _ = x_sample.shape
    row2 = lambda a: a.reshape(1, -1)
    gl_ = GROUP_LANES

    w_slab, w_kv, w_kvt = _slab_weight(w_in[0])
    cmp_wk = _compress_weights(cmp_pe_k[0], cmp_w1_k[0], cmp_b1_k[0], cmp_w2_k[0])
    cmp_wv = _compress_weights(cmp_pe_v[0], cmp_w1_v[0], cmp_b1_v[0], cmp_w2_v[0])
    rw_params = (row2(rwkv_mu[0]), row2(rwkv_w0[0]), rwkv_w2[0].astype(BF16), row2(rwkv_a0[0]),
                 rwkv_a2[0].astype(BF16), rwkv_g2[0].astype(BF16), row2(rwkv_kk[0]), row2(rwkv_ka[0]),
                 row2(rwkv_rk[0]), row2(rwkv_ln_g[0]), row2(rwkv_ln_b[0]))
    wn, wr, wm, wo = (a[0].astype(BF16) for a in (w_o_nsa, w_o_rwkv, w_o_mem, w_out))
    wg, wu, wd = (a[0].astype(BF16) for a in (w_gate, w_up, w_down))
    gf, gl = row2(ffn_norm[0]), row2(final_norm)

    xp2 = x_prompt.reshape(bp * t, d)
    slab, kvt, ktb, vb, ck, cv = proj_prompt(xp2, row2(attn_norm[0]), w_slab[:, :COL_KV], w_kv, w_kvt, bp, t,
                                             1024, 1024)
    slab3 = slab.reshape(bp, t, COL_KV)
    nch = t // CMP_STRIDE
    kc, vc = compress_prompt(ck.reshape(bp, nch, CMP_STRIDE * gl_), cv.reshape(bp, nch, CMP_STRIDE * gl_),
                             cmp_wk, cmp_wv)
    o_nsa = nsa_prompt(slab, kc, vc, ktb, vb.reshape(bp, t, 2 * gl_), bp, t)
    o_rwkv, s_p = rwkv(slab3, jnp.zeros((bp, 1, RWKV_PROJ), F32),
                       jnp.zeros((bp, RWKV_HEADS, RWKV_HEAD_DIM, RWKV_HEAD_DIM), F32), rw_params, 64, BF16)
    mem_n = mem_prompt.shape[1]
    mkv = norm_matmul(mem_prompt.reshape(bp * mem_n, d), row2(mem_norm[0]), w_mem_kv[0].astype(BF16),
                      min(1024, bp * mem_n), 512).reshape(bp, mem_n, 2 * MEM_WIDTH)
    o_mem = mem_attend(slab3, mkv, 0, mkv, 1, 512, BF16)
    x1 = merge(xp2, o_nsa, o_rwkv.reshape(bp * t, RWKV_WIDTH), o_mem.reshape(bp * t, MEM_WIDTH), slab,
               wn, wr, wm, wo, 512)
    y_prompt = ffn(x1, gf, wg, wu, wd, gl, 256).reshape(bp, t, d)

    wp0 = max(t - WINDOW, 0)
    stream = lambda i: kvt[:, i * gl_:(i + 1) * gl_, :]
    p_state = (_channel_major_rows(stream(0)), _channel_major_rows(stream(1)),
               _channel_major_rows(stream(2)), _channel_major_rows(stream(3)),
               _channel_major_rows(stream(4)[:, :, wp0:]), _channel_major_rows(stream(5)[:, :, wp0:]),
               slab3[:, t - 1, COL_PR:COL_PR + RWKV_PROJ][None],
               s_p[None],
               mkv[:, :, :MEM_WIDTH].reshape(1, bp, mem_n, MEM_HEADS, MEM_HEAD_DIM),
               mkv[:, :, MEM_WIDTH:].reshape(1, bp, mem_n, MEM_HEADS, MEM_HEAD_DIM))

    past_len = page_table.shape[1] * cache_cmp_k.shape[2]
    assert past_len % SLC_BLOCK == 0
    assert (past_len + tn) // CMP_STRIDE == past_len // CMP_STRIDE and tn <= TOK_PAD
    xs2 = x_sample.reshape(bs * tn, d)
    slab_s = norm_matmul(xs2, row2(attn_norm[0]), w_slab, bs * tn, 512)
    slab_s3 = slab_s.reshape(bs, tn, SLAB_COLS)
    kc_s, vc_s = compress_sample(_channel_major_view(cache_cmp_k[0]), _channel_major_view(cache_cmp_v[0]),
                                 page_table, cmp_wk, cmp_wv)
    o_nsa_s = nsa_sample(slab_s3, kc_s, vc_s, _channel_major_view(cache_win_k[0]), _channel_major_view(cache_win_v[0]),
                         _channel_major_view(cache_slc_k[0]), _channel_major_view(cache_slc_v[0]),
                         page_table, past_len)
    o_rwkv_s, s_s = rwkv(slab_s3, state_rwkv_shift[0][:, None, :], state_rwkv_wkv[0], rw_params, 16, F32)
    mem_s = cache_mem_k.shape[2]
    o_mem_s = mem_attend(slab_s3, cache_mem_k[0].reshape(bs, mem_s, MEM_WIDTH), 0,
                         cache_mem_v[0].reshape(bs, mem_s, MEM_WIDTH), 0, tn, F32,
                         nseq=8 if bs % 8 == 0 else 1)
    x1s = merge(xs2, o_nsa_s.reshape(bs * tn, NSA_WIDTH), o_rwkv_s.reshape(bs * tn, RWKV_WIDTH),
                o_mem_s.reshape(bs * tn, MEM_WIDTH), slab_s, wn, wr, wm, wo, min(512, bs * tn))
    y_sample = ffn(x1s, gf, wg, wu, wd, gl, min(256, bs * tn)).reshape(bs, tn, d)
    heads = lambda a: a.reshape(1, a.shape[0], a.shape[1], NSA_GROUPS, HEAD_DIM)
    kv_new = [slab_s3[:, :, COL_KV + i * gl_:COL_KV + (i + 1) * gl_] for i in range(6)]
    s_state = (heads(kv_new[0]), heads(kv_new[1]), heads(kv_new[2]), heads(kv_new[3]),
               jnp.concatenate([cache_win_k[0], heads(kv_new[4])[0]], axis=1)[:, tn:][None],
               jnp.concatenate([cache_win_v[0], heads(kv_new[5])[0]], axis=1)[:, tn:][None],
               slab_s3[:, tn - 1, COL_PR:COL_PR + RWKV_PROJ][None],
               s_s[None])
    return (y_prompt, y_sample) + p_state + s_state
```

```python
import functools

import numpy as np
import jax
import jax.numpy as jnp
from jax import lax
from jax.experimental import pallas as pl
from jax.experimental.pallas import tpu as pltpu

F32 = jnp.float32
BF16 = jnp.bfloat16

D_MODEL = 1024
HEAD_DIM = 64
NSA_WIDTH = 768
NSA_HEADS = 12
NSA_GROUPS = 2
NSA_HPG = 6
CMP_LEN = 32
CMP_STRIDE = 16
CMP_HID = 64
SLC_BLOCK = 64
N_SELECT = 16
WINDOW = 512
Q_BLOCK = 64
FORCE_SCORE = 1e4
RWKV_WIDTH = 768
RWKV_HEAD_DIM = 64
RWKV_HEADS = 12
DECAY_LORA = 64
ICL_LORA = 64
GATE_LORA = 128
RWKV_PROJ = 3 * RWKV_WIDTH + DECAY_LORA + ICL_LORA + GATE_LORA
GN_EPS = 64e-5
MEM_HEADS = 4
MEM_WIDTH = 512
MEM_HEAD_DIM = 128
N_BRANCHES = 3
NSA_KV_COLS = 3 * 2 * NSA_GROUPS * HEAD_DIM
GROUP_LANES = NSA_GROUPS * HEAD_DIM
RMS_EPS = 1e-6
NEG = -1e30
LOG2E = 1.4426950408889634

COL_MG = 0
COL_PR = 3072
COL_MQ = 5632
COL_Q = 6144
COL_GN = 6912
COL_KV = 7168
SLAB_COLS = 8192
MAX_PAGES_PER_STEP = 32


def _pages_per_step(n_pages):
    n = min(MAX_PAGES_PER_STEP, n_pages)
    assert n_pages % n == 0
    return n

VMEM_LIMIT = 56 * 1024 * 1024


def _dot(a, b):
    return jnp.dot(a, b, preferred_element_type=F32)


def _dot_nt(a, b):
    return lax.dot_general(a, b, (((1,), (1,)), ((), ())), preferred_element_type=F32)


def _dot_tn(a, b):
    return lax.dot_general(a, b, (((0,), (0,)), ((), ())), preferred_element_type=F32)


def _iota(shape, dim):
    return lax.broadcasted_iota(jnp.int32, shape, dim)


def _eye(n, dtype):
    return (_iota((n, n), 0) == _iota((n, n), 1)).astype(dtype)


def _sigmoid(x):
    return 1.0 / (1.0 + jnp.exp(-x))


def _tanh(x):
    t = jnp.exp(-2.0 * jnp.abs(x))
    r = (1.0 - t) / (1.0 + t)
    return jnp.where(x < 0.0, -r, r)


def _gelu_tanh(x):
    return 0.5 * x * (1.0 + jnp.tanh(np.sqrt(2.0 / np.pi).astype(np.float32) * (x + 0.044715 * (x * x * x))))


def _rms(x, g):
    ms = jnp.mean(x * x, axis=-1, keepdims=True)
    return (x * lax.rsqrt(ms + RMS_EPS)) * g


def _pad_rows(x, n):
    if x.shape[0] == n:
        return x
    return jnp.concatenate([x, jnp.zeros((n - x.shape[0],) + x.shape[1:], x.dtype)], axis=0)


def _norm_matmul_body(x_ref, g_ref, w_ref, o_ref, h_ref):
    @pl.when(pl.program_id(1) == 0)
    def _():
        h_ref[...] = _rms(x_ref[...], g_ref[...]).astype(BF16)

    o_ref[...] = _dot(h_ref[...], w_ref[...])


def norm_matmul(x, g, w, tm, tn):
    n, d = x.shape
    c = w.shape[1]
    return pl.pallas_call(
        _norm_matmul_body,
        grid=(n // tm, c // tn),
        in_specs=[pl.BlockSpec((tm, d), lambda i, j: (i, 0)),
                  pl.BlockSpec((1, d), lambda i, j: (0, 0)),
                  pl.BlockSpec((d, tn), lambda i, j: (0, j))],
        out_specs=pl.BlockSpec((tm, tn), lambda i, j: (i, j)),
        out_shape=jax.ShapeDtypeStruct((n, c), F32),
        scratch_shapes=[pltpu.VMEM((tm, d), BF16)],
        compiler_params=pltpu.CompilerParams(dimension_semantics=("parallel", "arbitrary"),
                                             vmem_limit_bytes=VMEM_LIMIT),
    )(x, g, w)


def _proj_prompt_body(x_ref, g_ref, w_ref, wkv_ref, wkvt_ref, o_ref, kvt_ref, ktb_ref, vb_ref, ck_ref, cv_ref, h_ref):
    @pl.when(pl.program_id(1) == 0)
    def _():
        h = _rms(x_ref[...], g_ref[...]).astype(BF16)
        h_ref[...] = h
        gl = GROUP_LANES
        kvt = _dot_nt(wkvt_ref[...], h)
        kvt_ref[0] = kvt
        ktb_ref[0] = jnp.concatenate([kvt[2 * gl:3 * gl], kvt[4 * gl:5 * gl]], axis=0).astype(BF16)
        kv = _dot(h, wkv_ref[...])
        ck_ref[...] = kv[:, 0:gl].astype(BF16)
        cv_ref[...] = kv[:, gl:2 * gl].astype(BF16)
        vb_ref[...] = jnp.concatenate([kv[:, 3 * gl:4 * gl], kv[:, 5 * gl:6 * gl]], axis=1).astype(BF16)

    o_ref[...] = _dot(h_ref[...], w_ref[...])


def proj_prompt(x, g, w, wkv, wkvt, b, t, tm, tn):
    n, d = x.shape
    c = w.shape[1]
    tpb = t // tm
    gl = GROUP_LANES
    full = lambda a: pl.BlockSpec(a.shape, lambda i, j: (0, 0))
    rows = lambda width: pl.BlockSpec((tm, width), lambda i, j: (i, 0))
    return pl.pallas_call(
        _proj_prompt_body,
        grid=(n // tm, c // tn),
        in_specs=[pl.BlockSpec((tm, d), lambda i, j: (i, 0)), full(g),
                  pl.BlockSpec((d, tn), lambda i, j: (0, j)), full(wkv), full(wkvt)],
        out_specs=[pl.BlockSpec((tm, tn), lambda i, j: (i, j)),
                   pl.BlockSpec((1, NSA_KV_COLS, tm), lambda i, j: (i // tpb, 0, i % tpb)),
                   pl.BlockSpec((1, 2 * gl, tm), lambda i, j: (i // tpb, 0, i % tpb)),
                   rows(2 * gl), rows(gl), rows(gl)],
        out_shape=[jax.ShapeDtypeStruct((n, c), F32),
                   jax.ShapeDtypeStruct((b, NSA_KV_COLS, t), F32),
                   jax.ShapeDtypeStruct((b, 2 * gl, t), BF16),
                   jax.ShapeDtypeStruct((n, 2 * gl), BF16),
                   jax.ShapeDtypeStruct((n, gl), BF16),
                   jax.ShapeDtypeStruct((n, gl), BF16)],
        scratch_shapes=[pltpu.VMEM((tm, d), BF16)],
        compiler_params=pltpu.CompilerParams(dimension_semantics=("parallel", "arbitrary"),
                                             vmem_limit_bytes=VMEM_LIMIT),
    )(x, g, w, wkv, wkvt)


def _compress_consts(pe_ref, b1_ref, w0_ref, w1_ref):
    pe0 = jnp.broadcast_to(pe_ref[0], (8, pe_ref.shape[2])).astype(BF16)
    pe1 = jnp.broadcast_to(pe_ref[1], (8, pe_ref.shape[2])).astype(BF16)
    c = _dot(pe0, w0_ref[...]) + _dot(pe1, w1_ref[...])
    return c[0:1] + b1_ref[...]


def _compress_finish(u0, u1, cst, w2):
    n = u0.shape[0]
    pre = u0 + pltpu.roll(u1, n - 1, 0) + cst
    out = _dot(_gelu_tanh(pre).astype(BF16), w2)
    return jnp.where(_iota(out.shape, 0) < n - 1, out, 0.0)


def _compress_prompt_body(xk_ref, xv_ref, pek_ref, b1k_ref, w0k_ref, w1k_ref, w2k_ref,
                          pev_ref, b1v_ref, w0v_ref, w1v_ref, w2v_ref, ok_ref, ov_ref):
    for x_ref, pe_ref, b1_ref, w0_ref, w1_ref, w2_ref, o_ref in (
            (xk_ref, pek_ref, b1k_ref, w0k_ref, w1k_ref, w2k_ref, ok_ref),
            (xv_ref, pev_ref, b1v_ref, w0v_ref, w1v_ref, w2v_ref, ov_ref)):
        x = x_ref[0].astype(BF16)
        cst = _compress_consts(pe_ref, b1_ref, w0_ref, w1_ref)
        res = _compress_finish(_dot(x, w0_ref[...]), _dot(x, w1_ref[...]), cst, w2_ref[...])
        for g in range(NSA_GROUPS):
            o_ref[0, g] = res[:, g * HEAD_DIM:(g + 1) * HEAD_DIM].astype(o_ref.dtype)


def _compress_weights(pe, w1, b1, w2):
    r = CMP_LEN // CMP_STRIDE
    eye = jnp.eye(NSA_GROUPS, dtype=F32)
    w1r = w1.reshape(r, CMP_STRIDE, HEAD_DIM, CMP_HID)
    w1e = jnp.einsum('icdh,gk->icgdkh', w1r, eye).reshape(r, CMP_STRIDE * NSA_GROUPS * HEAD_DIM,
                                                         NSA_GROUPS * CMP_HID)
    pee = jnp.broadcast_to(pe.reshape(r, CMP_STRIDE, 1, HEAD_DIM), (r, CMP_STRIDE, NSA_GROUPS, HEAD_DIM))
    pee = pee.reshape(r, 1, CMP_STRIDE * NSA_GROUPS * HEAD_DIM)
    b1e = jnp.tile(b1, NSA_GROUPS).reshape(1, NSA_GROUPS * CMP_HID)
    w2e = jnp.einsum('hd,gk->ghkd', w2, eye).reshape(NSA_GROUPS * CMP_HID, NSA_GROUPS * HEAD_DIM)
    return pee, b1e, w1e[0].astype(BF16), w1e[1].astype(BF16), w2e.astype(BF16)


def compress_prompt(xk, xv, wk, wv):
    b, nch, width = xk.shape
    full = lambda a: pl.BlockSpec(a.shape, lambda i: (0,) * a.ndim)
    xspec = pl.BlockSpec((1, nch, width), lambda i: (i, 0, 0))
    ospec = pl.BlockSpec((1, NSA_GROUPS, nch, HEAD_DIM), lambda i: (i, 0, 0, 0))
    oshape = jax.ShapeDtypeStruct((b, NSA_GROUPS, nch, HEAD_DIM), BF16)
    return pl.pallas_call(
        _compress_prompt_body,
        grid=(b,),
        in_specs=[xspec, xspec] + [full(a) for a in wk] + [full(a) for a in wv],
        out_specs=[ospec, ospec],
        out_shape=[oshape, oshape],
        compiler_params=pltpu.CompilerParams(dimension_semantics=("parallel",), vmem_limit_bytes=VMEM_LIMIT),
    )(xk, xv, *wk, *wv)


def _split3(x):
    hi = x.astype(BF16)
    r1 = x - hi.astype(F32)
    mid = r1.astype(BF16)
    lo = (r1 - mid.astype(F32)).astype(BF16)
    return hi, mid, lo


def _select_blocks_lanes(score, n_blocks):
    nq, nbp = score.shape
    cols = [score[:, jp:jp + 1] for jp in range(n_blocks)]
    cnts = []
    for c0 in range(0, nbp, 128):
        blk = score[:, c0:c0 + 128]
        jj = c0 + _iota(blk.shape, 1)
        cnt = jnp.zeros(blk.shape, F32)
        for jp in range(n_blocks):
            ge = jnp.where(cols[jp] >= blk, 1.0, 0.0)
            gt = jnp.where(cols[jp] > blk, 1.0, 0.0)
            if jp < c0:
                cnt = cnt + ge
            elif jp >= c0 + 128:
                cnt = cnt + gt
            else:
                cnt = cnt + jnp.where(jj > jp, ge, gt)
        cnts.append(cnt)
    cnt = jnp.concatenate(cnts, axis=1)
    return jnp.where((cnt < N_SELECT) & (score > -jnp.inf), 1.0, 0.0)


def _select_blocks_unrolled(score):
    nb = score.shape[0]
    rows = [score[jp:jp + 1, :] for jp in range(nb)]
    cnts = []
    for r in range(nb // 8):
        blk = score[8 * r:8 * r + 8]
        jj = 8 * r + _iota(blk.shape, 0)
        cnt = jnp.zeros(blk.shape, F32)
        for jp in range(nb):
            ge = jnp.where(rows[jp] >= blk, 1.0, 0.0)
            gt = jnp.where(rows[jp] > blk, 1.0, 0.0)
            if jp < 8 * r:
                cnt = cnt + ge
            elif jp >= 8 * r + 8:
                cnt = cnt + gt
            else:
                cnt = cnt + jnp.where(jj > jp, ge, gt)
        cnts.append(cnt)
    cnt = jnp.concatenate(cnts, axis=0)
    return jnp.where((cnt < N_SELECT) & (score > -jnp.inf), 1.0, 0.0)


def _masked_softmax_rows(s, valid):
    s = jnp.where(valid, s, NEG)
    m = jnp.max(s, axis=-1, keepdims=True)
    p = jnp.where(valid, jnp.exp2(s - m), 0.0)
    l = jnp.sum(p, axis=-1, keepdims=True)
    return p / jnp.where(l > 0.0, l, 1.0)


def _online_update(carry, s, valid, v, pv=_dot):
    m, l, acc = carry
    s = jnp.where(valid, s, NEG)
    m_new = jnp.maximum(m, jnp.max(s, axis=-1, keepdims=True))
    alpha = jnp.exp2(m - m_new)
    p = jnp.where(valid, jnp.exp2(s - m_new), 0.0)
    l = alpha * l + jnp.sum(p, axis=-1, keepdims=True)
    acc = alpha * acc + pv(p.astype(BF16), v)
    return m_new, l, acc


def _online_update_biased(carry, s, v):
    m, l, acc = carry
    m_new = jnp.maximum(m, jnp.max(s, axis=-1, keepdims=True))
    alpha = jnp.exp2(m - m_new)
    p = jnp.exp2(s - m_new)
    l = alpha * l + jnp.sum(p, axis=-1, keepdims=True)
    acc = alpha * acc + _dot(p.astype(BF16), v)
    return m_new, l, acc


KV_TILE = 1024
WIN_TILE = 640


def _nsa_prompt_body(q_ref, gn_ref, kc_ref, vc_ref, kts_ref, vs_ref, ktw_ref, vw_ref, ovt_ref, o_ref):
    g = pl.program_id(1)
    qb = pl.program_id(2)
    nq = Q_BLOCK
    hpg = NSA_HPG
    rows = hpg * nq
    qf = q_ref[...]
    q2f = jnp.concatenate([qf[:, h * HEAD_DIM:(h + 1) * HEAD_DIM] for h in range(hpg)], axis=0)
    q2 = q2f.astype(BF16)
    zero = jnp.zeros_like(q2f)
    q2w = jnp.where(g == 0, jnp.concatenate([q2f, zero], axis=1), jnp.concatenate([zero, q2f], axis=1)).astype(BF16)
    t_q1 = qb * nq + _iota((nq, 1), 0)
    tile6 = lambda x: jnp.concatenate([x] * hpg, axis=0)
    pick = lambda x: jnp.where(g == 0, x[:, 0:HEAD_DIM], x[:, HEAD_DIM:2 * HEAD_DIM])

    kc = kc_ref[0, 0]
    ncp = kc.shape[0]
    c_end = _iota((nq, ncp), 1) * CMP_STRIDE + (CMP_LEN - 1)
    s_c = _dot_nt(q2, kc) + tile6(jnp.where(c_end <= t_q1, 0.0, NEG))
    e_c = jnp.exp2(s_c - jnp.max(s_c, axis=-1, keepdims=True))
    l_c = jnp.sum(e_c, axis=-1, keepdims=True)
    any_c = tile6(t_q1 >= CMP_LEN - 1)
    p_c = e_c * jnp.where(any_c, 1.0 / l_c, 0.0)
    o_c = _dot(p_c.astype(BF16), vc_ref[0, 0])
    psum = p_c[0:nq]
    for h in range(1, hpg):
        psum = psum + p_c[h * nq:(h + 1) * nq]

    imp_t = sum(_dot_nt(ovt_ref[...], part) for part in _split3(psum))
    j_idx = _iota(imp_t.shape, 0)
    forced = (j_idx == 0) | (j_idx == qb) | (j_idx == qb - 1)
    score = jnp.where(forced, FORCE_SCORE, imp_t)
    sel_t = _select_blocks_unrolled(jnp.where(j_idx <= qb, score, -jnp.inf))
    sel = _dot_nt(_eye(nq, BF16), sel_t.astype(BF16)).astype(BF16)

    w0 = pl.multiple_of(jnp.maximum(qb * nq - WINDOW, 0) // 128 * 128, 128)
    diff = t_q1 - (w0 + _iota((nq, WIN_TILE), 1))
    ok_w = (diff >= 0) & (diff <= WINDOW)
    s_w = _dot(q2w, ktw_ref[0, :, pl.ds(w0, WIN_TILE)]) + tile6(jnp.where(ok_w, 0.0, NEG))
    e_w = jnp.exp2(s_w - jnp.max(s_w, axis=-1, keepdims=True))
    l_w = jnp.sum(e_w, axis=-1, keepdims=True)
    o_w = pick(_dot(e_w.astype(BF16), vw_ref[0, pl.ds(w0, WIN_TILE), :])) / l_w

    bpt = KV_TILE // SLC_BLOCK
    col_blk = _iota((sel.shape[1], KV_TILE), 1) // SLC_BLOCK
    row_blk = _iota((sel.shape[1], KV_TILE), 0)

    def block_mask(kt):
        expand = jnp.where(row_blk == col_blk + kt * bpt, 1.0, 0.0).astype(BF16)
        return _dot(sel, expand) > 0.5

    def scores(kt):
        off = pl.multiple_of(kt * KV_TILE, KV_TILE)
        return _dot(q2w, kts_ref[0, :, pl.ds(off, KV_TILE)])

    def values(kt):
        off = pl.multiple_of(kt * KV_TILE, KV_TILE)
        return vs_ref[0, pl.ds(off, KV_TILE), :]

    nt = qb // bpt
    ok_d = block_mask(nt) & (_iota((nq, KV_TILE), 1) + nt * KV_TILE <= t_q1)
    init = (jnp.full((rows, 1), NEG, F32), jnp.zeros((rows, 1), F32), jnp.zeros((rows, GROUP_LANES), F32))
    carry = _online_update_biased(init, scores(nt) + tile6(jnp.where(ok_d, 0.0, NEG)), values(nt))

    def tile_step(kt, carry):
        return _online_update_biased(carry, scores(kt) + tile6(jnp.where(block_mask(kt), 0.0, NEG)), values(kt))

    _, l_s, acc_s = lax.fori_loop(0, nt, tile_step, carry)
    o_s = pick(acc_s) / l_s

    gates = _sigmoid(gn_ref[...])
    per_group = hpg * 3
    gates = jnp.where(g == 0, gates[:, 0:per_group], gates[:, per_group:2 * per_group])
    outs = []
    for h in range(hpg):
        sl = slice(h * nq, (h + 1) * nq)
        outs.append(gates[:, 3 * h:3 * h + 1] * o_c[sl] + gates[:, 3 * h + 1:3 * h + 2] * o_s[sl]
                    + gates[:, 3 * h + 2:3 * h + 3] * o_w[sl])
    o_ref[...] = jnp.concatenate(outs, axis=1).astype(o_ref.dtype)


def _overlap_t(n_blocks, n_cmp_padded, n_cmp):
    i = np.arange(n_cmp_padded)[None, :] * CMP_STRIDE
    j = np.arange(n_blocks)[:, None] * SLC_BLOCK
    ov = (i < j + SLC_BLOCK) & (i + CMP_LEN > j) & (np.arange(n_cmp_padded)[None, :] < n_cmp)
    return jnp.asarray(ov.astype(np.float32)).astype(BF16)


def nsa_prompt(slab, kc, vc, ktb, vb, b, t):
    nb = t // Q_BLOCK
    gw = NSA_HPG * HEAD_DIM
    gl = GROUP_LANES
    ncp = kc.shape[2]
    ovt = _overlap_t(nb, ncp, ncp - 1)
    kt_spec = lambda k: pl.BlockSpec((1, gl, t), lambda bi, g, qb: (bi, k, 0))
    v_spec = lambda k: pl.BlockSpec((1, t, gl), lambda bi, g, qb: (bi, 0, k))
    cmp_spec = pl.BlockSpec((1, 1, ncp, HEAD_DIM), lambda bi, g, qb: (bi, g, 0, 0))
    return pl.pallas_call(
        _nsa_prompt_body,
        grid=(b, NSA_GROUPS, nb),
        in_specs=[pl.BlockSpec((Q_BLOCK, gw), lambda bi, g, qb: (bi * nb + qb, COL_Q // gw + g)),
                  pl.BlockSpec((Q_BLOCK, 128), lambda bi, g, qb: (bi * nb + qb, COL_GN // 128)),
                  cmp_spec, cmp_spec, kt_spec(0), v_spec(0), kt_spec(1), v_spec(1),
                  pl.BlockSpec(ovt.shape, lambda bi, g, qb: (0, 0))],
        out_specs=pl.BlockSpec((Q_BLOCK, gw), lambda bi, g, qb: (bi * nb + qb, g)),
        out_shape=jax.ShapeDtypeStruct((b * t, NSA_WIDTH), BF16),
        compiler_params=pltpu.CompilerParams(dimension_semantics=("parallel", "parallel", "arbitrary"),
                                             vmem_limit_bytes=VMEM_LIMIT),
    )(slab, slab, kc, vc, ktb, vb, ktb, vb, ovt)


def _compress_sample_body(n, pt_ref, *refs):
    k_pages, v_pages = refs[0:n], refs[n:2 * n]
    (pek_ref, b1k_ref, w0k_ref, w1k_ref, w2k_ref, wck_ref,
     pev_ref, b1v_ref, w0v_ref, w1v_ref, w2v_ref, wcv_ref, ok_ref, ov_ref, uk_ref, uv_ref) = refs[2 * n:]
    j = pl.program_id(1)
    page_rows = k_pages[0].shape[2]
    cpp = page_rows // CMP_STRIDE
    rows = n * cpp
    off = pl.multiple_of(j * rows, rows)
    rp = _iota((page_rows, page_rows), 0)
    perm = (_iota((page_rows, page_rows), 1) == CMP_STRIDE * (rp % cpp) + rp // cpp).astype(BF16)
    for pages, wc_ref, u_ref in ((k_pages, wck_ref, uk_ref), (v_pages, wcv_ref, uv_ref)):
        xp = [_dot_nt(perm, r[0].astype(BF16)) for r in pages]
        x = jnp.concatenate([jnp.concatenate([p[c * cpp:(c + 1) * cpp] for p in xp], axis=0)
                             for c in range(CMP_STRIDE)], axis=1)
        u_ref[pl.ds(off, rows), :] = _dot(x.astype(BF16), wc_ref[...])

    @pl.when(j == pl.num_programs(1) - 1)
    def _():
        half = NSA_GROUPS * CMP_HID
        for pe_ref, b1_ref, w0_ref, w1_ref, w2_ref, u_ref, o_ref in (
                (pek_ref, b1k_ref, w0k_ref, w1k_ref, w2k_ref, uk_ref, ok_ref),
                (pev_ref, b1v_ref, w0v_ref, w1v_ref, w2v_ref, uv_ref, ov_ref)):
            cst = _compress_consts(pe_ref, b1_ref, w0_ref, w1_ref)
            u = u_ref[...]
            res = _compress_finish(u[:, 0:half], u[:, half:2 * half], cst, w2_ref[...])
            for g in range(NSA_GROUPS):
                o_ref[0, g] = res[:, g * HEAD_DIM:(g + 1) * HEAD_DIM].astype(o_ref.dtype)


def compress_sample(pool_k, pool_v, page_table, wk, wv):
    bs, n_pages = page_table.shape
    _, lanes, page_rows = pool_k.shape
    n = _pages_per_step(n_pages)
    nch = n_pages * page_rows // CMP_STRIDE
    wck = jnp.concatenate([wk[2], wk[3]], axis=1)
    wcv = jnp.concatenate([wv[2], wv[3]], axis=1)
    page = lambda k: pl.BlockSpec((1, lanes, page_rows), lambda b, j, pt: (pt[b, n * j + k], 0, 0))
    full = lambda a: pl.BlockSpec(a.shape, lambda b, j, pt: (0,) * a.ndim)
    consts = list(wk) + [wck] + list(wv) + [wcv]
    ospec = pl.BlockSpec((1, NSA_GROUPS, nch, HEAD_DIM), lambda b, j, pt: (b, 0, 0, 0))
    oshape = jax.ShapeDtypeStruct((bs, NSA_GROUPS, nch, HEAD_DIM), BF16)
    return pl.pallas_call(
        functools.partial(_compress_sample_body, n),
        grid_spec=pltpu.PrefetchScalarGridSpec(
            num_scalar_prefetch=1,
            grid=(bs, n_pages // n),
            in_specs=[page(k) for k in range(n)] * 2 + [full(a) for a in consts],
            out_specs=[ospec, ospec],
            scratch_shapes=[pltpu.VMEM((nch, 2 * NSA_GROUPS * CMP_HID), F32)] * 2),
        out_shape=[oshape, oshape],
        compiler_params=pltpu.CompilerParams(dimension_semantics=("parallel", "arbitrary"),
                                             vmem_limit_bytes=VMEM_LIMIT),
    )(page_table, *([pool_k] * n), *([pool_v] * n), *consts)


TOK_PAD = 8


def _nsa_sample_body(past_len, tn, n, pt_ref, *refs):
    q_ref, gn_ref, skn_ref, svn_ref, wkn_ref, wvn_ref, kc_ref, vc_ref, wkc_ref, wvc_ref = refs[0:10]
    k_pages, v_pages = refs[10:10 + n], refs[10 + n:10 + 2 * n]
    ov_ref, o_ref, selt_ref, m_ref, l_ref, acc_ref, oc_ref, ow_ref = refs[10 + 2 * n:]
    j = pl.program_id(1)
    tp = TOK_PAD
    hpg = NSA_HPG
    grows = hpg * tp
    rows = NSA_GROUPS * grows
    gw = hpg * HEAD_DIM
    lanes = GROUP_LANES
    nsp = selt_ref.shape[0]

    q8 = _pad_rows(q_ref[0], tp)
    zero = jnp.zeros((grows, HEAD_DIM), F32)
    q2, q_parts = [], []
    for g in range(NSA_GROUPS):
        qg = jnp.concatenate([q8[:, g * gw + h * HEAD_DIM:g * gw + (h + 1) * HEAD_DIM] for h in range(hpg)], axis=0)
        q2.append(qg.astype(BF16))
        q_parts.append(jnp.concatenate([qg, zero] if g == 0 else [zero, qg], axis=1))
    q_all = jnp.concatenate(q_parts, axis=0).astype(BF16)
    tok = _iota((rows, 1), 0) % tp
    t_q = past_len + tok

    def stack_groups(x):
        return jnp.concatenate([x[g * tp:(g + 1) * tp] for g in range(NSA_GROUPS) for _ in range(hpg)], axis=0)

    def new_keys_valid(width):
        tk = _iota((rows, width), 1)
        return (tk <= tok) & (tk < tn)

    @pl.when(j == 0)
    def _():
        psums = []
        tq_g = t_q[0:grows]
        for g in range(NSA_GROUPS):
            s_c = _dot_nt(q2[g], kc_ref[0, g])
            c_end = _iota(s_c.shape, 1) * CMP_STRIDE + (CMP_LEN - 1)
            p_c = _masked_softmax_rows(s_c, c_end <= tq_g)
            oc_ref[g] = _dot(p_c.astype(BF16), vc_ref[0, g])
            ps = p_c[0:tp]
            for h in range(1, hpg):
                ps = ps + p_c[h * tp:(h + 1) * tp]
            psums.append(ps)
        psum = jnp.concatenate(psums, axis=0)
        imp = sum(_dot(part, ov_ref[...]) for part in _split3(psum))
        j_idx = _iota(imp.shape, 1)
        cur = (past_len + _iota(imp.shape, 0) % tp) // SLC_BLOCK
        forced = (j_idx == 0) | (j_idx == cur) | (j_idx == cur - 1)
        score = jnp.where(j_idx <= cur, jnp.where(forced, FORCE_SCORE, imp), -jnp.inf)
        sel = _select_blocks_lanes(score, (past_len + tn - 1) // SLC_BLOCK + 1)
        pick_row = (_iota((nsp, sel.shape[1]), 0) == _iota((nsp, sel.shape[1]), 1)).astype(BF16)
        selt_ref[...] = _dot_nt(pick_row, sel.astype(BF16))

        lw = wkc_ref.shape[2]
        kwn = _pad_rows(wkn_ref[0], 16).astype(BF16)
        vwn = _pad_rows(wvn_ref[0], 16).astype(BF16)
        diff = t_q - (past_len - lw + _iota((rows, lw), 1))
        carry = (jnp.full((rows, 1), NEG, F32), jnp.zeros((rows, 1), F32), jnp.zeros((rows, lanes), F32))
        carry = _online_update(carry, _dot(q_all, wkc_ref[0].astype(BF16)), (diff >= 0) & (diff <= WINDOW),
                               wvc_ref[0].astype(BF16), pv=_dot_nt)
        _, l_w, acc_w = _online_update(carry, _dot_nt(q_all, kwn), new_keys_valid(16), vwn)
        ow_ref[...] = acc_w / l_w
        m_ref[...] = jnp.full((rows, 1), NEG, F32)
        l_ref[...] = jnp.zeros((rows, 1), F32)
        acc_ref[...] = jnp.zeros((rows, lanes), F32)

    page_rows = k_pages[0].shape[2]
    nk = n * page_rows
    bps = nk // SLC_BLOCK
    kt = jnp.concatenate([r[0] for r in k_pages], axis=1).astype(BF16)
    vt = jnp.concatenate([r[0] for r in v_pages], axis=1).astype(BF16)
    expand = jnp.where(_iota((bps, nk), 0) == _iota((bps, nk), 1) // SLC_BLOCK, 1.0, 0.0).astype(BF16)
    sel_rows = selt_ref[pl.ds(pl.multiple_of(j * bps, bps), bps), :].astype(BF16)
    bias = stack_groups(jnp.where(_dot_tn(sel_rows, expand) > 0.5, 0.0, NEG))
    m, l, acc = (m_ref[...], l_ref[...], acc_ref[...])
    s = _dot(q_all, kt) + bias
    m_new = jnp.maximum(m, jnp.max(s, axis=-1, keepdims=True))
    alpha = jnp.exp2(m - m_new)
    p = jnp.exp2(s - m_new)
    m_ref[...] = m_new
    l_ref[...] = alpha * l + jnp.sum(p, axis=-1, keepdims=True)
    acc_ref[...] = alpha * acc + _dot_nt(p.astype(BF16), vt)

    @pl.when(j == pl.num_programs(1) - 1)
    def _():
        kn = _pad_rows(skn_ref[0], 16).astype(BF16)
        vn = _pad_rows(svn_ref[0], 16).astype(BF16)
        expand_n = jnp.where(_iota((nsp, 16), 0) == (past_len + _iota((nsp, 16), 1)) // SLC_BLOCK,
                             1.0, 0.0).astype(BF16)
        sel_n = stack_groups(_dot_tn(selt_ref[...].astype(BF16), expand_n)) > 0.5
        _, l_s, acc_s = _online_update((m_ref[...], l_ref[...], acc_ref[...]), _dot_nt(q_all, kn),
                                       sel_n & new_keys_valid(16), vn)
        o_s_all = acc_s / l_s
        o_w_all = ow_ref[...]
        gates = _sigmoid(_pad_rows(gn_ref[0], tp))
        outs = []
        for g in range(NSA_GROUPS):
            gsl = slice(g * HEAD_DIM, (g + 1) * HEAD_DIM)
            o_c = oc_ref[g]
            for h in range(hpg):
                sl = slice(h * tp, (h + 1) * tp)
                asl = slice(g * grows + h * tp, g * grows + (h + 1) * tp)
                c0 = (g * hpg + h) * 3
                outs.append(gates[:, c0:c0 + 1] * o_c[sl] + gates[:, c0 + 1:c0 + 2] * o_s_all[asl, gsl]
                            + gates[:, c0 + 2:c0 + 3] * o_w_all[asl, gsl])
        o_ref[0] = jnp.concatenate(outs, axis=1)[0:tn].astype(o_ref.dtype)


def nsa_sample(slab3, kc, vc, win_kt, win_vt, pool_kt, pool_vt, page_table, past_len):
    bs, tn, _ = slab3.shape
    n_pages = page_table.shape[1]
    n = _pages_per_step(n_pages)
    page_rows = pool_kt.shape[2]
    lanes = GROUP_LANES
    ncp = kc.shape[2]
    ns = -(-(past_len + tn) // SLC_BLOCK)
    nsp = -(-ns // 8) * 8
    ov = _overlap_t(-(-ns // 128) * 128, ncp, (past_len + tn) // CMP_STRIDE - CMP_LEN // CMP_STRIDE + 1).T
    rows = NSA_GROUPS * NSA_HPG * TOK_PAD
    tokblk = lambda width, col: pl.BlockSpec((1, tn, width), lambda b, j, pt: (b, 0, col // width))
    cmp_spec = pl.BlockSpec((1, NSA_GROUPS, ncp, HEAD_DIM), lambda b, j, pt: (b, 0, 0, 0))
    win_spec = pl.BlockSpec((1, lanes, win_kt.shape[2]), lambda b, j, pt: (b, 0, 0))
    page = lambda k: pl.BlockSpec((1, lanes, page_rows), lambda b, j, pt: (pt[b, n * j + k], 0, 0))
    return pl.pallas_call(
        functools.partial(_nsa_sample_body, past_len, tn, n),
        grid_spec=pltpu.PrefetchScalarGridSpec(
            num_scalar_prefetch=1,
            grid=(bs, n_pages // n),
            in_specs=[tokblk(NSA_WIDTH, COL_Q), tokblk(128, COL_GN),
                      tokblk(lanes, COL_KV + 2 * lanes), tokblk(lanes, COL_KV + 3 * lanes),
                      tokblk(lanes, COL_KV + 4 * lanes), tokblk(lanes, COL_KV + 5 * lanes),
                      cmp_spec, cmp_spec, win_spec, win_spec]
            + [page(k) for k in range(n)] * 2
            + [pl.BlockSpec(ov.shape, lambda b, j, pt: (0, 0))],
            out_specs=pl.BlockSpec((1, tn, NSA_WIDTH), lambda b, j, pt: (b, 0, 0)),
            scratch_shapes=[pltpu.VMEM((nsp, NSA_GROUPS * TOK_PAD), F32),
                            pltpu.VMEM((rows, 1), F32),
                            pltpu.VMEM((rows, 1), F32),
                            pltpu.VMEM((rows, lanes), F32),
                            pltpu.VMEM((NSA_GROUPS, NSA_HPG * TOK_PAD, HEAD_DIM), F32),
                            pltpu.VMEM((rows, lanes), F32)]),
        out_shape=jax.ShapeDtypeStruct((bs, tn, NSA_WIDTH), F32),
        compiler_params=pltpu.CompilerParams(dimension_semantics=("parallel", "arbitrary"),
                                             vmem_limit_bytes=VMEM_LIMIT),
    )(page_table, slab3, slab3, slab3, slab3, slab3, slab3, kc, vc, win_kt, win_vt,
      *([pool_kt] * n), *([pool_vt] * n), ov)


def _cumsum_rows(x):
    n = x.shape[0]
    row = _iota((n, 1), 0)
    k = 1
    while k < n:
        x = x + jnp.where(row >= k, pltpu.roll(x, k, 0), 0.0)
        k *= 2
    return x


def _rwkv_body(n_valid, chunk, p0_ref, p1_ref, p2_ref, p3_ref, p4_ref, prev_ref, s0_ref,
               mu_ref, w0_ref, w2_ref, a0_ref, a2_ref, g2_ref, kk_ref, ka_ref, rk_ref, lng_ref, lnb_ref,
               o_ref, sout_ref, carry_ref, s_ref):
    c = pl.program_id(1)
    hd = RWKV_HEAD_DIM

    @pl.when(c == 0)
    def _():
        carry_ref[...] = jnp.broadcast_to(prev_ref[0], carry_ref.shape)
        s_ref[...] = s0_ref[0]

    p = jnp.concatenate([r[0] for r in (p0_ref, p1_ref, p2_ref, p3_ref, p4_ref)], axis=1)
    p = _pad_rows(p, chunk)
    row = _iota((chunk, 1), 0)
    valid = row < n_valid
    prev = jnp.where(row == 0, carry_ref[0:1, :], pltpu.roll(p, 1, 0))
    xm = p + (prev - p) * mu_ref[...]
    carry_ref[...] = jnp.broadcast_to(p[n_valid - 1:n_valid, :], carry_ref.shape)

    wdt = RWKV_WIDTH
    r_all, k_all, v_all = xm[:, 0:wdt], xm[:, wdt:2 * wdt], xm[:, 2 * wdt:3 * wdt]
    o = 3 * wdt
    wd, ad, gd = xm[:, o:o + DECAY_LORA], xm[:, o + DECAY_LORA:o + DECAY_LORA + ICL_LORA], \
        xm[:, o + DECAY_LORA + ICL_LORA:o + DECAY_LORA + ICL_LORA + GATE_LORA]
    w = w0_ref[...] + _dot(_tanh(wd).astype(BF16), w2_ref[...])
    logw = -_sigmoid(w) * float(np.exp(-0.5))
    a_all = _sigmoid(a0_ref[...] + _dot(ad.astype(BF16), a2_ref[...]))
    g_all = _dot(_sigmoid(gd).astype(BF16), g2_ref[...])
    logw = jnp.where(valid, logw, 0.0)
    cum = _cumsum_rows(logw)
    total = cum[chunk - 1:chunk, :]
    w_in = jnp.exp(cum)
    w_ex = jnp.exp(cum - logw)
    w_inv = jnp.exp(-cum)
    w_rem = jnp.exp(total - cum)
    w_tot = jnp.exp(total)
    kk_all = k_all * kk_ref[...]
    k2_all = k_all * (1.0 + (a_all - 1.0) * ka_ref[...])

    t_i = _iota((chunk, chunk), 0)
    s_i = _iota((chunk, chunk), 1)
    strict = s_i < t_i
    incl = s_i <= t_i
    n_rounds = int(np.log2(chunk))
    heads = range(RWKV_HEADS)
    sls = [slice(h * hd, (h + 1) * hd) for h in heads]

    lr, bt, kt, bk, vb, at, rt = [], [], [], [], [], [], []
    head_of = (_iota((wdt, 128), 0) // hd == _iota((wdt, 128), 1)).astype(BF16)
    ssq = sum(_dot(part, head_of) for part in _split3(kk_all * kk_all))
    inv = sum(_dot_nt(part, head_of) for part in _split3(lax.rsqrt(jnp.maximum(ssq, 1e-24))))
    kkn_all = jnp.where(valid, kk_all * inv, 0.0)
    k2m_all = jnp.where(valid, k2_all, 0.0)
    vm_all = jnp.where(valid, v_all, 0.0)
    b_all = kkn_all * a_all
    at_all = -kkn_all * w_ex
    rt_all = r_all * w_in
    bt_all = b_all * w_inv
    kt_all = k2m_all * w_inv
    bp_all = b_all * w_rem
    kp_all = k2m_all * w_rem
    for sl in sls:
        at.append(at_all[:, sl].astype(BF16))
        rt.append(rt_all[:, sl].astype(BF16))
        lr.append(jnp.concatenate([at_all[:, sl], rt_all[:, sl]], axis=0).astype(BF16))
        bt.append(bt_all[:, sl].astype(BF16))
        kt.append(kt_all[:, sl].astype(BF16))
        bk.append(jnp.concatenate([bp_all[:, sl], kp_all[:, sl]], axis=0).astype(BF16))
        vb.append(vm_all[:, sl])
    m_b = [_dot_nt(lr[h], bt[h]) for h in heads]
    m_k = [_dot_nt(lr[h], kt[h]) for h in heads]
    a_ab = [jnp.where(strict, m[0:chunk], 0.0) for m in m_b]
    a_rb = [jnp.where(incl, m[chunk:2 * chunk], 0.0).astype(BF16) for m in m_b]
    a_ak = [jnp.where(strict, m[0:chunk], 0.0).astype(BF16) for m in m_k]
    a_rk = [jnp.where(incl, m[chunk:2 * chunk], 0.0).astype(BF16) for m in m_k]
    s0 = [s_ref[h] for h in heads]
    s0b = [x.astype(BF16) for x in s0]
    vbb = [x.astype(BF16) for x in vb]
    u = [_dot_nt(at[h], s0b[h]) + _dot(a_ak[h], vbb[h]) for h in heads]
    pw = a_ab
    for it in range(n_rounds):
        pwb = [x.astype(BF16) for x in pw]
        u = [u[h] + _dot(pwb[h], u[h].astype(BF16)) for h in heads]
        if it + 1 < n_rounds:
            pw = [_dot(x, x) for x in pwb]
    ub = [x.astype(BF16) for x in u]
    y = [_dot_nt(rt[h], s0b[h]) + _dot(a_rb[h], ub[h]) + _dot(a_rk[h], vbb[h]) for h in heads]
    for h in heads:
        uv = jnp.concatenate([u[h], vb[h]], axis=0).astype(BF16)
        s_ref[h] = s0[h] * w_tot[:, sls[h]] + _dot_tn(uv, bk[h])
    outs = []
    for h in heads:
        sl = sls[h]
        mean = jnp.mean(y[h], axis=-1, keepdims=True)
        yc = y[h] - mean
        var = jnp.mean(yc * yc, axis=-1, keepdims=True)
        yn = yc * lax.rsqrt(var + GN_EPS)
        bonus = jnp.sum(r_all[:, sl] * k2_all[:, sl] * rk_ref[:, sl], axis=-1, keepdims=True) * v_all[:, sl]
        outs.append((yn * lng_ref[:, sl] + lnb_ref[:, sl] + bonus) * g_all[:, sl])
    out = jnp.concatenate(outs, axis=1)
    o_ref[0] = out[0:o_ref.shape[1]].astype(o_ref.dtype)

    @pl.when(c == pl.num_programs(1) - 1)
    def _():
        sout_ref[0] = s_ref[...]


def rwkv(slab3, p_prev, s0, params, chunk, out_dtype):
    b, t, _ = slab3.shape
    tc = min(t, chunk)
    nchunks = t // tc
    blk = 512
    pspec = lambda k: pl.BlockSpec((1, tc, blk), lambda bi, c: (bi, c, COL_PR // blk + k))
    full = lambda a: pl.BlockSpec(a.shape, lambda bi, c: (0,) * a.ndim)
    sspec = pl.BlockSpec((1, RWKV_HEADS, RWKV_HEAD_DIM, RWKV_HEAD_DIM), lambda bi, c: (bi, 0, 0, 0))
    return pl.pallas_call(
        functools.partial(_rwkv_body, tc, chunk),
        grid=(b, nchunks),
        in_specs=[pspec(k) for k in range(5)]
        + [pl.BlockSpec((1, 1, RWKV_PROJ), lambda bi, c: (bi, 0, 0)), sspec]
        + [full(a) for a in params],
        out_specs=[pl.BlockSpec((1, tc, RWKV_WIDTH), lambda bi, c: (bi, c, 0)), sspec],
        out_shape=[jax.ShapeDtypeStruct((b, t, RWKV_WIDTH), out_dtype),
                   jax.ShapeDtypeStruct(s0.shape, F32)],
        scratch_shapes=[pltpu.VMEM((8, RWKV_PROJ), F32),
                        pltpu.VMEM((RWKV_HEADS, RWKV_HEAD_DIM, RWKV_HEAD_DIM), F32)],
        compiler_params=pltpu.CompilerParams(dimension_semantics=("parallel", "arbitrary"),
                                             vmem_limit_bytes=VMEM_LIMIT),
    )(slab3, slab3, slab3, slab3, slab3, p_prev, s0, *params)


def _mem_attend_body(q_ref, k_ref, v_ref, o_ref):
    nseq, tm = q_ref.shape[0], q_ref.shape[1]
    for bb in range(nseq):
        q = _pad_rows(q_ref[bb], max(tm, 16)).astype(BF16)
        k = k_ref[bb].astype(BF16)
        v = v_ref[bb].astype(BF16)
        outs = []
        for h in range(MEM_HEADS):
            sl = slice(h * MEM_HEAD_DIM, (h + 1) * MEM_HEAD_DIM)
            s = _dot_nt(q[:, sl], k[:, sl]) * (MEM_HEAD_DIM ** -0.5)
            m = jnp.max(s, axis=-1, keepdims=True)
            p = jnp.exp(s - m)
            p = p / jnp.sum(p, axis=-1, keepdims=True)
            outs.append(_dot(p.astype(BF16), v[:, sl]))
        o_ref[bb] = jnp.concatenate(outs, axis=1)[0:tm].astype(o_ref.dtype)


def mem_attend(slab3, mk, k_blk, mv, v_blk, tm, out_dtype, nseq=1):
    b, t, _ = slab3.shape
    m = mk.shape[1]
    return pl.pallas_call(
        _mem_attend_body,
        grid=(b // nseq, t // tm),
        in_specs=[pl.BlockSpec((nseq, tm, MEM_WIDTH), lambda bi, i: (bi, i, COL_MQ // MEM_WIDTH)),
                  pl.BlockSpec((nseq, m, MEM_WIDTH), lambda bi, i: (bi, 0, k_blk)),
                  pl.BlockSpec((nseq, m, MEM_WIDTH), lambda bi, i: (bi, 0, v_blk))],
        out_specs=pl.BlockSpec((nseq, tm, MEM_WIDTH), lambda bi, i: (bi, i, 0)),
        out_shape=jax.ShapeDtypeStruct((b, t, MEM_WIDTH), out_dtype),
        compiler_params=pltpu.CompilerParams(dimension_semantics=("parallel", "parallel"),
                                             vmem_limit_bytes=VMEM_LIMIT),
    )(slab3, mk, mv)


def _merge_body(x_ref, on_ref, or_ref, om_ref, g0_ref, g1_ref, g2_ref, wn_ref, wr_ref, wm_ref, wo_ref, o_ref):
    m = _sigmoid(g0_ref[...]) * _dot(on_ref[...].astype(BF16), wn_ref[...])
    m = m + _sigmoid(g1_ref[...]) * _dot(or_ref[...].astype(BF16), wr_ref[...])
    m = m + _sigmoid(g2_ref[...]) * _dot(om_ref[...].astype(BF16), wm_ref[...])
    o_ref[...] = x_ref[...] + _dot(m.astype(BF16), wo_ref[...])


def merge(x, o_nsa, o_rwkv, o_mem, slab, wn, wr, wm, wo, tm):
    n, d = x.shape
    row = lambda w: pl.BlockSpec((tm, w), lambda i: (i, 0))
    full = lambda a: pl.BlockSpec(a.shape, lambda i: (0, 0))
    gate = lambda k: pl.BlockSpec((tm, d), lambda i: (i, COL_MG // d + k))
    return pl.pallas_call(
        _merge_body,
        grid=(n // tm,),
        in_specs=[row(d), row(NSA_WIDTH), row(RWKV_WIDTH), row(MEM_WIDTH), gate(0), gate(1), gate(2),
                  full(wn), full(wr), full(wm), full(wo)],
        out_specs=row(d),
        out_shape=jax.ShapeDtypeStruct((n, d), F32),
        compiler_params=pltpu.CompilerParams(dimension_semantics=("parallel",), vmem_limit_bytes=VMEM_LIMIT),
    )(x, o_nsa, o_rwkv, o_mem, slab, slab, slab, wn, wr, wm, wo)


def _ffn_body(x_ref, gf_ref, wg_ref, wu_ref, wd_ref, gl_ref, o_ref):
    x = x_ref[...]
    hf = _rms(x, gf_ref[...]).astype(BF16)
    gate = _dot(hf, wg_ref[...])
    up = _dot(hf, wu_ref[...])
    act = (gate * _sigmoid(gate) * up).astype(BF16)
    x2 = x + _dot(act, wd_ref[...])
    o_ref[...] = _rms(x2, gl_ref[...])


def ffn(x, gf, wg, wu, wd, gl, tm):
    n, d = x.shape
    row = pl.BlockSpec((tm, d), lambda i: (i, 0))
    full = lambda a: pl.BlockSpec(a.shape, lambda i: (0, 0), pipeline_mode=pl.Buffered(1))
    return pl.pallas_call(
        _ffn_body,
        grid=(n // tm,),
        in_specs=[row, full(gf), full(wg), full(wu), full(wd), full(gl)],
        out_specs=row,
        out_shape=jax.ShapeDtypeStruct((n, d), F32),
        compiler_params=pltpu.CompilerParams(dimension_semantics=("parallel",), vmem_limit_bytes=VMEM_LIMIT),
    )(x, gf, wg, wu, wd, gl)


def _slab_weight(w_in):
    wq, wkv, wgn, wpr, wmq, wmg = jnp.split(w_in, np.cumsum(
        [NSA_WIDTH, NSA_KV_COLS, 3 * NSA_HEADS, RWKV_PROJ, MEM_WIDTH])[:5].tolist(), axis=1)
    d = w_in.shape[0]
    zeros = lambda n: jnp.zeros((d, n), w_in.dtype)
    w = jnp.concatenate([wmg, wpr, wmq, wq * (HEAD_DIM ** -0.5 * LOG2E), wgn, zeros(COL_KV - COL_GN - 3 * NSA_HEADS),
                         wkv, zeros(SLAB_COLS - COL_KV - NSA_KV_COLS)], axis=1)
    return w.astype(BF16), wkv.astype(BF16), wkv.T.astype(BF16)


def _channel_major_rows(x):
    b, _, t = x.shape
    return jnp.transpose(x.reshape(b, NSA_GROUPS, HEAD_DIM, t), (0, 3, 1, 2))[None]


def _channel_major_view(x):
    b, t = x.shape[:2]
    return jnp.transpose(x, (0, 2, 3, 1)).reshape(b, GROUP_LANES, t)


def kernel(x_prompt, x_sample, cache_cmp_k, cache_cmp_v, cache_slc_k, cache_slc_v, cache_win_k, cache_win_v, state_rwkv_shift, state_rwkv_wkv, cache_mem_k, cache_mem_v, page_table, mem_prompt, attn_norm, w_in, cmp_pe_k, cmp_w1_k, cmp_b1_k, cmp_w2_k, cmp_pe_v, cmp_w1_v, cmp_b1_v, cmp_w2_v, rwkv_mu, rwkv_w0, rwkv_w2, rwkv_a0, rwkv_a2, rwkv_g2, rwkv_kk, rwkv_ka, rwkv_rk, rwkv_ln_g, rwkv_ln_b, mem_norm, w_mem_kv, w_o_nsa, w_o_rwkv, w_o_mem, w_out, ffn_norm, w_gate, w_up, w_down, final_norm):
    assert w_in.shape[0] == 1, "one layer"
    bp, t, d = x_prompt.shape
    bs, tn, _ = x_sample.shape
    row2 = lambda a: a.reshape(1, -1)
    gl_ = GROUP_LANES

    w_slab, w_kv, w_kvt = _slab_weight(w_in[0])
    cmp_wk = _compress_weights(cmp_pe_k[0], cmp_w1_k[0], cmp_b1_k[0], cmp_w2_k[0])
    cmp_wv = _compress_weights(cmp_pe_v[0], cmp_w1_v[0], cmp_b1_v[0], cmp_w2_v[0])
    rw_params = (row2(rwkv_mu[0]), row2(rwkv_w0[0]), rwkv_w2[0].astype(BF16), row2(rwkv_a0[0]),
                 rwkv_a2[0].astype(BF16), rwkv_g2[0].astype(BF16), row2(rwkv_kk[0]), row2(rwkv_ka[0]),
                 row2(rwkv_rk[0]), row2(rwkv_ln_g[0]), row2(rwkv_ln_b[0]))
    wn, wr, wm, wo = (a[0].astype(BF16) for a in (w_o_nsa, w_o_rwkv, w_o_mem, w_out))
    wg, wu, wd = (a[0].astype(BF16) for a in (w_gate, w_up, w_down))
    gf, gl = row2(ffn_norm[0]), row2(final_norm)

    xp2 = x_prompt.reshape(bp * t, d)
    slab, kvt, ktb, vb, ck, cv = proj_prompt(xp2, row2(attn_norm[0]), w_slab[:, :COL_KV], w_kv, w_kvt, bp, t,
                                             1024, 1024)
    slab3 = slab.reshape(bp, t, COL_KV)
    nch = t // CMP_STRIDE
    kc, vc = compress_prompt(ck.reshape(bp, nch, CMP_STRIDE * gl_), cv.reshape(bp, nch, CMP_STRIDE * gl_),
                             cmp_wk, cmp_wv)
    o_nsa = nsa_prompt(slab, kc, vc, ktb, vb.reshape(bp, t, 2 * gl_), bp, t)
    o_rwkv, s_p = rwkv(slab3, jnp.zeros((bp, 1, RWKV_PROJ), F32),
                       jnp.zeros((bp, RWKV_HEADS, RWKV_HEAD_DIM, RWKV_HEAD_DIM), F32), rw_params, 64, BF16)
    mem_n = mem_prompt.shape[1]
    mkv = norm_matmul(mem_prompt.reshape(bp * mem_n, d), row2(mem_norm[0]), w_mem_kv[0].astype(BF16),
                      min(1024, bp * mem_n), 512).reshape(bp, mem_n, 2 * MEM_WIDTH)
    o_mem = mem_attend(slab3, mkv, 0, mkv, 1, 512, BF16)
    x1 = merge(xp2, o_nsa, o_rwkv.reshape(bp * t, RWKV_WIDTH), o_mem.reshape(bp * t, MEM_WIDTH), slab,
               wn, wr, wm, wo, 512)
    y_prompt = ffn(x1, gf, wg, wu, wd, gl, 256).reshape(bp, t, d)

    wp0 = max(t - WINDOW, 0)
    stream = lambda i: kvt[:, i * gl_:(i + 1) * gl_, :]
    p_state = (_channel_major_rows(stream(0)), _channel_major_rows(stream(1)),
               _channel_major_rows(stream(2)), _channel_major_rows(stream(3)),
               _channel_major_rows(stream(4)[:, :, wp0:]), _channel_major_rows(stream(5)[:, :, wp0:]),
               slab3[:, t - 1, COL_PR:COL_PR + RWKV_PROJ][None],
               s_p[None],
               mkv[:, :, :MEM_WIDTH].reshape(1, bp, mem_n, MEM_HEADS, MEM_HEAD_DIM),
               mkv[:, :, MEM_WIDTH:].reshape(1, bp, mem_n, MEM_HEADS, MEM_HEAD_DIM))

    past_len = page_table.shape[1] * cache_cmp_k.shape[2]
    assert past_len % SLC_BLOCK == 0
    assert (past_len + tn) // CMP_STRIDE == past_len // CMP_STRIDE and tn <= TOK_PAD
    xs2 = x_sample.reshape(bs * tn, d)
    slab_s = norm_matmul(xs2, row2(attn_norm[0]), w_slab, bs * tn, 512)
    slab_s3 = slab_s.reshape(bs, tn, SLAB_COLS)
    kc_s, vc_s = compress_sample(_channel_major_view(cache_cmp_k[0]), _channel_major_view(cache_cmp_v[0]),
                                 page_table, cmp_wk, cmp_wv)
    o_nsa_s = nsa_sample(slab_s3, kc_s, vc_s, _channel_major_view(cache_win_k[0]), _channel_major_view(cache_win_v[0]),
                         _channel_major_view(cache_slc_k[0]), _channel_major_view(cache_slc_v[0]),
                         page_table, past_len)
    o_rwkv_s, s_s = rwkv(slab_s3, state_rwkv_shift[0][:, None, :], state_rwkv_wkv[0], rw_params, 16, F32)
    mem_s = cache_mem_k.shape[2]
    o_mem_s = mem_attend(slab_s3, cache_mem_k[0].reshape(bs, mem_s, MEM_WIDTH), 0,
                         cache_mem_v[0].reshape(bs, mem_s, MEM_WIDTH), 0, tn, F32,
                         nseq=8 if bs % 8 == 0 else 1)
    x1s = merge(xs2, o_nsa_s.reshape(bs * tn, NSA_WIDTH), o_rwkv_s.reshape(bs * tn, RWKV_WIDTH),
                o_mem_s.reshape(bs * tn, MEM_WIDTH), slab_s, wn, wr, wm, wo, min(512, bs * tn))
    y_sample = ffn(x1s, gf, wg, wu, wd, gl, min(256, bs * tn)).reshape(bs, tn, d)
    heads = lambda a: a.reshape(1, a.shape[0], a.shape[1], NSA_GROUPS, HEAD_DIM)
    kv_new = [slab_s3[:, :, COL_KV + i * gl_:COL_KV + (i + 1) * gl_] for i in range(6)]
    s_state = (heads(kv_new[0]), heads(kv_new[1]), heads(kv_new[2]), heads(kv_new[3]),
               jnp.concatenate([cache_win_k[0], heads(kv_new[4])[0]], axis=1)[:, tn:][None],
               jnp.concatenate([cache_win_v[0], heads(kv_new[5])[0]], axis=1)[:, tn:][None],
               slab_s3[:, tn - 1, COL_PR:COL_PR + RWKV_PROJ][None],
               s_s[None])
    return (y_prompt, y_sample) + p_state + s_state
```

```python
import functools

import numpy as np
import jax
import jax.numpy as jnp
from jax import lax
from jax.experimental import pallas as pl
from jax.experimental.pallas import tpu as pltpu

F32 = jnp.float32
BF16 = jnp.bfloat16

D_MODEL = 1024
HEAD_DIM = 64
NSA_WIDTH = 768
NSA_HEADS = 12
NSA_GROUPS = 2
NSA_HPG = 6
CMP_LEN = 32
CMP_STRIDE = 16
CMP_HID = 64
SLC_BLOCK = 64
N_SELECT = 16
WINDOW = 512
Q_BLOCK = 64
FORCE_SCORE = 1e4
RWKV_WIDTH = 768
RWKV_HEAD_DIM = 64
RWKV_HEADS = 12
DECAY_LORA = 64
ICL_LORA = 64
GATE_LORA = 128
RWKV_PROJ = 3 * RWKV_WIDTH + DECAY_LORA + ICL_LORA + GATE_LORA
GN_EPS = 64e-5
MEM_HEADS = 4
MEM_WIDTH = 512
MEM_HEAD_DIM = 128
N_BRANCHES = 3
NSA_KV_COLS = 3 * 2 * NSA_GROUPS * HEAD_DIM
GROUP_LANES = NSA_GROUPS * HEAD_DIM
RMS_EPS = 1e-6
NEG = -1e30
LOG2E = 1.4426950408889634

COL_MG = 0
COL_PR = 3072
COL_MQ = 5632
COL_Q = 6144
COL_GN = 6912
COL_KV = 7168
SLAB_COLS = 8192
MAX_PAGES_PER_STEP = 64


def _pages_per_step(n_pages):
    n = min(MAX_PAGES_PER_STEP, n_pages)
    assert n_pages % n == 0
    return n

VMEM_LIMIT = 56 * 1024 * 1024


def _dot(a, b):
    return jnp.dot(a, b, preferred_element_type=F32)


def _dot_nt(a, b):
    return lax.dot_general(a, b, (((1,), (1,)), ((), ())), preferred_element_type=F32)


def _dot_tn(a, b):
    return lax.dot_general(a, b, (((0,), (0,)), ((), ())), preferred_element_type=F32)


def _iota(shape, dim):
    return lax.broadcasted_iota(jnp.int32, shape, dim)


def _eye(n, dtype):
    return (_iota((n, n), 0) == _iota((n, n), 1)).astype(dtype)


def _sigmoid(x):
    return 1.0 / (1.0 + jnp.exp(-x))


def _tanh(x):
    t = jnp.exp(-2.0 * jnp.abs(x))
    r = (1.0 - t) / (1.0 + t)
    return jnp.where(x < 0.0, -r, r)


def _gelu_tanh(x):
    return 0.5 * x * (1.0 + jnp.tanh(np.sqrt(2.0 / np.pi).astype(np.float32) * (x + 0.044715 * (x * x * x))))


def _rms(x, g):
    ms = jnp.mean(x * x, axis=-1, keepdims=True)
    return (x * lax.rsqrt(ms + RMS_EPS)) * g


def _pad_rows(x, n):
    if x.shape[0] == n:
        return x
    return jnp.concatenate([x, jnp.zeros((n - x.shape[0],) + x.shape[1:], x.dtype)], axis=0)


def _norm_matmul_body(x_ref, g_ref, w_ref, o_ref, h_ref):
    @pl.when(pl.program_id(1) == 0)
    def _():
        h_ref[...] = _rms(x_ref[...], g_ref[...]).astype(BF16)

    o_ref[...] = _dot(h_ref[...], w_ref[...])


def norm_matmul(x, g, w, tm, tn):
    n, d = x.shape
    c = w.shape[1]
    return pl.pallas_call(
        _norm_matmul_body,
        grid=(n // tm, c // tn),
        in_specs=[pl.BlockSpec((tm, d), lambda i, j: (i, 0)),
                  pl.BlockSpec((1, d), lambda i, j: (0, 0)),
                  pl.BlockSpec((d, tn), lambda i, j: (0, j))],
        out_specs=pl.BlockSpec((tm, tn), lambda i, j: (i, j)),
        out_shape=jax.ShapeDtypeStruct((n, c), F32),
        scratch_shapes=[pltpu.VMEM((tm, d), BF16)],
        compiler_params=pltpu.CompilerParams(dimension_semantics=("parallel", "arbitrary"),
                                             vmem_limit_bytes=VMEM_LIMIT),
    )(x, g, w)


def _proj_prompt_body(x_ref, g_ref, w_ref, wkv_ref, wkvt_ref, o_ref, kvt_ref, ktb_ref, vb_ref, ck_ref, cv_ref, h_ref):
    @pl.when(pl.program_id(1) == 0)
    def _():
        h = _rms(x_ref[...], g_ref[...]).astype(BF16)
        h_ref[...] = h
        gl = GROUP_LANES
        kvt = _dot_nt(wkvt_ref[...], h)
        kvt_ref[0] = kvt
        ktb_ref[0] = jnp.concatenate([kvt[2 * gl:3 * gl], kvt[4 * gl:5 * gl]], axis=0).astype(BF16)
        kv = _dot(h, wkv_ref[...])
        ck_ref[...] = kv[:, 0:gl].astype(BF16)
        cv_ref[...] = kv[:, gl:2 * gl].astype(BF16)
        vb_ref[...] = jnp.concatenate([kv[:, 3 * gl:4 * gl], kv[:, 5 * gl:6 * gl]], axis=1).astype(BF16)

    o_ref[...] = _dot(h_ref[...], w_ref[...])


def proj_prompt(x, g, w, wkv, wkvt, b, t, tm, tn):
    n, d = x.shape
    c = w.shape[1]
    tpb = t // tm
    gl = GROUP_LANES
    full = lambda a: pl.BlockSpec(a.shape, lambda i, j: (0, 0))
    rows = lambda width: pl.BlockSpec((tm, width), lambda i, j: (i, 0))
    return pl.pallas_call(
        _proj_prompt_body,
        grid=(n // tm, c // tn),
        in_specs=[pl.BlockSpec((tm, d), lambda i, j: (i, 0)), full(g),
                  pl.BlockSpec((d, tn), lambda i, j: (0, j)), full(wkv), full(wkvt)],
        out_specs=[pl.BlockSpec((tm, tn), lambda i, j: (i, j)),
                   pl.BlockSpec((1, NSA_KV_COLS, tm), lambda i, j: (i // tpb, 0, i % tpb)),
                   pl.BlockSpec((1, 2 * gl, tm), lambda i, j: (i // tpb, 0, i % tpb)),
                   rows(2 * gl), rows(gl), rows(gl)],
        out_shape=[jax.ShapeDtypeStruct((n, c), F32),
                   jax.ShapeDtypeStruct((b, NSA_KV_COLS, t), F32),
                   jax.ShapeDtypeStruct((b, 2 * gl, t), BF16),
                   jax.ShapeDtypeStruct((n, 2 * gl), BF16),
                   jax.ShapeDtypeStruct((n, gl), BF16),
                   jax.ShapeDtypeStruct((n, gl), BF16)],
        scratch_shapes=[pltpu.VMEM((tm, d), BF16)],
        compiler_params=pltpu.CompilerParams(dimension_semantics=("parallel", "arbitrary"),
                                             vmem_limit_bytes=VMEM_LIMIT),
    )(x, g, w, wkv, wkvt)


def _compress_consts(pe_ref, b1_ref, w0_ref, w1_ref):
    pe0 = jnp.broadcast_to(pe_ref[0], (8, pe_ref.shape[2])).astype(BF16)
    pe1 = jnp.broadcast_to(pe_ref[1], (8, pe_ref.shape[2])).astype(BF16)
    c = _dot(pe0, w0_ref[...]) + _dot(pe1, w1_ref[...])
    return c[0:1] + b1_ref[...]


def _compress_finish(u0, u1, cst, w2):
    n = u0.shape[0]
    pre = u0 + pltpu.roll(u1, n - 1, 0) + cst
    out = _dot(_gelu_tanh(pre).astype(BF16), w2)
    return jnp.where(_iota(out.shape, 0) < n - 1, out, 0.0)


def _compress_prompt_body(xk_ref, xv_ref, pek_ref, b1k_ref, w0k_ref, w1k_ref, w2k_ref,
                          pev_ref, b1v_ref, w0v_ref, w1v_ref, w2v_ref, ok_ref, ov_ref):
    for x_ref, pe_ref, b1_ref, w0_ref, w1_ref, w2_ref, o_ref in (
            (xk_ref, pek_ref, b1k_ref, w0k_ref, w1k_ref, w2k_ref, ok_ref),
            (xv_ref, pev_ref, b1v_ref, w0v_ref, w1v_ref, w2v_ref, ov_ref)):
        x = x_ref[0].astype(BF16)
        cst = _compress_consts(pe_ref, b1_ref, w0_ref, w1_ref)
        res = _compress_finish(_dot(x, w0_ref[...]), _dot(x, w1_ref[...]), cst, w2_ref[...])
        for g in range(NSA_GROUPS):
            o_ref[0, g] = res[:, g * HEAD_DIM:(g + 1) * HEAD_DIM].astype(o_ref.dtype)


def _compress_weights(pe, w1, b1, w2):
    r = CMP_LEN // CMP_STRIDE
    eye = jnp.eye(NSA_GROUPS, dtype=F32)
    w1r = w1.reshape(r, CMP_STRIDE, HEAD_DIM, CMP_HID)
    w1e = jnp.einsum('icdh,gk->icgdkh', w1r, eye).reshape(r, CMP_STRIDE * NSA_GROUPS * HEAD_DIM,
                                                         NSA_GROUPS * CMP_HID)
    pee = jnp.broadcast_to(pe.reshape(r, CMP_STRIDE, 1, HEAD_DIM), (r, CMP_STRIDE, NSA_GROUPS, HEAD_DIM))
    pee = pee.reshape(r, 1, CMP_STRIDE * NSA_GROUPS * HEAD_DIM)
    b1e = jnp.tile(b1, NSA_GROUPS).reshape(1, NSA_GROUPS * CMP_HID)
    w2e = jnp.einsum('hd,gk->ghkd', w2, eye).reshape(NSA_GROUPS * CMP_HID, NSA_GROUPS * HEAD_DIM)
    return pee, b1e, w1e[0].astype(BF16), w1e[1].astype(BF16), w2e.astype(BF16)


def compress_prompt(xk, xv, wk, wv):
    b, nch, width = xk.shape
    full = lambda a: pl.BlockSpec(a.shape, lambda i: (0,) * a.ndim)
    xspec = pl.BlockSpec((1, nch, width), lambda i: (i, 0, 0))
    ospec = pl.BlockSpec((1, NSA_GROUPS, nch, HEAD_DIM), lambda i: (i, 0, 0, 0))
    oshape = jax.ShapeDtypeStruct((b, NSA_GROUPS, nch, HEAD_DIM), BF16)
    return pl.pallas_call(
        _compress_prompt_body,
        grid=(b,),
        in_specs=[xspec, xspec] + [full(a) for a in wk] + [full(a) for a in wv],
        out_specs=[ospec, ospec],
        out_shape=[oshape, oshape],
        compiler_params=pltpu.CompilerParams(dimension_semantics=("parallel",), vmem_limit_bytes=VMEM_LIMIT),
    )(xk, xv, *wk, *wv)


def _split3(x):
    hi = x.astype(BF16)
    r1 = x - hi.astype(F32)
    mid = r1.astype(BF16)
    lo = (r1 - mid.astype(F32)).astype(BF16)
    return hi, mid, lo


def _select_blocks_lanes(score, n_blocks):
    nq, nbp = score.shape
    cols = [score[:, jp:jp + 1] for jp in range(n_blocks)]
    cnts = []
    for c0 in range(0, nbp, 128):
        blk = score[:, c0:c0 + 128]
        jj = c0 + _iota(blk.shape, 1)
        cnt = jnp.zeros(blk.shape, F32)
        for jp in range(n_blocks):
            ge = jnp.where(cols[jp] >= blk, 1.0, 0.0)
            gt = jnp.where(cols[jp] > blk, 1.0, 0.0)
            if jp < c0:
                cnt = cnt + ge
            elif jp >= c0 + 128:
                cnt = cnt + gt
            else:
                cnt = cnt + jnp.where(jj > jp, ge, gt)
        cnts.append(cnt)
    cnt = jnp.concatenate(cnts, axis=1)
    return jnp.where((cnt < N_SELECT) & (score > -jnp.inf), 1.0, 0.0)


def _select_blocks_unrolled(score):
    nb = score.shape[0]
    rows = [score[jp:jp + 1, :] for jp in range(nb)]
    cnts = []
    for r in range(nb // 8):
        blk = score[8 * r:8 * r + 8]
        jj = 8 * r + _iota(blk.shape, 0)
        cnt = jnp.zeros(blk.shape, F32)
        for jp in range(nb):
            ge = jnp.where(rows[jp] >= blk, 1.0, 0.0)
            gt = jnp.where(rows[jp] > blk, 1.0, 0.0)
            if jp < 8 * r:
                cnt = cnt + ge
            elif jp >= 8 * r + 8:
                cnt = cnt + gt
            else:
                cnt = cnt + jnp.where(jj > jp, ge, gt)
        cnts.append(cnt)
    cnt = jnp.concatenate(cnts, axis=0)
    return jnp.where((cnt < N_SELECT) & (score > -jnp.inf), 1.0, 0.0)


def _masked_softmax_rows(s, valid):
    s = jnp.where(valid, s, NEG)
    m = jnp.max(s, axis=-1, keepdims=True)
    p = jnp.where(valid, jnp.exp2(s - m), 0.0)
    l = jnp.sum(p, axis=-1, keepdims=True)
    return p / jnp.where(l > 0.0, l, 1.0)


def _online_update(carry, s, valid, v, pv=_dot):
    m, l, acc = carry
    s = jnp.where(valid, s, NEG)
    m_new = jnp.maximum(m, jnp.max(s, axis=-1, keepdims=True))
    alpha = jnp.exp2(m - m_new)
    p = jnp.where(valid, jnp.exp2(s - m_new), 0.0)
    l = alpha * l + jnp.sum(p, axis=-1, keepdims=True)
    acc = alpha * acc + pv(p.astype(BF16), v)
    return m_new, l, acc


def _online_update_biased(carry, s, v):
    m, l, acc = carry
    m_new = jnp.maximum(m, jnp.max(s, axis=-1, keepdims=True))
    alpha = jnp.exp2(m - m_new)
    p = jnp.exp2(s - m_new)
    l = alpha * l + jnp.sum(p, axis=-1, keepdims=True)
    acc = alpha * acc + _dot(p.astype(BF16), v)
    return m_new, l, acc


KV_TILE = 1024
WIN_TILE = 640


def _nsa_prompt_body(q_ref, gn_ref, kc_ref, vc_ref, kts_ref, vs_ref, ktw_ref, vw_ref, ovt_ref, o_ref):
    g = pl.program_id(1)
    qb = pl.program_id(2)
    nq = Q_BLOCK
    hpg = NSA_HPG
    rows = hpg * nq
    qf = q_ref[...]
    q2f = jnp.concatenate([qf[:, h * HEAD_DIM:(h + 1) * HEAD_DIM] for h in range(hpg)], axis=0)
    q2 = q2f.astype(BF16)
    zero = jnp.zeros_like(q2f)
    q2w = jnp.where(g == 0, jnp.concatenate([q2f, zero], axis=1), jnp.concatenate([zero, q2f], axis=1)).astype(BF16)
    t_q1 = qb * nq + _iota((nq, 1), 0)
    tile6 = lambda x: jnp.concatenate([x] * hpg, axis=0)
    pick = lambda x: jnp.where(g == 0, x[:, 0:HEAD_DIM], x[:, HEAD_DIM:2 * HEAD_DIM])

    kc = kc_ref[0, 0]
    ncp = kc.shape[0]
    c_end = _iota((nq, ncp), 1) * CMP_STRIDE + (CMP_LEN - 1)
    s_c = _dot_nt(q2, kc) + tile6(jnp.where(c_end <= t_q1, 0.0, NEG))
    e_c = jnp.exp2(s_c - jnp.max(s_c, axis=-1, keepdims=True))
    l_c = jnp.sum(e_c, axis=-1, keepdims=True)
    any_c = tile6(t_q1 >= CMP_LEN - 1)
    p_c = e_c * jnp.where(any_c, 1.0 / l_c, 0.0)
    o_c = _dot(p_c.astype(BF16), vc_ref[0, 0])
    psum = p_c[0:nq]
    for h in range(1, hpg):
        psum = psum + p_c[h * nq:(h + 1) * nq]

    imp_t = sum(_dot_nt(ovt_ref[...], part) for part in _split3(psum))
    j_idx = _iota(imp_t.shape, 0)
    forced = (j_idx == 0) | (j_idx == qb) | (j_idx == qb - 1)
    score = jnp.where(forced, FORCE_SCORE, imp_t)
    sel_t = _select_blocks_unrolled(jnp.where(j_idx <= qb, score, -jnp.inf))
    sel = _dot_nt(_eye(nq, BF16), sel_t.astype(BF16)).astype(BF16)

    w0 = pl.multiple_of(jnp.maximum(qb * nq - WINDOW, 0) // 128 * 128, 128)
    diff = t_q1 - (w0 + _iota((nq, WIN_TILE), 1))
    ok_w = (diff >= 0) & (diff <= WINDOW)
    s_w = _dot(q2w, ktw_ref[0, :, pl.ds(w0, WIN_TILE)]) + tile6(jnp.where(ok_w, 0.0, NEG))
    e_w = jnp.exp2(s_w - jnp.max(s_w, axis=-1, keepdims=True))
    l_w = jnp.sum(e_w, axis=-1, keepdims=True)
    o_w = pick(_dot(e_w.astype(BF16), vw_ref[0, pl.ds(w0, WIN_TILE), :])) / l_w

    bpt = KV_TILE // SLC_BLOCK
    col_blk = _iota((sel.shape[1], KV_TILE), 1) // SLC_BLOCK
    row_blk = _iota((sel.shape[1], KV_TILE), 0)

    def block_mask(kt):
        expand = jnp.where(row_blk == col_blk + kt * bpt, 1.0, 0.0).astype(BF16)
        return _dot(sel, expand) > 0.5

    def scores(kt):
        off = pl.multiple_of(kt * KV_TILE, KV_TILE)
        return _dot(q2w, kts_ref[0, :, pl.ds(off, KV_TILE)])

    def values(kt):
        off = pl.multiple_of(kt * KV_TILE, KV_TILE)
        return vs_ref[0, pl.ds(off, KV_TILE), :]

    nt = qb // bpt
    ok_d = block_mask(nt) & (_iota((nq, KV_TILE), 1) + nt * KV_TILE <= t_q1)
    init = (jnp.full((rows, 1), NEG, F32), jnp.zeros((rows, 1), F32), jnp.zeros((rows, GROUP_LANES), F32))
    carry = _online_update_biased(init, scores(nt) + tile6(jnp.where(ok_d, 0.0, NEG)), values(nt))

    def tile_step(kt, carry):
        return _online_update_biased(carry, scores(kt) + tile6(jnp.where(block_mask(kt), 0.0, NEG)), values(kt))

    _, l_s, acc_s = lax.fori_loop(0, nt, tile_step, carry)
    o_s = pick(acc_s) / l_s

    gates = _sigmoid(gn_ref[...])
    per_group = hpg * 3
    gates = jnp.where(g == 0, gates[:, 0:per_group], gates[:, per_group:2 * per_group])
    outs = []
    for h in range(hpg):
        sl = slice(h * nq, (h + 1) * nq)
        outs.append(gates[:, 3 * h:3 * h + 1] * o_c[sl] + gates[:, 3 * h + 1:3 * h + 2] * o_s[sl]
                    + gates[:, 3 * h + 2:3 * h + 3] * o_w[sl])
    o_ref[...] = jnp.concatenate(outs, axis=1).astype(o_ref.dtype)


def _overlap_t(n_blocks, n_cmp_padded, n_cmp):
    i = np.arange(n_cmp_padded)[None, :] * CMP_STRIDE
    j = np.arange(n_blocks)[:, None] * SLC_BLOCK
    ov = (i < j + SLC_BLOCK) & (i + CMP_LEN > j) & (np.arange(n_cmp_padded)[None, :] < n_cmp)
    return jnp.asarray(ov.astype(np.float32)).astype(BF16)


def nsa_prompt(slab, kc, vc, ktb, vb, b, t):
    nb = t // Q_BLOCK
    gw = NSA_HPG * HEAD_DIM
    gl = GROUP_LANES
    ncp = kc.shape[2]
    ovt = _overlap_t(nb, ncp, ncp - 1)
    kt_spec = lambda k: pl.BlockSpec((1, gl, t), lambda bi, g, qb: (bi, k, 0))
    v_spec = lambda k: pl.BlockSpec((1, t, gl), lambda bi, g, qb: (bi, 0, k))
    cmp_spec = pl.BlockSpec((1, 1, ncp, HEAD_DIM), lambda bi, g, qb: (bi, g, 0, 0))
    return pl.pallas_call(
        _nsa_prompt_body,
        grid=(b, NSA_GROUPS, nb),
        in_specs=[pl.BlockSpec((Q_BLOCK, gw), lambda bi, g, qb: (bi * nb + qb, COL_Q // gw + g)),
                  pl.BlockSpec((Q_BLOCK, 128), lambda bi, g, qb: (bi * nb + qb, COL_GN // 128)),
                  cmp_spec, cmp_spec, kt_spec(0), v_spec(0), kt_spec(1), v_spec(1),
                  pl.BlockSpec(ovt.shape, lambda bi, g, qb: (0, 0))],
        out_specs=pl.BlockSpec((Q_BLOCK, gw), lambda bi, g, qb: (bi * nb + qb, g)),
        out_shape=jax.ShapeDtypeStruct((b * t, NSA_WIDTH), BF16),
        compiler_params=pltpu.CompilerParams(dimension_semantics=("parallel", "parallel", "arbitrary"),
                                             vmem_limit_bytes=VMEM_LIMIT),
    )(slab, slab, kc, vc, ktb, vb, ktb, vb, ovt)


def _compress_sample_body(n, pt_ref, *refs):
    k_pages, v_pages = refs[0:n], refs[n:2 * n]
    (pek_ref, b1k_ref, w0k_ref, w1k_ref, w2k_ref, wck_ref,
     pev_ref, b1v_ref, w0v_ref, w1v_ref, w2v_ref, wcv_ref, ok_ref, ov_ref, uk_ref, uv_ref) = refs[2 * n:]
    j = pl.program_id(1)
    page_rows = k_pages[0].shape[2]
    cpp = page_rows // CMP_STRIDE
    rows = n * cpp
    off = pl.multiple_of(j * rows, rows)
    rp = _iota((page_rows, page_rows), 0)
    perm = (_iota((page_rows, page_rows), 1) == CMP_STRIDE * (rp % cpp) + rp // cpp).astype(BF16)
    for pages, wc_ref, u_ref in ((k_pages, wck_ref, uk_ref), (v_pages, wcv_ref, uv_ref)):
        xp = [_dot_nt(perm, r[0].astype(BF16)) for r in pages]
        x = jnp.concatenate([jnp.concatenate([p[c * cpp:(c + 1) * cpp] for p in xp], axis=0)
                             for c in range(CMP_STRIDE)], axis=1)
        u_ref[pl.ds(off, rows), :] = _dot(x.astype(BF16), wc_ref[...])

    @pl.when(j == pl.num_programs(1) - 1)
    def _():
        half = NSA_GROUPS * CMP_HID
        for pe_ref, b1_ref, w0_ref, w1_ref, w2_ref, u_ref, o_ref in (
                (pek_ref, b1k_ref, w0k_ref, w1k_ref, w2k_ref, uk_ref, ok_ref),
                (pev_ref, b1v_ref, w0v_ref, w1v_ref, w2v_ref, uv_ref, ov_ref)):
            cst = _compress_consts(pe_ref, b1_ref, w0_ref, w1_ref)
            u = u_ref[...]
            res = _compress_finish(u[:, 0:half], u[:, half:2 * half], cst, w2_ref[...])
            for g in range(NSA_GROUPS):
                o_ref[0, g] = res[:, g * HEAD_DIM:(g + 1) * HEAD_DIM].astype(o_ref.dtype)


def compress_sample(pool_k, pool_v, page_table, wk, wv):
    bs, n_pages = page_table.shape
    _, lanes, page_rows = pool_k.shape
    n = _pages_per_step(n_pages)
    nch = n_pages * page_rows // CMP_STRIDE
    wck = jnp.concatenate([wk[2], wk[3]], axis=1)
    wcv = jnp.concatenate([wv[2], wv[3]], axis=1)
    page = lambda k: pl.BlockSpec((1, lanes, page_rows), lambda b, j, pt: (pt[b, n * j + k], 0, 0))
    full = lambda a: pl.BlockSpec(a.shape, lambda b, j, pt: (0,) * a.ndim)
    consts = list(wk) + [wck] + list(wv) + [wcv]
    ospec = pl.BlockSpec((1, NSA_GROUPS, nch, HEAD_DIM), lambda b, j, pt: (b, 0, 0, 0))
    oshape = jax.ShapeDtypeStruct((bs, NSA_GROUPS, nch, HEAD_DIM), BF16)
    return pl.pallas_call(
        functools.partial(_compress_sample_body, n),
        grid_spec=pltpu.PrefetchScalarGridSpec(
            num_scalar_prefetch=1,
            grid=(bs, n_pages // n),
            in_specs=[page(k) for k in range(n)] * 2 + [full(a) for a in consts],
            out_specs=[ospec, ospec],
            scratch_shapes=[pltpu.VMEM((nch, 2 * NSA_GROUPS * CMP_HID), F32)] * 2),
        out_shape=[oshape, oshape],
        compiler_params=pltpu.CompilerParams(dimension_semantics=("parallel", "arbitrary"),
                                             vmem_limit_bytes=VMEM_LIMIT),
    )(page_table, *([pool_k] * n), *([pool_v] * n), *consts)


TOK_PAD = 8


def _nsa_sample_body(past_len, tn, n, pt_ref, *refs):
    q_ref, gn_ref, skn_ref, svn_ref, wkn_ref, wvn_ref, kc_ref, vc_ref, wkc_ref, wvc_ref = refs[0:10]
    k_pages, v_pages = refs[10:10 + n], refs[10 + n:10 + 2 * n]
    ov_ref, o_ref, selt_ref, m_ref, l_ref, acc_ref, oc_ref, ow_ref = refs[10 + 2 * n:]
    j = pl.program_id(1)
    tp = TOK_PAD
    hpg = NSA_HPG
    grows = hpg * tp
    rows = NSA_GROUPS * grows
    gw = hpg * HEAD_DIM
    lanes = GROUP_LANES
    nsp = selt_ref.shape[0]

    q8 = _pad_rows(q_ref[0], tp)
    zero = jnp.zeros((grows, HEAD_DIM), F32)
    q2, q_parts = [], []
    for g in range(NSA_GROUPS):
        qg = jnp.concatenate([q8[:, g * gw + h * HEAD_DIM:g * gw + (h + 1) * HEAD_DIM] for h in range(hpg)], axis=0)
        q2.append(qg.astype(BF16))
        q_parts.append(jnp.concatenate([qg, zero] if g == 0 else [zero, qg], axis=1))
    q_all = jnp.concatenate(q_parts, axis=0).astype(BF16)
    tok = _iota((rows, 1), 0) % tp
    t_q = past_len + tok

    def stack_groups(x):
        return jnp.concatenate([x[g * tp:(g + 1) * tp] for g in range(NSA_GROUPS) for _ in range(hpg)], axis=0)

    def new_keys_valid(width):
        tk = _iota((rows, width), 1)
        return (tk <= tok) & (tk < tn)

    @pl.when(j == 0)
    def _():
        psums = []
        tq_g = t_q[0:grows]
        for g in range(NSA_GROUPS):
            s_c = _dot_nt(q2[g], kc_ref[0, g])
            c_end = _iota(s_c.shape, 1) * CMP_STRIDE + (CMP_LEN - 1)
            p_c = _masked_softmax_rows(s_c, c_end <= tq_g)
            oc_ref[g] = _dot(p_c.astype(BF16), vc_ref[0, g])
            ps = p_c[0:tp]
            for h in range(1, hpg):
                ps = ps + p_c[h * tp:(h + 1) * tp]
            psums.append(ps)
        psum = jnp.concatenate(psums, axis=0)
        imp = sum(_dot(part, ov_ref[...]) for part in _split3(psum))
        j_idx = _iota(imp.shape, 1)
        cur = (past_len + _iota(imp.shape, 0) % tp) // SLC_BLOCK
        forced = (j_idx == 0) | (j_idx == cur) | (j_idx == cur - 1)
        score = jnp.where(j_idx <= cur, jnp.where(forced, FORCE_SCORE, imp), -jnp.inf)
        sel = _select_blocks_lanes(score, (past_len + tn - 1) // SLC_BLOCK + 1)
        pick_row = (_iota((nsp, sel.shape[1]), 0) == _iota((nsp, sel.shape[1]), 1)).astype(BF16)
        selt_ref[...] = _dot_nt(pick_row, sel.astype(BF16))

        lw = wkc_ref.shape[2]
        kwn = _pad_rows(wkn_ref[0], 16).astype(BF16)
        vwn = _pad_rows(wvn_ref[0], 16).astype(BF16)
        diff = t_q - (past_len - lw + _iota((rows, lw), 1))
        carry = (jnp.full((rows, 1), NEG, F32), jnp.zeros((rows, 1), F32), jnp.zeros((rows, lanes), F32))
        carry = _online_update(carry, _dot(q_all, wkc_ref[0].astype(BF16)), (diff >= 0) & (diff <= WINDOW),
                               wvc_ref[0].astype(BF16), pv=_dot_nt)
        _, l_w, acc_w = _online_update(carry, _dot_nt(q_all, kwn), new_keys_valid(16), vwn)
        ow_ref[...] = acc_w / l_w
        m_ref[...] = jnp.full((rows, 1), NEG, F32)
        l_ref[...] = jnp.zeros((rows, 1), F32)
        acc_ref[...] = jnp.zeros((rows, lanes), F32)

    page_rows = k_pages[0].shape[2]
    nk = n * page_rows
    bps = nk // SLC_BLOCK
    kt = jnp.concatenate([r[0] for r in k_pages], axis=1).astype(BF16)
    vt = jnp.concatenate([r[0] for r in v_pages], axis=1).astype(BF16)
    expand = jnp.where(_iota((bps, nk), 0) == _iota((bps, nk), 1) // SLC_BLOCK, 1.0, 0.0).astype(BF16)
    sel_rows = selt_ref[pl.ds(pl.multiple_of(j * bps, bps), bps), :].astype(BF16)
    bias = stack_groups(jnp.where(_dot_tn(sel_rows, expand) > 0.5, 0.0, NEG))
    m, l, acc = (m_ref[...], l_ref[...], acc_ref[...])
    s = _dot(q_all, kt) + bias
    m_new = jnp.maximum(m, jnp.max(s, axis=-1, keepdims=True))
    alpha = jnp.exp2(m - m_new)
    p = jnp.exp2(s - m_new)
    m_ref[...] = m_new
    l_ref[...] = alpha * l + jnp.sum(p, axis=-1, keepdims=True)
    acc_ref[...] = alpha * acc + _dot_nt(p.astype(BF16), vt)

    @pl.when(j == pl.num_programs(1) - 1)
    def _():
        kn = _pad_rows(skn_ref[0], 16).astype(BF16)
        vn = _pad_rows(svn_ref[0], 16).astype(BF16)
        expand_n = jnp.where(_iota((nsp, 16), 0) == (past_len + _iota((nsp, 16), 1)) // SLC_BLOCK,
                             1.0, 0.0).astype(BF16)
        sel_n = stack_groups(_dot_tn(selt_ref[...].astype(BF16), expand_n)) > 0.5
        _, l_s, acc_s = _online_update((m_ref[...], l_ref[...], acc_ref[...]), _dot_nt(q_all, kn),
                                       sel_n & new_keys_valid(16), vn)
        o_s_all = acc_s / l_s
        o_w_all = ow_ref[...]
        gates = _sigmoid(_pad_rows(gn_ref[0], tp))
        outs = []
        for g in range(NSA_GROUPS):
            gsl = slice(g * HEAD_DIM, (g + 1) * HEAD_DIM)
            o_c = oc_ref[g]
            for h in range(hpg):
                sl = slice(h * tp, (h + 1) * tp)
                asl = slice(g * grows + h * tp, g * grows + (h + 1) * tp)
                c0 = (g * hpg + h) * 3
                outs.append(gates[:, c0:c0 + 1] * o_c[sl] + gates[:, c0 + 1:c0 + 2] * o_s_all[asl, gsl]
                            + gates[:, c0 + 2:c0 + 3] * o_w_all[asl, gsl])
        o_ref[0] = jnp.concatenate(outs, axis=1)[0:tn].astype(o_ref.dtype)


def nsa_sample(slab3, kc, vc, win_kt, win_vt, pool_kt, pool_vt, page_table, past_len):
    bs, tn, _ = slab3.shape
    n_pages = page_table.shape[1]
    n = _pages_per_step(n_pages)
    page_rows = pool_kt.shape[2]
    lanes = GROUP_LANES
    ncp = kc.shape[2]
    ns = -(-(past_len + tn) // SLC_BLOCK)
    nsp = -(-ns // 8) * 8
    ov = _overlap_t(-(-ns // 128) * 128, ncp, (past_len + tn) // CMP_STRIDE - CMP_LEN // CMP_STRIDE + 1).T
    rows = NSA_GROUPS * NSA_HPG * TOK_PAD
    tokblk = lambda width, col: pl.BlockSpec((1, tn, width), lambda b, j, pt: (b, 0, col // width))
    cmp_spec = pl.BlockSpec((1, NSA_GROUPS, ncp, HEAD_DIM), lambda b, j, pt: (b, 0, 0, 0))
    win_spec = pl.BlockSpec((1, lanes, win_kt.shape[2]), lambda b, j, pt: (b, 0, 0))
    page = lambda k: pl.BlockSpec((1, lanes, page_rows), lambda b, j, pt: (pt[b, n * j + k], 0, 0))
    return pl.pallas_call(
        functools.partial(_nsa_sample_body, past_len, tn, n),
        grid_spec=pltpu.PrefetchScalarGridSpec(
            num_scalar_prefetch=1,
            grid=(bs, n_pages // n),
            in_specs=[tokblk(NSA_WIDTH, COL_Q), tokblk(128, COL_GN),
                      tokblk(lanes, COL_KV + 2 * lanes), tokblk(lanes, COL_KV + 3 * lanes),
                      tokblk(lanes, COL_KV + 4 * lanes), tokblk(lanes, COL_KV + 5 * lanes),
                      cmp_spec, cmp_spec, win_spec, win_spec]
            + [page(k) for k in range(n)] * 2
            + [pl.BlockSpec(ov.shape, lambda b, j, pt: (0, 0))],
            out_specs=pl.BlockSpec((1, tn, NSA_WIDTH), lambda b, j, pt: (b, 0, 0)),
            scratch_shapes=[pltpu.VMEM((nsp, NSA_GROUPS * TOK_PAD), F32),
                            pltpu.VMEM((rows, 1), F32),
                            pltpu.VMEM((rows, 1), F32),
                            pltpu.VMEM((rows, lanes), F32),
                            pltpu.VMEM((NSA_GROUPS, NSA_HPG * TOK_PAD, HEAD_DIM), F32),
                            pltpu.VMEM((rows, lanes), F32)]),
        out_shape=jax.ShapeDtypeStruct((bs, tn, NSA_WIDTH), F32),
        compiler_params=pltpu.CompilerParams(dimension_semantics=("parallel", "arbitrary"),
                                             vmem_limit_bytes=VMEM_LIMIT),
    )(page_table, slab3, slab3, slab3, slab3, slab3, slab3, kc, vc, win_kt, win_vt,
      *([pool_kt] * n), *([pool_vt] * n), ov)


def _cumsum_rows(x):
    n = x.shape[0]
    row = _iota((n, 1), 0)
    k = 1
    while k < n:
        x = x + jnp.where(row >= k, pltpu.roll(x, k, 0), 0.0)
        k *= 2
    return x


def _rwkv_body(n_valid, chunk, p0_ref, p1_ref, p2_ref, p3_ref, p4_ref, prev_ref, s0_ref,
               mu_ref, w0_ref, w2_ref, a0_ref, a2_ref, g2_ref, kk_ref, ka_ref, rk_ref, lng_ref, lnb_ref,
               o_ref, sout_ref, carry_ref, s_ref):
    c = pl.program_id(1)
    hd = RWKV_HEAD_DIM

    @pl.when(c == 0)
    def _():
        carry_ref[...] = jnp.broadcast_to(prev_ref[0], carry_ref.shape)
        s_ref[...] = s0_ref[0]

    p = jnp.concatenate([r[0] for r in (p0_ref, p1_ref, p2_ref, p3_ref, p4_ref)], axis=1)
    p = _pad_rows(p, chunk)
    row = _iota((chunk, 1), 0)
    valid = row < n_valid
    prev = jnp.where(row == 0, carry_ref[0:1, :], pltpu.roll(p, 1, 0))
    xm = p + (prev - p) * mu_ref[...]
    carry_ref[...] = jnp.broadcast_to(p[n_valid - 1:n_valid, :], carry_ref.shape)

    wdt = RWKV_WIDTH
    r_all, k_all, v_all = xm[:, 0:wdt], xm[:, wdt:2 * wdt], xm[:, 2 * wdt:3 * wdt]
    o = 3 * wdt
    wd, ad, gd = xm[:, o:o + DECAY_LORA], xm[:, o + DECAY_LORA:o + DECAY_LORA + ICL_LORA], \
        xm[:, o + DECAY_LORA + ICL_LORA:o + DECAY_LORA + ICL_LORA + GATE_LORA]
    w = w0_ref[...] + _dot(_tanh(wd).astype(BF16), w2_ref[...])
    logw = -_sigmoid(w) * float(np.exp(-0.5))
    a_all = _sigmoid(a0_ref[...] + _dot(ad.astype(BF16), a2_ref[...]))
    g_all = _dot(_sigmoid(gd).astype(BF16), g2_ref[...])
    logw = jnp.where(valid, logw, 0.0)
    cum = _cumsum_rows(logw)
    total = cum[chunk - 1:chunk, :]
    w_in = jnp.exp(cum)
    w_ex = jnp.exp(cum - logw)
    w_inv = jnp.exp(-cum)
    w_rem = jnp.exp(total - cum)
    w_tot = jnp.exp(total)
    kk_all = k_all * kk_ref[...]
    k2_all = k_all * (1.0 + (a_all - 1.0) * ka_ref[...])

    t_i = _iota((chunk, chunk), 0)
    s_i = _iota((chunk, chunk), 1)
    strict = s_i < t_i
    incl = s_i <= t_i
    n_rounds = int(np.log2(chunk))
    heads = range(RWKV_HEADS)
    sls = [slice(h * hd, (h + 1) * hd) for h in heads]

    lr, bt, kt, bk, vb, at, rt = [], [], [], [], [], [], []
    head_of = (_iota((wdt, 128), 0) // hd == _iota((wdt, 128), 1)).astype(BF16)
    ssq = sum(_dot(part, head_of) for part in _split3(kk_all * kk_all))
    inv = sum(_dot_nt(part, head_of) for part in _split3(lax.rsqrt(jnp.maximum(ssq, 1e-24))))
    kkn_all = jnp.where(valid, kk_all * inv, 0.0)
    k2m_all = jnp.where(valid, k2_all, 0.0)
    vm_all = jnp.where(valid, v_all, 0.0)
    b_all = kkn_all * a_all
    at_all = -kkn_all * w_ex
    rt_all = r_all * w_in
    bt_all = b_all * w_inv
    kt_all = k2m_all * w_inv
    bp_all = b_all * w_rem
    kp_all = k2m_all * w_rem
    for sl in sls:
        at.append(at_all[:, sl].astype(BF16))
        rt.append(rt_all[:, sl].astype(BF16))
        lr.append(jnp.concatenate([at_all[:, sl], rt_all[:, sl]], axis=0).astype(BF16))
        bt.append(bt_all[:, sl].astype(BF16))
        kt.append(kt_all[:, sl].astype(BF16))
        bk.append(jnp.concatenate([bp_all[:, sl], kp_all[:, sl]], axis=0).astype(BF16))
        vb.append(vm_all[:, sl])
    m_b = [_dot_nt(lr[h], bt[h]) for h in heads]
    m_k = [_dot_nt(lr[h], kt[h]) for h in heads]
    a_ab = [jnp.where(strict, m[0:chunk], 0.0) for m in m_b]
    a_rb = [jnp.where(incl, m[chunk:2 * chunk], 0.0).astype(BF16) for m in m_b]
    a_ak = [jnp.where(strict, m[0:chunk], 0.0).astype(BF16) for m in m_k]
    a_rk = [jnp.where(incl, m[chunk:2 * chunk], 0.0).astype(BF16) for m in m_k]
    s0 = [s_ref[h] for h in heads]
    s0b = [x.astype(BF16) for x in s0]
    vbb = [x.astype(BF16) for x in vb]
    u = [_dot_nt(at[h], s0b[h]) + _dot(a_ak[h], vbb[h]) for h in heads]
    pw = a_ab
    for it in range(n_rounds):
        pwb = [x.astype(BF16) for x in pw]
        u = [u[h] + _dot(pwb[h], u[h].astype(BF16)) for h in heads]
        if it + 1 < n_rounds:
            pw = [_dot(x, x) for x in pwb]
    ub = [x.astype(BF16) for x in u]
    y = [_dot_nt(rt[h], s0b[h]) + _dot(a_rb[h], ub[h]) + _dot(a_rk[h], vbb[h]) for h in heads]
    for h in heads:
        uv = jnp.concatenate([u[h], vb[h]], axis=0).astype(BF16)
        s_ref[h] = s0[h] * w_tot[:, sls[h]] + _dot_tn(uv, bk[h])
    outs = []
    for h in heads:
        sl = sls[h]
        mean = jnp.mean(y[h], axis=-1, keepdims=True)
        yc = y[h] - mean
        var = jnp.mean(yc * yc, axis=-1, keepdims=True)
        yn = yc * lax.rsqrt(var + GN_EPS)
        bonus = jnp.sum(r_all[:, sl] * k2_all[:, sl] * rk_ref[:, sl], axis=-1, keepdims=True) * v_all[:, sl]
        outs.append((yn * lng_ref[:, sl] + lnb_ref[:, sl] + bonus) * g_all[:, sl])
    out = jnp.concatenate(outs, axis=1)
    o_ref[0] = out[0:o_ref.shape[1]].astype(o_ref.dtype)

    @pl.when(c == pl.num_programs(1) - 1)
    def _():
        sout_ref[0] = s_ref[...]


def rwkv(slab3, p_prev, s0, params, chunk, out_dtype):
    b, t, _ = slab3.shape
    tc = min(t, chunk)
    nchunks = t // tc
    blk = 512
    pspec = lambda k: pl.BlockSpec((1, tc, blk), lambda bi, c: (bi, c, COL_PR // blk + k))
    full = lambda a: pl.BlockSpec(a.shape, lambda bi, c: (0,) * a.ndim)
    sspec = pl.BlockSpec((1, RWKV_HEADS, RWKV_HEAD_DIM, RWKV_HEAD_DIM), lambda bi, c: (bi, 0, 0, 0))
    return pl.pallas_call(
        functools.partial(_rwkv_body, tc, chunk),
        grid=(b, nchunks),
        in_specs=[pspec(k) for k in range(5)]
        + [pl.BlockSpec((1, 1, RWKV_PROJ), lambda bi, c: (bi, 0, 0)), sspec]
        + [full(a) for a in params],
        out_specs=[pl.BlockSpec((1, tc, RWKV_WIDTH), lambda bi, c: (bi, c, 0)), sspec],
        out_shape=[jax.ShapeDtypeStruct((b, t, RWKV_WIDTH), out_dtype),
                   jax.ShapeDtypeStruct(s0.shape, F32)],
        scratch_shapes=[pltpu.VMEM((8, RWKV_PROJ), F32),
                        pltpu.VMEM((RWKV_HEADS, RWKV_HEAD_DIM, RWKV_HEAD_DIM), F32)],
        compiler_params=pltpu.CompilerParams(dimension_semantics=("parallel", "arbitrary"),
                                             vmem_limit_bytes=VMEM_LIMIT),
    )(slab3, slab3, slab3, slab3, slab3, p_prev, s0, *params)


def _mem_attend_body(q_ref, k_ref, v_ref, o_ref):
    nseq, tm = q_ref.shape[0], q_ref.shape[1]
    for bb in range(nseq):
        q = _pad_rows(q_ref[bb], max(tm, 16)).astype(BF16)
        k = k_ref[bb].astype(BF16)
        v = v_ref[bb].astype(BF16)
        outs = []
        for h in range(MEM_HEADS):
            sl = slice(h * MEM_HEAD_DIM, (h + 1) * MEM_HEAD_DIM)
            s = _dot_nt(q[:, sl], k[:, sl]) * (MEM_HEAD_DIM ** -0.5)
            m = jnp.max(s, axis=-1, keepdims=True)
            p = jnp.exp(s - m)
            p = p / jnp.sum(p, axis=-1, keepdims=True)
            outs.append(_dot(p.astype(BF16), v[:, sl]))
        o_ref[bb] = jnp.concatenate(outs, axis=1)[0:tm].astype(o_ref.dtype)


def mem_attend(slab3, mk, k_blk, mv, v_blk, tm, out_dtype, nseq=1):
    b, t, _ = slab3.shape
    m = mk.shape[1]
    return pl.pallas_call(
        _mem_attend_body,
        grid=(b // nseq, t // tm),
        in_specs=[pl.BlockSpec((nseq, tm, MEM_WIDTH), lambda bi, i: (bi, i, COL_MQ // MEM_WIDTH)),
                  pl.BlockSpec((nseq, m, MEM_WIDTH), lambda bi, i: (bi, 0, k_blk)),
                  pl.BlockSpec((nseq, m, MEM_WIDTH), lambda bi, i: (bi, 0, v_blk))],
        out_specs=pl.BlockSpec((nseq, tm, MEM_WIDTH), lambda bi, i: (bi, i, 0)),
        out_shape=jax.ShapeDtypeStruct((b, t, MEM_WIDTH), out_dtype),
        compiler_params=pltpu.CompilerParams(dimension_semantics=("parallel", "parallel"),
                                             vmem_limit_bytes=VMEM_LIMIT),
    )(slab3, mk, mv)


def _merge_body(x_ref, on_ref, or_ref, om_ref, g0_ref, g1_ref, g2_ref, wn_ref, wr_ref, wm_ref, wo_ref, o_ref):
    m = _sigmoid(g0_ref[...]) * _dot(on_ref[...].astype(BF16), wn_ref[...])
    m = m + _sigmoid(g1_ref[...]) * _dot(or_ref[...].astype(BF16), wr_ref[...])
    m = m + _sigmoid(g2_ref[...]) * _dot(om_ref[...].astype(BF16), wm_ref[...])
    o_ref[...] = x_ref[...] + _dot(m.astype(BF16), wo_ref[...])


def merge(x, o_nsa, o_rwkv, o_mem, slab, wn, wr, wm, wo, tm):
    n, d = x.shape
    row = lambda w: pl.BlockSpec((tm, w), lambda i: (i, 0))
    full = lambda a: pl.BlockSpec(a.shape, lambda i: (0, 0))
    gate = lambda k: pl.BlockSpec((tm, d), lambda i: (i, COL_MG // d + k))
    return pl.pallas_call(
        _merge_body,
        grid=(n // tm,),
        in_specs=[row(d), row(NSA_WIDTH), row(RWKV_WIDTH), row(MEM_WIDTH), gate(0), gate(1), gate(2),
                  full(wn), full(wr), full(wm), full(wo)],
        out_specs=row(d),
        out_shape=jax.ShapeDtypeStruct((n, d), F32),
        compiler_params=pltpu.CompilerParams(dimension_semantics=("parallel",), vmem_limit_bytes=VMEM_LIMIT),
    )(x, o_nsa, o_rwkv, o_mem, slab, slab, slab, wn, wr, wm, wo)


def _ffn_body(x_ref, gf_ref, wg_ref, wu_ref, wd_ref, gl_ref, o_ref):
    x = x_ref[...]
    hf = _rms(x, gf_ref[...]).astype(BF16)
    gate = _dot(hf, wg_ref[...])
    up = _dot(hf, wu_ref[...])
    act = (gate * _sigmoid(gate) * up).astype(BF16)
    x2 = x + _dot(act, wd_ref[...])
    o_ref[...] = _rms(x2, gl_ref[...])


def ffn(x, gf, wg, wu, wd, gl, tm):
    n, d = x.shape
    row = pl.BlockSpec((tm, d), lambda i: (i, 0))
    full = lambda a: pl.BlockSpec(a.shape, lambda i: (0, 0), pipeline_mode=pl.Buffered(1))
    return pl.pallas_call(
        _ffn_body,
        grid=(n // tm,),
        in_specs=[row, full(gf), full(wg), full(wu), full(wd), full(gl)],
        out_specs=row,
        out_shape=jax.ShapeDtypeStruct((n, d), F32),
        compiler_params=pltpu.CompilerParams(dimension_semantics=("parallel",), vmem_limit_bytes=VMEM_LIMIT),
    )(x, gf, wg, wu, wd, gl)


def _slab_weight(w_in):
    wq, wkv, wgn, wpr, wmq, wmg = jnp.split(w_in, np.cumsum(
        [NSA_WIDTH, NSA_KV_COLS, 3 * NSA_HEADS, RWKV_PROJ, MEM_WIDTH])[:5].tolist(), axis=1)
    d = w_in.shape[0]
    zeros = lambda n: jnp.zeros((d, n), w_in.dtype)
    w = jnp.concatenate([wmg, wpr, wmq, wq * (HEAD_DIM ** -0.5 * LOG2E), wgn, zeros(COL_KV - COL_GN - 3 * NSA_HEADS),
                         wkv, zeros(SLAB_COLS - COL_KV - NSA_KV_COLS)], axis=1)
    return w.astype(BF16), wkv.astype(BF16), wkv.T.astype(BF16)


def _channel_major_rows(x):
    b, _, t = x.shape
    return jnp.transpose(x.reshape(b, NSA_GROUPS, HEAD_DIM, t), (0, 3, 1, 2))[None]


def _channel_major_view(x):
    b, t = x.shape[:2]
    return jnp.transpose(x, (0, 2, 3, 1)).reshape(b, GROUP_LANES, t)


def kernel(x_prompt, x_sample, cache_cmp_k, cache_cmp_v, cache_slc_k, cache_slc_v, cache_win_k, cache_win_v, state_rwkv_shift, state_rwkv_wkv, cache_mem_k, cache_mem_v, page_table, mem_prompt, attn_norm, w_in, cmp_pe_k, cmp_w1_k, cmp_b1_k, cmp_w2_k, cmp_pe_v, cmp_w1_v, cmp_b1_v, cmp_w2_v, rwkv_mu, rwkv_w0, rwkv_w2, rwkv_a0, rwkv_a2, rwkv_g2, rwkv_kk, rwkv_ka, rwkv_rk, rwkv_ln_g, rwkv_ln_b, mem_norm, w_mem_kv, w_o_nsa, w_o_rwkv, w_o_mem, w_out, ffn_norm, w_gate, w_up, w_down, final_norm):
    assert w_in.shape[0] == 1, "one layer"
    bp, t, d = x_prompt.shape
    bs, tn, _ = x_sample.shape
    row2 = lambda a: a.reshape(1, -1)
    gl_ = GROUP_LANES

    w_slab, w_kv, w_kvt = _slab_weight(w_in[0])
    cmp_wk = _compress_weights(cmp_pe_k[0], cmp_w1_k[0], cmp_b1_k[0], cmp_w2_k[0])
    cmp_wv = _compress_weights(cmp_pe_v[0], cmp_w1_v[0], cmp_b1_v[0], cmp_w2_v[0])
    rw_params = (row2(rwkv_mu[0]), row2(rwkv_w0[0]), rwkv_w2[0].astype(BF16), row2(rwkv_a0[0]),
                 rwkv_a2[0].astype(BF16), rwkv_g2[0].astype(BF16), row2(rwkv_kk[0]), row2(rwkv_ka[0]),
                 row2(rwkv_rk[0]), row2(rwkv_ln_g[0]), row2(rwkv_ln_b[0]))
    wn, wr, wm, wo = (a[0].astype(BF16) for a in (w_o_nsa, w_o_rwkv, w_o_mem, w_out))
    wg, wu, wd = (a[0].astype(BF16) for a in (w_gate, w_up, w_down))
    gf, gl = row2(ffn_norm[0]), row2(final_norm)

    xp2 = x_prompt.reshape(bp * t, d)
    slab, kvt, ktb, vb, ck, cv = proj_prompt(xp2, row2(attn_norm[0]), w_slab[:, :COL_KV], w_kv, w_kvt, bp, t,
                                             1024, 1024)
    slab3 = slab.reshape(bp, t, COL_KV)
    nch = t // CMP_STRIDE
    kc, vc = compress_prompt(ck.reshape(bp, nch, CMP_STRIDE * gl_), cv.reshape(bp, nch, CMP_STRIDE * gl_),
                             cmp_wk, cmp_wv)
    o_nsa = nsa_prompt(slab, kc, vc, ktb, vb.reshape(bp, t, 2 * gl_), bp, t)
    o_rwkv, s_p = rwkv(slab3, jnp.zeros((bp, 1, RWKV_PROJ), F32),
                       jnp.zeros((bp, RWKV_HEADS, RWKV_HEAD_DIM, RWKV_HEAD_DIM), F32), rw_params, 64, BF16)
    mem_n = mem_prompt.shape[1]
    mkv = norm_matmul(mem_prompt.reshape(bp * mem_n, d), row2(mem_norm[0]), w_mem_kv[0].astype(BF16),
                      min(1024, bp * mem_n), 512).reshape(bp, mem_n, 2 * MEM_WIDTH)
    o_mem = mem_attend(slab3, mkv, 0, mkv, 1, 512, BF16)
    x1 = merge(xp2, o_nsa, o_rwkv.reshape(bp * t, RWKV_WIDTH), o_mem.reshape(bp * t, MEM_WIDTH), slab,
               wn, wr, wm, wo, 512)
    y_prompt = ffn(x1, gf, wg, wu, wd, gl, 256).reshape(bp, t, d)

    wp0 = max(t - WINDOW, 0)
    stream = lambda i: kvt[:, i * gl_:(i + 1) * gl_, :]
    p_state = (_channel_major_rows(stream(0)), _channel_major_rows(stream(1)),
               _channel_major_rows(stream(2)), _channel_major_rows(stream(3)),
               _channel_major_rows(stream(4)[:, :, wp0:]), _channel_major_rows(stream(5)[:, :, wp0:]),
               slab3[:, t - 1, COL_PR:COL_PR + RWKV_PROJ][None],
               s_p[None],
               mkv[:, :, :MEM_WIDTH].reshape(1, bp, mem_n, MEM_HEADS, MEM_HEAD_DIM),
               mkv[:, :, MEM_WIDTH:].reshape(1, bp, mem_n, MEM_HEADS, MEM_HEAD_DIM))

    past_len = page_table.shape[1] * cache_cmp_k.shape[2]
    assert past_len % SLC_BLOCK == 0
    assert (past_len + tn) // CMP_STRIDE == past_len // CMP_STRIDE and tn <= TOK_PAD
    xs2 = x_sample.reshape(bs * tn, d)
    slab_s = norm_matmul(xs2, row2(attn_norm[0]), w_slab, bs * tn, 512)
    slab_s3 = slab_s.reshape(bs, tn, SLAB_COLS)
    kc_s, vc_s = compress_sample(_channel_major_view(cache_cmp_k[0]), _channel_major_view(cache_cmp_v[0]),
                                 page_table, cmp_wk, cmp_wv)
    o_nsa_s = nsa_sample(slab_s3, kc_s, vc_s, _channel_major_view(cache_win_k[0]), _channel_major_view(cache_win_v[0]),
                         _channel_major_view(cache_slc_k[0]), _channel_major_view(cache_slc_v[0]),
                         page_table, past_len)
    o_rwkv_s, s_s = rwkv(slab_s3, state_rwkv_shift[0][:, None, :], state_rwkv_wkv[0], rw_params, 16, F32)
    mem_s = cache_mem_k.shape[2]
    o_mem_s = mem_attend(slab_s3, cache_mem_k[0].reshape(bs, mem_s, MEM_WIDTH), 0,
                         cache_mem_v[0].reshape(bs, mem_s, MEM_WIDTH), 0, tn, F32,
                         nseq=8 if bs % 8 == 0 else 1)
    x1s = merge(xs2, o_nsa_s.reshape(bs * tn, NSA_WIDTH), o_rwkv_s.reshape(bs * tn, RWKV_WIDTH),
                o_mem_s.reshape(bs * tn, MEM_WIDTH), slab_s, wn, wr, wm, wo, min(512, bs * tn))
    y_sample = ffn(x1s, gf, wg, wu, wd, gl, min(256, bs * tn)).reshape(bs, tn, d)
    heads = lambda a: a.reshape(1, a.shape[0], a.shape[1], NSA_GROUPS, HEAD_DIM)
    kv_new = [slab_s3[:, :, COL_KV + i * gl_:COL_KV + (i + 1) * gl_] for i in range(6)]
    s_state = (heads(kv_new[0]), heads(kv_new[1]), heads(kv_new[2]), heads(kv_new[3]),
               jnp.concatenate([cache_win_k[0], heads(kv_new[4])[0]], axis=1)[:, tn:][None],
               jnp.concatenate([cache_win_v[0], heads(kv_new[5])[0]], axis=1)[:, tn:][None],
               slab_s3[:, tn - 1, COL_PR:COL_PR + RWKV_PROJ][None],
               s_s[None])
    return (y_prompt, y_sample) + p_state + s_state
```
